```python
import math
import jax, jax.numpy as jnp
from jax import lax
import numpy as np

D_MODEL = 4096
BATCH = 4
SEQ = 2048
DEPTH = 1
DEC_BATCH = 128
DEC_SEQ = 8
PAST_LEN = 16384
PAGE_SIZE = 128

HM = 4
DK_M = 256
DV_M = 512
HR = 8
DK_R = 128
DV_R = 256
MIX = HM * DV_M + HR * DV_R
D_FF = 4 * D_MODEL
CHUNK = 64
GATE_SOFTCAP = 15.0
ROPE_BASE = 10000.0
EPS = 1e-6
COLS = (HM * DK_M, HM * DK_M, HM * DV_M, HM * DV_M, HM, HM,
        HR * DK_R, HR * DK_R, HR * DV_R, HR * DV_R)
IN_COLS = sum(COLS)

kernel_name = "hymba_mlstm_retention_decode_step"


def rmsnorm(x, g):
    x32 = x.astype(jnp.float32)
    y = x32 * lax.rsqrt(jnp.mean(x32 * x32, axis=-1, keepdims=True) + EPS)
    return (y * g.astype(jnp.float32)).astype(x.dtype)


def head_rmsnorm(h, g):
    y = h * lax.rsqrt(jnp.mean(h * h, axis=-1, keepdims=True) + EPS)
    return y * g.astype(jnp.float32)


def softcap(z):
    return GATE_SOFTCAP * jnp.tanh(z / GATE_SOFTCAP)


def rotary(x, pos):
    d = x.shape[-1]
    freqs = ROPE_BASE ** (-jnp.arange(0, d, 2, dtype=jnp.float32) / d)
    ang = pos[:, None] * freqs[None, :]
    cos = jnp.cos(ang)[None, :, None, :]
    sin = jnp.sin(ang)[None, :, None, :]
    x1, x2 = x[..., : d // 2], x[..., d // 2:]
    return jnp.concatenate([x1 * cos - x2 * sin, x1 * sin + x2 * cos], axis=-1)


def to_chunks(a, L):
    B, T = a.shape[:2]
    a = a.reshape((B, T // L, L) + a.shape[2:])
    if a.ndim == 5:
        return a.transpose(1, 0, 3, 2, 4)
    return a.transpose(1, 0, 3, 2)


def from_chunks(a):
    NC, B, H, L, d = a.shape
    return a.transpose(1, 0, 3, 2, 4).reshape(B, NC * L, H, d)


def mlstm_chunked(q, k, v, logi, logf, C0, n0, m0):
    T = q.shape[1]
    L = math.gcd(T, CHUNK)
    causal = jnp.tril(jnp.ones((L, L), dtype=bool))

    def step(carry, inp):
        C, n, m = carry
        qc, kc, vc, ic, fc = inp
        b = jnp.cumsum(fc, axis=-1)
        dlog = b[..., :, None] - b[..., None, :] + ic[..., None, :]
        dlog = jnp.where(causal, dlog, -jnp.inf)
        inter = b + m[..., None]
        mt = jnp.maximum(inter, jnp.max(dlog, axis=-1))
        dw = jnp.exp(dlog - mt[..., None])
        iw = jnp.exp(inter - mt)
        s = jnp.einsum('bhld,bhsd->bhls', qc, kc) * dw
        num = iw[..., None] * jnp.einsum('bhld,bhde->bhle', qc, C) + jnp.einsum('bhls,bhse->bhle', s, vc)
        den = iw * jnp.einsum('bhld,bhd->bhl', qc, n) + jnp.sum(s, axis=-1)
        h = num / jnp.maximum(jnp.abs(den), jnp.exp(-mt))[..., None]
        bL = b[..., -1]
        wlog = bL[..., None] - b + ic
        m_new = jnp.maximum(bL + m, jnp.max(wlog, axis=-1))
        decay = jnp.exp(bL + m - m_new)
        w = jnp.exp(wlog - m_new[..., None])
        C_new = decay[..., None, None] * C + jnp.einsum('bhs,bhsd,bhse->bhde', w, kc, vc)
        n_new = decay[..., None] * n + jnp.einsum('bhs,bhsd->bhd', w, kc)
        return (C_new, n_new, m_new), h

    xs = (to_chunks(q, L), to_chunks(k, L), to_chunks(v, L), to_chunks(logi, L), to_chunks(logf, L))
    (C, n, m), h = lax.scan(step, (C0, n0, m0), xs)
    return from_chunks(h), C, n, m


def retention_chunked(q, k, v, S0):
    T = q.shape[1]
    L = math.gcd(T, CHUNK)
    lg = jnp.log(1.0 - 2.0 ** (-5.0 - jnp.arange(HR, dtype=jnp.float32)))
    idx = jnp.arange(L, dtype=jnp.float32)
    diff = idx[:, None] - idx[None, :]
    dmat = jnp.where(diff >= 0, jnp.exp(jnp.maximum(diff, 0.0)[None] * lg[:, None, None]), 0.0)
    inter = jnp.exp((idx[None, :] + 1.0) * lg[:, None])
    kdec = jnp.exp((L - 1.0 - idx[None, :]) * lg[:, None])
    sdec = jnp.exp(L * lg)

    def step(S, inp):
        qc, kc, vc = inp
        s = jnp.einsum('bhld,bhsd->bhls', qc, kc) * dmat
        o = jnp.einsum('bhls,bhse->bhle', s, vc) + inter[..., None] * jnp.einsum('bhld,bhde->bhle', qc, S)
        S_new = sdec[:, None, None] * S + jnp.einsum('hs,bhsd,bhse->bhde', kdec, kc, vc)
        return S_new, o

    xs = (to_chunks(q, L), to_chunks(k, L), to_chunks(v, L))
    S, o = lax.scan(step, S0, xs)
    return from_chunks(o), S


def layer(x, C0, n0, m0, S0, pos0, w_in, b_igate, b_fgate, g_mlstm_head, g_ret_head,
          w_out, g_norm_mix, g_norm_ffn, w_up, w_down):
    B, T, _ = x.shape
    f32 = jnp.float32
    h = rmsnorm(x, g_norm_mix)
    proj = jnp.einsum('btd,dc->btc', h, w_in).astype(f32)
    splits = np.cumsum(COLS)[:-1].tolist()
    qm, km, vm, om, ig, fg, qr, kr, vr, gr = jnp.split(proj, splits, axis=-1)

    qm = qm.reshape(B, T, HM, DK_M) * (DK_M ** -0.5)
    km = km.reshape(B, T, HM, DK_M)
    vm = vm.reshape(B, T, HM, DV_M)
    logi = softcap(ig + b_igate.astype(f32))
    logf = jax.nn.log_sigmoid(softcap(fg + b_fgate.astype(f32)))
    hm, C, n, m = mlstm_chunked(qm, km, vm, logi, logf,
                                C0.astype(f32), n0.astype(f32), m0.astype(f32))
    hm = head_rmsnorm(hm, g_mlstm_head) * jax.nn.sigmoid(om.reshape(B, T, HM, DV_M))

    pos = jnp.arange(T, dtype=f32) + pos0
    qr = rotary(qr.reshape(B, T, HR, DK_R), pos)
    kr = rotary(kr.reshape(B, T, HR, DK_R), pos) * (DK_R ** -0.5)
    vr = vr.reshape(B, T, HR, DV_R)
    hr, S = retention_chunked(qr, kr, vr, S0.astype(f32))
    hr = head_rmsnorm(hr, g_ret_head) * jax.nn.silu(gr.reshape(B, T, HR, DV_R))

    cat = jnp.concatenate([hm.reshape(B, T, HM * DV_M), hr.reshape(B, T, HR * DV_R)], axis=-1)
    x = x + jnp.einsum('btc,cd->btd', cat.astype(x.dtype), w_out)

    u = jnp.einsum('btd,df->btf', rmsnorm(x, g_norm_ffn), w_up)
    x = x + jnp.einsum('btf,fd->btd', jnp.square(jax.nn.relu(u)), w_down)
    return x, C, n, m, S


def setup_inputs(seed: int = 0) -> dict:
    key = jax.random.key(seed)
    ks = jax.random.split(key, 20)
    nrm = jax.random.normal
    f32 = jnp.float32
    return {
        "x_prompt": nrm(ks[0], (BATCH, SEQ, D_MODEL), f32),
        "x_sample": nrm(ks[1], (DEC_BATCH, DEC_SEQ, D_MODEL), f32),
        "state_mlstm_C": 0.5 * nrm(ks[2], (DEPTH, DEC_BATCH, HM, DK_M, DV_M), f32),
        "state_mlstm_n": 0.5 * nrm(ks[3], (DEPTH, DEC_BATCH, HM, DK_M), f32),
        "state_mlstm_m": nrm(ks[4], (DEPTH, DEC_BATCH, HM), f32),
        "state_ret_S": 0.5 * nrm(ks[5], (DEPTH, DEC_BATCH, HR, DK_R, DV_R), f32),
        "w_in": nrm(ks[6], (DEPTH, D_MODEL, IN_COLS), f32) * D_MODEL ** -0.5,
        "b_igate": 0.1 * nrm(ks[7], (DEPTH, HM), f32),
        "b_fgate": jnp.linspace(3.0, 6.0, HM, dtype=f32)[None] + 0.1 * nrm(ks[8], (DEPTH, HM), f32),
        "g_mlstm_head": 1.0 + 0.02 * nrm(ks[9], (DEPTH, HM, DV_M), f32),
        "g_ret_head": 1.0 + 0.02 * nrm(ks[10], (DEPTH, HR, DV_R), f32),
        "w_out": nrm(ks[11], (DEPTH, MIX, D_MODEL), f32) * MIX ** -0.5,
        "g_norm_mix": 1.0 + 0.02 * nrm(ks[12], (DEPTH, D_MODEL), f32),
        "g_norm_ffn": 1.0 + 0.02 * nrm(ks[13], (DEPTH, D_MODEL), f32),
        "w_up": nrm(ks[14], (DEPTH, D_MODEL, D_FF), f32) * D_MODEL ** -0.5,
        "w_down": nrm(ks[15], (DEPTH, D_FF, D_MODEL), f32) * D_FF ** -0.5,
        "g_final": 1.0 + 0.02 * nrm(ks[16], (D_MODEL,), f32),
    }


def reference(x_prompt, x_sample, state_mlstm_C, state_mlstm_n, state_mlstm_m, state_ret_S,
              w_in, b_igate, b_fgate, g_mlstm_head, g_ret_head, w_out, g_norm_mix, g_norm_ffn,
              w_up, w_down, g_final):
    f32 = jnp.float32
    B = x_prompt.shape[0]
    yp, ys = x_prompt, x_sample
    pC, pn, pm, pS, sC, sn, sm, sS = [], [], [], [], [], [], [], []
    for l in range(DEPTH):
        w = (w_in[l], b_igate[l], b_fgate[l], g_mlstm_head[l], g_ret_head[l], w_out[l],
             g_norm_mix[l], g_norm_ffn[l], w_up[l], w_down[l])
        yp, c, n, m, s = layer(yp, jnp.zeros((B, HM, DK_M, DV_M), f32), jnp.zeros((B, HM, DK_M), f32),
                               jnp.zeros((B, HM), f32), jnp.zeros((B, HR, DK_R, DV_R), f32), 0.0, *w)
        pC.append(c); pn.append(n); pm.append(m); pS.append(s)
        ys, c, n, m, s = layer(ys, state_mlstm_C[l], state_mlstm_n[l], state_mlstm_m[l], state_ret_S[l],
                               float(PAST_LEN), *w)
        sC.append(c); sn.append(n); sm.append(m); sS.append(s)
    y_prompt = rmsnorm(yp, g_final)
    y_sample = rmsnorm(ys, g_final)
    return (y_prompt, y_sample,
            jnp.stack(pC), jnp.stack(pn), jnp.stack(pm), jnp.stack(pS),
            jnp.stack(sC), jnp.stack(sn), jnp.stack(sm), jnp.stack(sS))
```

```python
import functools
import math

import jax
import jax.numpy as jnp
import numpy as np
from jax import lax
from jax.experimental import pallas as pl
from jax.experimental.pallas import tpu as pltpu

F32 = jnp.float32
BF16 = jnp.bfloat16

D_MODEL = 4096
HM, DK_M, DV_M = 4, 256, 512
HR, DK_R, DV_R = 8, 128, 256
D_FF = 4 * D_MODEL
PAST_LEN = 16384
GATE_SOFTCAP = 15.0
ROPE_BASE = 10000.0
EPS = 1e-6

LANES = 128
SUBLANES = 8
BF16_ROWS = 16
VMEM_LIMIT = 60 * 1024 * 1024

N_MAIN = 2 * HM * DK_M + 2 * HM * DV_M + 2 * HR * DK_R + 2 * HR * DV_R
GATE_OFF = 2 * HM * DK_M + 2 * HM * DV_M
OFF_QM, OFF_KM, OFF_VM, OFF_OM = 0, HM * DK_M, 2 * HM * DK_M, 2 * HM * DK_M + HM * DV_M
OFF_QR = GATE_OFF
OFF_KR = OFF_QR + HR * DK_R
OFF_VR = OFF_KR + HR * DK_R
OFF_GR = OFF_VR + HR * DV_R

PROMPT_CHUNK = 256
SAMPLE_ROWS = 128


def _params(sem):
    return pltpu.CompilerParams(dimension_semantics=sem, vmem_limit_bytes=VMEM_LIMIT)


def _sigmoid(x):
    return 1.0 / (1.0 + jnp.exp(-x))


def _rms(x, g):
    return x * lax.rsqrt(jnp.mean(x * x, axis=-1, keepdims=True) + EPS) * g


def _norm_gates_kernel(x_ref, g_ref, wg_ref, bias_ref, h_ref, gc_ref, gr_ref):
    hb = _rms(x_ref[...], g_ref[...]).astype(BF16)
    h_ref[...] = hb
    z = jnp.dot(hb, wg_ref[...], preferred_element_type=F32) + bias_ref[...]
    zc = GATE_SOFTCAP * jnp.tanh(z / GATE_SOFTCAP)
    logsig = jnp.minimum(zc, 0.0) - jnp.log1p(jnp.exp(-jnp.abs(zc)))
    lane = lax.broadcasted_iota(jnp.int32, zc.shape, 1)
    out = jnp.where(lane >= HM, logsig, zc)
    gc_ref[...] = out
    gr_ref[...] = out.T[:SUBLANES, :]


def _norm_gates(x2d, g, wg, bias, bm=512):
    m = x2d.shape[0]
    return pl.pallas_call(
        _norm_gates_kernel,
        grid=(m // bm,),
        in_specs=[
            pl.BlockSpec((bm, D_MODEL), lambda i: (i, 0)),
            pl.BlockSpec((1, D_MODEL), lambda i: (0, 0)),
            pl.BlockSpec((D_MODEL, LANES), lambda i: (0, 0)),
            pl.BlockSpec((1, LANES), lambda i: (0, 0)),
        ],
        out_specs=[
            pl.BlockSpec((bm, D_MODEL), lambda i: (i, 0)),
            pl.BlockSpec((bm, LANES), lambda i: (i, 0)),
            pl.BlockSpec((SUBLANES, bm), lambda i: (0, i)),
        ],
        out_shape=[
            jax.ShapeDtypeStruct((m, D_MODEL), BF16),
            jax.ShapeDtypeStruct((m, LANES), F32),
            jax.ShapeDtypeStruct((SUBLANES, m), F32),
        ],
        compiler_params=_params(("parallel",)),
        name="norm_gates",
    )(x2d, g, wg, bias)


def _in_proj_kernel(a_ref, w_ref, o_ref):
    o_ref[...] = jnp.dot(a_ref[...], w_ref[...], preferred_element_type=F32)


def _in_proj(h, w, bm=1024, bn=1024):
    m, k = h.shape
    n = w.shape[1]
    return pl.pallas_call(
        _in_proj_kernel,
        grid=(n // bn, m // bm),
        in_specs=[
            pl.BlockSpec((bm, k), lambda j, i: (i, 0)),
            pl.BlockSpec((k, bn), lambda j, i: (0, j)),
        ],
        out_specs=pl.BlockSpec((bm, bn), lambda j, i: (i, j)),
        out_shape=jax.ShapeDtypeStruct((m, n), F32),
        compiler_params=_params(("parallel", "parallel")),
        name="in_proj",
    )(h, w)


def _segment_masks(rows, nseg):
    ri = lax.broadcasted_iota(jnp.int32, (rows, rows), 0)
    ci = lax.broadcasted_iota(jnp.int32, (rows, rows), 1)
    if nseg == 1:
        same = None
        causal = ci <= ri
        upper = ri <= ci
    else:
        shift = int(math.log2(rows // nseg))
        same = lax.shift_right_logical(ri, shift) == lax.shift_right_logical(ci, shift)
        causal = (ci <= ri) & same
        upper = (ri <= ci) & same
    return same, causal, upper


def _state_dot(qb, state_bf16, nseg):
    if nseg == 1:
        return jnp.dot(qb, state_bf16[0], preferred_element_type=F32)
    rps = qb.shape[0] // nseg
    assert 2 * rps == BF16_ROWS
    outs = []
    for p in range(nseg // 2):
        qp = qb[p * BF16_ROWS:(p + 1) * BF16_ROWS]
        outs.append(jnp.dot(qp, state_bf16[2 * p], preferred_element_type=F32)[:rps])
        outs.append(jnp.dot(qp, state_bf16[2 * p + 1], preferred_element_type=F32)[rps:])
    return jnp.concatenate(outs, axis=0)


def _segment_lhs(xt, j, nseg):
    if nseg == 1:
        return xt.astype(BF16)
    rps = xt.shape[1] // nseg
    lane = lax.broadcasted_iota(jnp.int32, xt.shape, 1)
    keep = (lane >= j * rps) & (lane < (j + 1) * rps)
    return jnp.where(keep, xt, 0.0).astype(BF16)


def _mlstm_kernel(*refs, rows, nseg, has_init):
    if has_init:
        (q_ref, k_ref, v_ref, om_ref, gc_ref, gr_ref, gh_ref, m0_ref, c0_ref, n0_ref,
         h_ref, c_ref, n_ref, m_ref) = refs
    else:
        (q_ref, k_ref, v_ref, om_ref, gc_ref, gr_ref, gh_ref,
         h_ref, c_ref, n_ref, m_ref) = refs
    head = pl.program_id(1)
    rps = rows // nseg
    lane = lax.broadcasted_iota(jnp.int32, (rows, LANES), 1)
    sub = lax.broadcasted_iota(jnp.int32, (SUBLANES, rows), 0)

    def pick_col(x, idx):
        return jnp.sum(jnp.where(lane == idx, x, 0.0), axis=1, keepdims=True)

    def pick_row(x, idx):
        return jnp.sum(jnp.where(sub == idx, x, 0.0), axis=0, keepdims=True)

    if has_init:
        c_prev, n_prev = c0_ref, n0_ref
        m_col = pick_col(m0_ref[...], head)
    else:
        @pl.when(pl.program_id(2) == 0)
        def _zero_state():
            c_ref[...] = jnp.zeros_like(c_ref)
            n_ref[...] = jnp.zeros_like(n_ref)
            m_ref[...] = jnp.zeros_like(m_ref)
        c_prev, n_prev = c_ref, n_ref
        m_col = m_ref[:, 0:1]

    gc = gc_ref[...]
    gr = gr_ref[...]
    i_col, f_col = pick_col(gc, head), pick_col(gc, head + HM)
    i_row, f_row = pick_row(gr, head), pick_row(gr, head + HM)

    same, causal, upper = _segment_masks(rows, nseg)
    b_col = jnp.sum(jnp.where(causal, f_row, 0.0), axis=1, keepdims=True)
    b_row = jnp.sum(jnp.where(upper, f_col, 0.0), axis=0, keepdims=True)
    if same is None:
        bl_col = jnp.sum(f_row, axis=1, keepdims=True) + jnp.zeros_like(b_col)
        bl_row = jnp.sum(f_col, axis=0, keepdims=True) + jnp.zeros_like(b_row)
    else:
        bl_col = jnp.sum(jnp.where(same, f_row, 0.0), axis=1, keepdims=True)
        bl_row = jnp.sum(jnp.where(same, f_col, 0.0), axis=0, keepdims=True)

    dlog = b_col - b_row + i_row
    dmax = jnp.max(jnp.where(causal, dlog, -jnp.inf), axis=1, keepdims=True)
    inter = b_col + m_col
    mt = jnp.maximum(inter, dmax)
    dw = jnp.where(causal, jnp.exp(dlog - mt), 0.0)
    iw = jnp.exp(inter - mt)

    q = q_ref[...] * (DK_M ** -0.5)
    k = k_ref[...]
    qb, kb, vb = q.astype(BF16), k.astype(BF16), v_ref[...].astype(BF16)
    s = lax.dot_general(qb, kb, (((1,), (1,)), ((), ())), preferred_element_type=F32) * dw
    intra = jnp.dot(s.astype(BF16), vb, preferred_element_type=F32)
    q_c = _state_dot(qb, [c_prev[j].astype(BF16) for j in range(nseg)], nseg)
    n_tok = jnp.broadcast_to(n_prev[...], (nseg, rps, DK_M)).reshape(rows, DK_M)
    q_n = jnp.sum(q * n_tok, axis=1, keepdims=True)
    num = iw * q_c + intra
    den = iw * q_n + jnp.sum(s, axis=1, keepdims=True)
    hh = num * (1.0 / jnp.maximum(jnp.abs(den), jnp.exp(-mt)))
    h_ref[...] = (_rms(hh, gh_ref[...]) * _sigmoid(om_ref[...])).astype(BF16)

    wlog_col = bl_col - b_col + i_col
    wlog_row = bl_row - b_row + i_row
    if same is None:
        wmax = jnp.max(wlog_row, axis=1, keepdims=True) + jnp.zeros_like(b_col)
    else:
        wmax = jnp.max(jnp.where(same, wlog_row, -jnp.inf), axis=1, keepdims=True)
    m_new = jnp.maximum(bl_col + m_col, wmax)
    decay = jnp.exp(bl_col + m_col - m_new)
    kw = k * jnp.exp(wlog_col - m_new)
    kw_t = kw.T
    for j in range(nseg):
        d_j = decay[j * rps:j * rps + 1, :]
        upd = jnp.dot(_segment_lhs(kw_t, j, nseg), vb, preferred_element_type=F32)
        c_ref[j] = d_j * c_prev[j] + upd
        n_ref[j] = d_j * n_prev[j] + jnp.sum(kw[j * rps:(j + 1) * rps], axis=0, keepdims=True)
    m_ref[...] = jnp.broadcast_to(m_new, (rows, LANES))


def _mlstm(proj, gc, gr, g_head, rows, nseg, nchunks, init=None):
    m = proj.shape[0]
    groups = m // (rows * nchunks)
    has_init = init is not None
    assert not has_init or nchunks == 1
    row = lambda g, h, c: g * nchunks + c
    qk_blk, v_blk = DK_M, DV_M
    in_specs = [
        pl.BlockSpec((rows, DK_M), lambda g, h, c: (row(g, h, c), OFF_QM // qk_blk + h)),
        pl.BlockSpec((rows, DK_M), lambda g, h, c: (row(g, h, c), OFF_KM // qk_blk + h)),
        pl.BlockSpec((rows, DV_M), lambda g, h, c: (row(g, h, c), OFF_VM // v_blk + h)),
        pl.BlockSpec((rows, DV_M), lambda g, h, c: (row(g, h, c), OFF_OM // v_blk + h)),
        pl.BlockSpec((rows, LANES), lambda g, h, c: (row(g, h, c), 0)),
        pl.BlockSpec((SUBLANES, rows), lambda g, h, c: (0, row(g, h, c))),
        pl.BlockSpec((1, DV_M), lambda g, h, c: (0, h)),
    ]
    args = [proj, proj, proj, proj, gc, gr, g_head]
    if has_init:
        m0_tok, c0, n0 = init
        in_specs += [
            pl.BlockSpec((rows, LANES), lambda g, h, c: (g, 0)),
            pl.BlockSpec((nseg, None, DK_M, DV_M), lambda g, h, c: (g, h, 0, 0)),
            pl.BlockSpec((nseg, None, 1, DK_M), lambda g, h, c: (g, h, 0, 0)),
        ]
        args += [m0_tok, c0, n0]
    nb = groups * nseg
    return pl.pallas_call(
        functools.partial(_mlstm_kernel, rows=rows, nseg=nseg, has_init=has_init),
        grid=(groups, HM, nchunks),
        in_specs=in_specs,
        out_specs=[
            pl.BlockSpec((rows, DV_M), lambda g, h, c: (row(g, h, c), h)),
            pl.BlockSpec((nseg, None, DK_M, DV_M), lambda g, h, c: (g, h, 0, 0)),
            pl.BlockSpec((nseg, None, 1, DK_M), lambda g, h, c: (g, h, 0, 0)),
            pl.BlockSpec((None, None, rows, LANES), lambda g, h, c: (g, h, 0, 0)),
        ],
        out_shape=[
            jax.ShapeDtypeStruct((m, HM * DV_M), BF16),
            jax.ShapeDtypeStruct((nb, HM, DK_M, DV_M), F32),
            jax.ShapeDtypeStruct((nb, HM, 1, DK_M), F32),
            jax.ShapeDtypeStruct((groups, HM, rows, LANES), F32),
        ],
        compiler_params=_params(("parallel", "parallel", "arbitrary")),
        name="mlstm",
    )(*args)


def _ret_kernel(*refs, rows, nseg, has_init):
    if has_init:
        (q_ref, k_ref, v_ref, g_ref, cos_ref, sin_ref, dmat_ref, dec_ref, gh_ref, s0_ref,
         o_ref, s_ref) = refs
        s_prev = s0_ref
    else:
        (q_ref, k_ref, v_ref, g_ref, cos_ref, sin_ref, dmat_ref, dec_ref, gh_ref,
         o_ref, s_ref) = refs
        s_prev = s_ref

        @pl.when(pl.program_id(2) == 0)
        def _zero_state():
            s_ref[...] = jnp.zeros_like(s_ref)

    cosf, sins = cos_ref[...], sin_ref[...]
    half = DK_R // 2

    def rot(x):
        return x * cosf + pltpu.roll(x, half, 1) * sins

    qr = rot(q_ref[...])
    kr = rot(k_ref[...]) * (DK_R ** -0.5)
    dec = dec_ref[...]
    inter, kdec, sdec = dec[:, 0:1], dec[:, 1:2], dec[0:1, 2:3]
    qb, kb, vb = qr.astype(BF16), kr.astype(BF16), v_ref[...].astype(BF16)
    s = lax.dot_general(qb, kb, (((1,), (1,)), ((), ())), preferred_element_type=F32) * dmat_ref[...]
    q_s = _state_dot(qb, [s_prev[j].astype(BF16) for j in range(nseg)], nseg)
    o = jnp.dot(s.astype(BF16), vb, preferred_element_type=F32) + inter * q_s
    g = g_ref[...]
    o_ref[...] = (_rms(o, gh_ref[...]) * (g * _sigmoid(g))).astype(BF16)

    kd_t = (kr * kdec).T
    for j in range(nseg):
        upd = jnp.dot(_segment_lhs(kd_t, j, nseg), vb, preferred_element_type=F32)
        s_ref[j] = sdec * s_prev[j] + upd


def _retention(proj, cos_t, sin_t, dmat, dec, g_head, rows, nseg, nchunks, init=None):
    m = proj.shape[0]
    groups = m // (rows * nchunks)
    has_init = init is not None
    assert not has_init or nchunks == 1
    row = lambda g, h, c: g * nchunks + c
    in_specs = [
        pl.BlockSpec((rows, DK_R), lambda g, h, c: (row(g, h, c), OFF_QR // DK_R + h)),
        pl.BlockSpec((rows, DK_R), lambda g, h, c: (row(g, h, c), OFF_KR // DK_R + h)),
        pl.BlockSpec((rows, DV_R), lambda g, h, c: (row(g, h, c), OFF_VR // DV_R + h)),
        pl.BlockSpec((rows, DV_R), lambda g, h, c: (row(g, h, c), OFF_GR // DV_R + h)),
        pl.BlockSpec((rows, DK_R), lambda g, h, c: (c, 0)),
        pl.BlockSpec((rows, DK_R), lambda g, h, c: (c, 0)),
        pl.BlockSpec((None, rows, rows), lambda g, h, c: (h, 0, 0)),
        pl.BlockSpec((None, rows, LANES), lambda g, h, c: (h, 0, 0)),
        pl.BlockSpec((1, DV_R), lambda g, h, c: (0, h)),
    ]
    args = [proj, proj, proj, proj, cos_t, sin_t, dmat, dec, g_head]
    if has_init:
        in_specs.append(pl.BlockSpec((nseg, None, DK_R, DV_R), lambda g, h, c: (g, h, 0, 0)))
        args.append(init)
    nb = groups * nseg
    return pl.pallas_call(
        functools.partial(_ret_kernel, rows=rows, nseg=nseg, has_init=has_init),
        grid=(groups, HR, nchunks),
        in_specs=in_specs,
        out_specs=[
            pl.BlockSpec((rows, DV_R), lambda g, h, c: (row(g, h, c), h)),
            pl.BlockSpec((nseg, None, DK_R, DV_R), lambda g, h, c: (g, h, 0, 0)),
        ],
        out_shape=[
            jax.ShapeDtypeStruct((m, HR * DV_R), BF16),
            jax.ShapeDtypeStruct((nb, HR, DK_R, DV_R), F32),
        ],
        compiler_params=_params(("parallel", "parallel", "arbitrary")),
        name="retention",
    )(*args)


def _retention_tables(seg_len, nseg, pos0):
    rows = seg_len * nseg
    t = jnp.arange(rows, dtype=jnp.int32) % seg_len
    seg = jnp.arange(rows, dtype=jnp.int32) // seg_len
    lg = jnp.log(1.0 - 2.0 ** (-5.0 - jnp.arange(HR, dtype=F32)))
    idx = t.astype(F32)
    diff = idx[:, None] - idx[None, :]
    ok = (diff >= 0) & (seg[:, None] == seg[None, :])
    dmat = jnp.where(ok[None], jnp.exp(jnp.maximum(diff, 0.0)[None] * lg[:, None, None]), 0.0)
    inter = jnp.exp((idx[None, :] + 1.0) * lg[:, None])
    kdec = jnp.exp((seg_len - 1.0 - idx[None, :]) * lg[:, None])
    sdec = jnp.broadcast_to(jnp.exp(seg_len * lg)[:, None], (HR, rows))
    dec = jnp.stack([inter, kdec, sdec], axis=-1)
    dec = jnp.pad(dec, ((0, 0), (0, 0), (0, LANES - 3)))
    return dmat, dec


def _rotary_tables(pos):
    freqs = ROPE_BASE ** (-jnp.arange(0, DK_R, 2, dtype=F32) / DK_R)
    ang = pos[:, None] * freqs[None, :]
    cos, sin = jnp.cos(ang), jnp.sin(ang)
    return jnp.concatenate([cos, cos], axis=-1), jnp.concatenate([-sin, sin], axis=-1)


def _out_proj_kernel(a1_ref, a2_ref, w_ref, x_ref, o_ref):
    k1 = a1_ref.shape[1]
    acc = jnp.dot(a1_ref[...], w_ref[:k1, :], preferred_element_type=F32)
    acc += jnp.dot(a2_ref[...], w_ref[k1:, :], preferred_element_type=F32)
    o_ref[...] = x_ref[...] + acc


def _out_proj(a1, a2, w, x2d, bm=1024, bn=512):
    m = x2d.shape[0]
    k1, k2 = a1.shape[1], a2.shape[1]
    n = w.shape[1]
    return pl.pallas_call(
        _out_proj_kernel,
        grid=(n // bn, m // bm),
        in_specs=[
            pl.BlockSpec((bm, k1), lambda j, i: (i, 0)),
            pl.BlockSpec((bm, k2), lambda j, i: (i, 0)),
            pl.BlockSpec((k1 + k2, bn), lambda j, i: (0, j)),
            pl.BlockSpec((bm, bn), lambda j, i: (i, j)),
        ],
        out_specs=pl.BlockSpec((bm, bn), lambda j, i: (i, j)),
        out_shape=jax.ShapeDtypeStruct((m, n), F32),
        compiler_params=_params(("parallel", "parallel")),
        name="out_proj",
    )(a1, a2, w, x2d)


def _ffn_kernel(x_ref, g_ref, wu_ref, wd_ref, gf_ref, o_ref, h_ref, *, final_norm):
    f = pl.program_id(1)

    @pl.when(f == 0)
    def _start():
        x = x_ref[...]
        h_ref[...] = _rms(x, g_ref[...]).astype(BF16)
        o_ref[...] = x

    u = jnp.dot(h_ref[...], wu_ref[...], preferred_element_type=F32)
    a = jnp.square(jnp.maximum(u, 0.0)).astype(BF16)
    o_ref[...] += jnp.dot(a, wd_ref[...], preferred_element_type=F32)

    if final_norm:
        @pl.when(f == pl.num_programs(1) - 1)
        def _finish():
            o_ref[...] = _rms(o_ref[...], gf_ref[...])


def _ffn(x1, g, wu, wd, g_final, final_norm, bm=512, bf=256):
    m = x1.shape[0]
    return pl.pallas_call(
        functools.partial(_ffn_kernel, final_norm=final_norm),
        grid=(m // bm, D_FF // bf),
        in_specs=[
            pl.BlockSpec((bm, D_MODEL), lambda i, f: (i, 0)),
            pl.BlockSpec((1, D_MODEL), lambda i, f: (0, 0)),
            pl.BlockSpec((D_MODEL, bf), lambda i, f: (0, f)),
            pl.BlockSpec((bf, D_MODEL), lambda i, f: (f, 0)),
            pl.BlockSpec((1, D_MODEL), lambda i, f: (0, 0)),
        ],
        out_specs=pl.BlockSpec((bm, D_MODEL), lambda i, f: (i, 0)),
        out_shape=jax.ShapeDtypeStruct((m, D_MODEL), F32),
        scratch_shapes=[pltpu.VMEM((bm, D_MODEL), BF16)],
        compiler_params=_params(("parallel", "arbitrary")),
        name="ffn",
    )(x1, g, wu, wd, g_final)


def _layer(x2d, w, tables, rows, nseg, nchunks, init, g_final, final_norm):
    h, gc, gr = _norm_gates(x2d, w["g_mix"], w["wg"], w["gate_bias"])
    proj = _in_proj(h, w["w_main"])
    m_init = None if init is None else (init["m_tok"], init["C"], init["n"])
    hm, c_new, n_new, m_slab = _mlstm(proj, gc, gr, w["g_mh"], rows, nseg, nchunks, m_init)
    s_init = None if init is None else init["S"]
    hr, s_new = _retention(proj, *tables, w["g_rh"], rows, nseg, nchunks, s_init)
    x1 = _out_proj(hm, hr, w["w_out"], x2d)
    y = _ffn(x1, w["g_ffn"], w["w_up"], w["w_down"], g_final, final_norm)
    return y, c_new, n_new, m_slab, s_new


def kernel(x_prompt, x_sample, state_mlstm_C, state_mlstm_n, state_mlstm_m, state_ret_S, w_in, b_igate, b_fgate, g_mlstm_head, g_ret_head, w_out, g_norm_mix, g_norm_ffn, w_up, w_down, g_final):
    depth = w_in.shape[0]
    batch, seq, _ = x_prompt.shape
    dec_batch, dec_seq, _ = x_sample.shape
    assert seq % PROMPT_CHUNK == 0 and SAMPLE_ROWS % dec_seq == 0
    assert (dec_batch * dec_seq) % SAMPLE_ROWS == 0
    seg_s = SAMPLE_ROWS // dec_seq
    nchunks_p = seq // PROMPT_CHUNK

    cos_p, sin_p = _rotary_tables(jnp.arange(seq, dtype=F32) + 0.0)
    pos_s = jnp.arange(dec_seq, dtype=F32) + float(PAST_LEN)
    cos_s, sin_s = _rotary_tables(jnp.tile(pos_s, seg_s))
    tables_p = (cos_p, sin_p) + _retention_tables(PROMPT_CHUNK, 1, 0.0)
    tables_s = (cos_s, sin_s) + _retention_tables(dec_seq, seg_s, float(PAST_LEN))

    yp = x_prompt.reshape(batch * seq, D_MODEL)
    ys = x_sample.reshape(dec_batch * dec_seq, D_MODEL)
    g_fin = g_final.reshape(1, D_MODEL)
    outs = {k: [] for k in ("pC", "pn", "pm", "pS", "sC", "sn", "sm", "sS")}
    for l in range(depth):
        wl = w_in[l]
        w_gate = wl[:, GATE_OFF:GATE_OFF + 2 * HM]
        w = {
            "w_main": jnp.concatenate([wl[:, :GATE_OFF], wl[:, GATE_OFF + 2 * HM:]], axis=1).astype(BF16),
            "wg": jnp.pad(w_gate, ((0, 0), (0, LANES - 2 * HM))).astype(BF16),
            "gate_bias": jnp.pad(jnp.concatenate([b_igate[l], b_fgate[l]]), (0, LANES - 2 * HM)).reshape(1, LANES),
            "g_mix": g_norm_mix[l].reshape(1, D_MODEL),
            "g_ffn": g_norm_ffn[l].reshape(1, D_MODEL),
            "g_mh": g_mlstm_head[l].reshape(1, HM * DV_M),
            "g_rh": g_ret_head[l].reshape(1, HR * DV_R),
            "w_out": w_out[l].astype(BF16),
            "w_up": w_up[l].astype(BF16),
            "w_down": w_down[l].astype(BF16),
        }
        last = l == depth - 1
        yp, c, n, m_slab, s = _layer(yp, w, tables_p, PROMPT_CHUNK, 1, nchunks_p, None, g_fin, last)
        outs["pC"].append(c)
        outs["pn"].append(n.reshape(batch, HM, DK_M))
        outs["pm"].append(m_slab[:, :, 0, 0])
        outs["pS"].append(s)

        m_tok = jnp.pad(jnp.repeat(state_mlstm_m[l], dec_seq, axis=0), ((0, 0), (0, LANES - HM)))
        init = {"m_tok": m_tok, "C": state_mlstm_C[l],
                "n": state_mlstm_n[l].reshape(dec_batch, HM, 1, DK_M), "S": state_ret_S[l]}
        ys, c, n, m_slab, s = _layer(ys, w, tables_s, SAMPLE_ROWS, seg_s, 1, init, g_fin, last)
        outs["sC"].append(c)
        outs["sn"].append(n.reshape(dec_batch, HM, DK_M))
        m_seg = m_slab[:, :, ::dec_seq, 0]
        outs["sm"].append(m_seg.transpose(0, 2, 1).reshape(dec_batch, HM))
        outs["sS"].append(s)

    return (yp.reshape(batch, seq, D_MODEL), ys.reshape(dec_batch, dec_seq, D_MODEL),
            jnp.stack(outs["pC"]), jnp.stack(outs["pn"]), jnp.stack(outs["pm"]), jnp.stack(outs["pS"]),
            jnp.stack(outs["sC"]), jnp.stack(outs["sn"]), jnp.stack(outs["sm"]), jnp.stack(outs["sS"]))
```

```python
import functools
import math

import jax
import jax.numpy as jnp
import numpy as np
from jax import lax
from jax.experimental import pallas as pl
from jax.experimental.pallas import tpu as pltpu

F32 = jnp.float32
BF16 = jnp.bfloat16

D_MODEL = 4096
HM, DK_M, DV_M = 4, 256, 512
HR, DK_R, DV_R = 8, 128, 256
D_FF = 4 * D_MODEL
PAST_LEN = 16384
GATE_SOFTCAP = 15.0
ROPE_BASE = 10000.0
EPS = 1e-6

LANES = 128
SUBLANES = 8
BF16_ROWS = 16
VMEM_LIMIT = 60 * 1024 * 1024

N_MAIN = 2 * HM * DK_M + 2 * HM * DV_M + 2 * HR * DK_R + 2 * HR * DV_R
GATE_OFF = 2 * HM * DK_M + 2 * HM * DV_M
OFF_QM, OFF_KM, OFF_VM, OFF_OM = 0, HM * DK_M, 2 * HM * DK_M, 2 * HM * DK_M + HM * DV_M
OFF_QR = GATE_OFF
OFF_KR = OFF_QR + HR * DK_R
OFF_VR = OFF_KR + HR * DK_R
OFF_GR = OFF_VR + HR * DV_R

PROMPT_CHUNK = 256
SAMPLE_ROWS = 128


def _params(sem):
    return pltpu.CompilerParams(dimension_semantics=sem, vmem_limit_bytes=VMEM_LIMIT)


def _sigmoid(x):
    return 1.0 / (1.0 + jnp.exp(-x))


def _rms(x, g):
    return x * lax.rsqrt(jnp.mean(x * x, axis=-1, keepdims=True) + EPS) * g


def _norm_gates_kernel(x_ref, g_ref, wg_ref, bias_ref, h_ref, gc_ref, gr_ref):
    hb = _rms(x_ref[...], g_ref[...]).astype(BF16)
    h_ref[...] = hb
    z = jnp.dot(hb, wg_ref[...], preferred_element_type=F32) + bias_ref[...]
    zc = GATE_SOFTCAP * jnp.tanh(z / GATE_SOFTCAP)
    logsig = jnp.minimum(zc, 0.0) - jnp.log1p(jnp.exp(-jnp.abs(zc)))
    lane = lax.broadcasted_iota(jnp.int32, zc.shape, 1)
    out = jnp.where(lane >= HM, logsig, zc)
    gc_ref[...] = out
    gr_ref[...] = out.T[:SUBLANES, :]


def _norm_gates(x2d, g, wg, bias, bm=512):
    m = x2d.shape[0]
    return pl.pallas_call(
        _norm_gates_kernel,
        grid=(m // bm,),
        in_specs=[
            pl.BlockSpec((bm, D_MODEL), lambda i: (i, 0)),
            pl.BlockSpec((1, D_MODEL), lambda i: (0, 0)),
            pl.BlockSpec((D_MODEL, LANES), lambda i: (0, 0)),
            pl.BlockSpec((1, LANES), lambda i: (0, 0)),
        ],
        out_specs=[
            pl.BlockSpec((bm, D_MODEL), lambda i: (i, 0)),
            pl.BlockSpec((bm, LANES), lambda i: (i, 0)),
            pl.BlockSpec((SUBLANES, bm), lambda i: (0, i)),
        ],
        out_shape=[
            jax.ShapeDtypeStruct((m, D_MODEL), BF16),
            jax.ShapeDtypeStruct((m, LANES), F32),
            jax.ShapeDtypeStruct((SUBLANES, m), F32),
        ],
        compiler_params=_params(("parallel",)),
        name="norm_gates",
    )(x2d, g, wg, bias)


def _in_proj_kernel(a_ref, w_ref, o_ref):
    o_ref[...] = jnp.dot(a_ref[...], w_ref[...], preferred_element_type=F32)


def _in_proj(h, w, bm=1024, bn=1024):
    m, k = h.shape
    n = w.shape[1]
    return pl.pallas_call(
        _in_proj_kernel,
        grid=(n // bn, m // bm),
        in_specs=[
            pl.BlockSpec((bm, k), lambda j, i: (i, 0)),
            pl.BlockSpec((k, bn), lambda j, i: (0, j)),
        ],
        out_specs=pl.BlockSpec((bm, bn), lambda j, i: (i, j)),
        out_shape=jax.ShapeDtypeStruct((m, n), F32),
        compiler_params=_params(("parallel", "parallel")),
        name="in_proj",
    )(h, w)


def _segment_masks(rows, nseg):
    ri = lax.broadcasted_iota(jnp.int32, (rows, rows), 0)
    ci = lax.broadcasted_iota(jnp.int32, (rows, rows), 1)
    if nseg == 1:
        same = None
        causal = ci <= ri
        upper = ri <= ci
    else:
        shift = int(math.log2(rows // nseg))
        same = lax.shift_right_logical(ri, shift) == lax.shift_right_logical(ci, shift)
        causal = (ci <= ri) & same
        upper = (ri <= ci) & same
    return same, causal, upper


def _state_dot(qb, state_bf16, nseg):
    if nseg == 1:
        return jnp.dot(qb, state_bf16[0], preferred_element_type=F32)
    rps = qb.shape[0] // nseg
    assert 2 * rps == BF16_ROWS
    outs = []
    for p in range(nseg // 2):
        qp = qb[p * BF16_ROWS:(p + 1) * BF16_ROWS]
        outs.append(jnp.dot(qp, state_bf16[2 * p], preferred_element_type=F32)[:rps])
        outs.append(jnp.dot(qp, state_bf16[2 * p + 1], preferred_element_type=F32)[rps:])
    return jnp.concatenate(outs, axis=0)


def _segment_lhs(xt, j, nseg):
    if nseg == 1:
        return xt.astype(BF16)
    rps = xt.shape[1] // nseg
    lane = lax.broadcasted_iota(jnp.int32, xt.shape, 1)
    keep = (lane >= j * rps) & (lane < (j + 1) * rps)
    return jnp.where(keep, xt, 0.0).astype(BF16)


def _mlstm_kernel(*refs, rows, nseg, has_init):
    if has_init:
        (q_ref, k_ref, v_ref, om_ref, gc_ref, gr_ref, gh_ref, m0_ref, c0_ref, n0_ref,
         h_ref, c_ref, n_ref, m_ref) = refs
    else:
        (q_ref, k_ref, v_ref, om_ref, gc_ref, gr_ref, gh_ref,
         h_ref, c_ref, n_ref, m_ref) = refs
    head = pl.program_id(1)
    rps = rows // nseg
    lane = lax.broadcasted_iota(jnp.int32, (rows, LANES), 1)
    sub = lax.broadcasted_iota(jnp.int32, (SUBLANES, rows), 0)

    def pick_col(x, idx):
        return jnp.sum(jnp.where(lane == idx, x, 0.0), axis=1, keepdims=True)

    def pick_row(x, idx):
        return jnp.sum(jnp.where(sub == idx, x, 0.0), axis=0, keepdims=True)

    if has_init:
        c_prev, n_prev = c0_ref, n0_ref
        m_col = pick_col(m0_ref[...], head)
    else:
        @pl.when(pl.program_id(2) == 0)
        def _zero_state():
            c_ref[...] = jnp.zeros_like(c_ref)
            n_ref[...] = jnp.zeros_like(n_ref)
            m_ref[...] = jnp.zeros_like(m_ref)
        c_prev, n_prev = c_ref, n_ref
        m_col = m_ref[:, 0:1]

    gc = gc_ref[...]
    gr = gr_ref[...]
    i_col, f_col = pick_col(gc, head), pick_col(gc, head + HM)
    i_row, f_row = pick_row(gr, head), pick_row(gr, head + HM)

    same, causal, upper = _segment_masks(rows, nseg)
    b_col = jnp.sum(jnp.where(causal, f_row, 0.0), axis=1, keepdims=True)
    b_row = jnp.sum(jnp.where(upper, f_col, 0.0), axis=0, keepdims=True)
    if same is None:
        bl_col = jnp.sum(f_row, axis=1, keepdims=True) + jnp.zeros_like(b_col)
        bl_row = jnp.sum(f_col, axis=0, keepdims=True) + jnp.zeros_like(b_row)
    else:
        bl_col = jnp.sum(jnp.where(same, f_row, 0.0), axis=1, keepdims=True)
        bl_row = jnp.sum(jnp.where(same, f_col, 0.0), axis=0, keepdims=True)

    dlog = b_col - b_row + i_row
    dmax = jnp.max(jnp.where(causal, dlog, -jnp.inf), axis=1, keepdims=True)
    inter = b_col + m_col
    mt = jnp.maximum(inter, dmax)
    dw = jnp.where(causal, jnp.exp(dlog - mt), 0.0)
    iw = jnp.exp(inter - mt)

    q = q_ref[...] * (DK_M ** -0.5)
    k = k_ref[...]
    qb, kb, vb = q.astype(BF16), k.astype(BF16), v_ref[...].astype(BF16)
    s = lax.dot_general(qb, kb, (((1,), (1,)), ((), ())), preferred_element_type=F32) * dw
    intra = jnp.dot(s.astype(BF16), vb, preferred_element_type=F32)
    q_c = _state_dot(qb, [c_prev[j].astype(BF16) for j in range(nseg)], nseg)
    n_tok = jnp.broadcast_to(n_prev[...], (nseg, rps, DK_M)).reshape(rows, DK_M)
    q_n = jnp.sum(q * n_tok, axis=1, keepdims=True)
    num = iw * q_c + intra
    den = iw * q_n + jnp.sum(s, axis=1, keepdims=True)
    hh = num * (1.0 / jnp.maximum(jnp.abs(den), jnp.exp(-mt)))
    h_ref[...] = (_rms(hh, gh_ref[...]) * _sigmoid(om_ref[...])).astype(BF16)

    wlog_col = bl_col - b_col + i_col
    wlog_row = bl_row - b_row + i_row
    if same is None:
        wmax = jnp.max(wlog_row, axis=1, keepdims=True) + jnp.zeros_like(b_col)
    else:
        wmax = jnp.max(jnp.where(same, wlog_row, -jnp.inf), axis=1, keepdims=True)
    m_new = jnp.maximum(bl_col + m_col, wmax)
    decay = jnp.exp(bl_col + m_col - m_new)
    kw = k * jnp.exp(wlog_col - m_new)
    kw_t = kw.T
    for j in range(nseg):
        d_j = decay[j * rps:j * rps + 1, :]
        upd = jnp.dot(_segment_lhs(kw_t, j, nseg), vb, preferred_element_type=F32)
        c_ref[j] = d_j * c_prev[j] + upd
        n_ref[j] = d_j * n_prev[j] + jnp.sum(kw[j * rps:(j + 1) * rps], axis=0, keepdims=True)
    m_ref[...] = jnp.broadcast_to(m_new, (rows, LANES))


def _mlstm(proj, gc, gr, g_head, rows, nseg, nchunks, init=None):
    m = proj.shape[0]
    groups = m // (rows * nchunks)
    has_init = init is not None
    assert not has_init or nchunks == 1
    row = lambda g, h, c: g * nchunks + c
    qk_blk, v_blk = DK_M, DV_M
    in_specs = [
        pl.BlockSpec((rows, DK_M), lambda g, h, c: (row(g, h, c), OFF_QM // qk_blk + h)),
        pl.BlockSpec((rows, DK_M), lambda g, h, c: (row(g, h, c), OFF_KM // qk_blk + h)),
        pl.BlockSpec((rows, DV_M), lambda g, h, c: (row(g, h, c), OFF_VM // v_blk + h)),
        pl.BlockSpec((rows, DV_M), lambda g, h, c: (row(g, h, c), OFF_OM // v_blk + h)),
        pl.BlockSpec((rows, LANES), lambda g, h, c: (row(g, h, c), 0)),
        pl.BlockSpec((SUBLANES, rows), lambda g, h, c: (0, row(g, h, c))),
        pl.BlockSpec((1, DV_M), lambda g, h, c: (0, h)),
    ]
    args = [proj, proj, proj, proj, gc, gr, g_head]
    if has_init:
        m0_tok, c0, n0 = init
        in_specs += [
            pl.BlockSpec((rows, LANES), lambda g, h, c: (g, 0)),
            pl.BlockSpec((nseg, None, DK_M, DV_M), lambda g, h, c: (g, h, 0, 0)),
            pl.BlockSpec((nseg, None, 1, DK_M), lambda g, h, c: (g, h, 0, 0)),
        ]
        args += [m0_tok, c0, n0]
    nb = groups * nseg
    return pl.pallas_call(
        functools.partial(_mlstm_kernel, rows=rows, nseg=nseg, has_init=has_init),
        grid=(groups, HM, nchunks),
        in_specs=in_specs,
        out_specs=[
            pl.BlockSpec((rows, DV_M), lambda g, h, c: (row(g, h, c), h)),
            pl.BlockSpec((nseg, None, DK_M, DV_M), lambda g, h, c: (g, h, 0, 0)),
            pl.BlockSpec((nseg, None, 1, DK_M), lambda g, h, c: (g, h, 0, 0)),
            pl.BlockSpec((None, None, rows, LANES), lambda g, h, c: (g, h, 0, 0)),
        ],
        out_shape=[
            jax.ShapeDtypeStruct((m, HM * DV_M), BF16),
            jax.ShapeDtypeStruct((nb, HM, DK_M, DV_M), F32),
            jax.ShapeDtypeStruct((nb, HM, 1, DK_M), F32),
            jax.ShapeDtypeStruct((groups, HM, rows, LANES), F32),
        ],
        compiler_params=_params(("parallel", "parallel", "arbitrary")),
        name="mlstm",
    )(*args)


def _ret_kernel(*refs, rows, nseg, has_init):
    if has_init:
        (q_ref, k_ref, v_ref, g_ref, cos_ref, sin_ref, dmat_ref, dec_ref, gh_ref, s0_ref,
         o_ref, s_ref) = refs
        s_prev = s0_ref
    else:
        (q_ref, k_ref, v_ref, g_ref, cos_ref, sin_ref, dmat_ref, dec_ref, gh_ref,
         o_ref, s_ref) = refs
        s_prev = s_ref

        @pl.when(pl.program_id(2) == 0)
        def _zero_state():
            s_ref[...] = jnp.zeros_like(s_ref)

    cosf, sins = cos_ref[...], sin_ref[...]
    half = DK_R // 2

    def rot(x):
        return x * cosf + pltpu.roll(x, half, 1) * sins

    qr = rot(q_ref[...])
    kr = rot(k_ref[...]) * (DK_R ** -0.5)
    dec = dec_ref[...]
    inter, kdec, sdec = dec[:, 0:1], dec[:, 1:2], dec[0:1, 2:3]
    qb, kb, vb = qr.astype(BF16), kr.astype(BF16), v_ref[...].astype(BF16)
    s = lax.dot_general(qb, kb, (((1,), (1,)), ((), ())), preferred_element_type=F32) * dmat_ref[...]
    q_s = _state_dot(qb, [s_prev[j].astype(BF16) for j in range(nseg)], nseg)
    o = jnp.dot(s.astype(BF16), vb, preferred_element_type=F32) + inter * q_s
    g = g_ref[...]
    o_ref[...] = (_rms(o, gh_ref[...]) * (g * _sigmoid(g))).astype(BF16)

    kd_t = (kr * kdec).T
    for j in range(nseg):
        upd = jnp.dot(_segment_lhs(kd_t, j, nseg), vb, preferred_element_type=F32)
        s_ref[j] = sdec * s_prev[j] + upd


def _retention(proj, cos_t, sin_t, dmat, dec, g_head, rows, nseg, nchunks, init=None):
    m = proj.shape[0]
    groups = m // (rows * nchunks)
    has_init = init is not None
    assert not has_init or nchunks == 1
    row = lambda g, h, c: g * nchunks + c
    in_specs = [
        pl.BlockSpec((rows, DK_R), lambda g, h, c: (row(g, h, c), OFF_QR // DK_R + h)),
        pl.BlockSpec((rows, DK_R), lambda g, h, c: (row(g, h, c), OFF_KR // DK_R + h)),
        pl.BlockSpec((rows, DV_R), lambda g, h, c: (row(g, h, c), OFF_VR // DV_R + h)),
        pl.BlockSpec((rows, DV_R), lambda g, h, c: (row(g, h, c), OFF_GR // DV_R + h)),
        pl.BlockSpec((rows, DK_R), lambda g, h, c: (c, 0)),
        pl.BlockSpec((rows, DK_R), lambda g, h, c: (c, 0)),
        pl.BlockSpec((None, rows, rows), lambda g, h, c: (h, 0, 0)),
        pl.BlockSpec((None, rows, LANES), lambda g, h, c: (h, 0, 0)),
        pl.BlockSpec((1, DV_R), lambda g, h, c: (0, h)),
    ]
    args = [proj, proj, proj, proj, cos_t, sin_t, dmat, dec, g_head]
    if has_init:
        in_specs.append(pl.BlockSpec((nseg, None, DK_R, DV_R), lambda g, h, c: (g, h, 0, 0)))
        args.append(init)
    nb = groups * nseg
    return pl.pallas_call(
        functools.partial(_ret_kernel, rows=rows, nseg=nseg, has_init=has_init),
        grid=(groups, HR, nchunks),
        in_specs=in_specs,
        out_specs=[
            pl.BlockSpec((rows, DV_R), lambda g, h, c: (row(g, h, c), h)),
            pl.BlockSpec((nseg, None, DK_R, DV_R), lambda g, h, c: (g, h, 0, 0)),
        ],
        out_shape=[
            jax.ShapeDtypeStruct((m, HR * DV_R), BF16),
            jax.ShapeDtypeStruct((nb, HR, DK_R, DV_R), F32),
        ],
        compiler_params=_params(("parallel", "parallel", "arbitrary")),
        name="retention",
    )(*args)


def _retention_tables(seg_len, nseg):
    rows = seg_len * nseg
    idx = (np.arange(rows) % seg_len).astype(np.float64)
    seg = np.arange(rows) // seg_len
    lg = np.log(1.0 - 2.0 ** (-5.0 - np.arange(HR, dtype=np.float64)))
    diff = idx[:, None] - idx[None, :]
    ok = (diff >= 0) & (seg[:, None] == seg[None, :])
    dmat = np.where(ok[None], np.exp(np.maximum(diff, 0.0)[None] * lg[:, None, None]), 0.0)
    dec = np.zeros((HR, rows, LANES))
    dec[:, :, 0] = np.exp((idx[None, :] + 1.0) * lg[:, None])
    dec[:, :, 1] = np.exp((seg_len - 1.0 - idx[None, :]) * lg[:, None])
    dec[:, :, 2] = np.exp(seg_len * lg)[:, None]
    return jnp.asarray(dmat, F32), jnp.asarray(dec, F32)


def _rotary_tables(pos):
    freqs = ROPE_BASE ** (-jnp.arange(0, DK_R, 2, dtype=F32) / DK_R)
    ang = pos[:, None] * freqs[None, :]
    cos, sin = jnp.cos(ang), jnp.sin(ang)
    return jnp.concatenate([cos, cos], axis=-1), jnp.concatenate([-sin, sin], axis=-1)


def _out_proj_kernel(a1_ref, a2_ref, w_ref, x_ref, o_ref):
    k1 = a1_ref.shape[1]
    acc = jnp.dot(a1_ref[...], w_ref[:k1, :], preferred_element_type=F32)
    acc += jnp.dot(a2_ref[...], w_ref[k1:, :], preferred_element_type=F32)
    o_ref[...] = x_ref[...] + acc


def _out_proj(a1, a2, w, x2d, bm=1024, bn=512):
    m = x2d.shape[0]
    k1, k2 = a1.shape[1], a2.shape[1]
    n = w.shape[1]
    return pl.pallas_call(
        _out_proj_kernel,
        grid=(n // bn, m // bm),
        in_specs=[
            pl.BlockSpec((bm, k1), lambda j, i: (i, 0)),
            pl.BlockSpec((bm, k2), lambda j, i: (i, 0)),
            pl.BlockSpec((k1 + k2, bn), lambda j, i: (0, j)),
            pl.BlockSpec((bm, bn), lambda j, i: (i, j)),
        ],
        out_specs=pl.BlockSpec((bm, bn), lambda j, i: (i, j)),
        out_shape=jax.ShapeDtypeStruct((m, n), F32),
        compiler_params=_params(("parallel", "parallel")),
        name="out_proj",
    )(a1, a2, w, x2d)


def _rmsnorm_kernel(x_ref, g_ref, o_ref):
    o_ref[...] = _rms(x_ref[...], g_ref[...]).astype(o_ref.dtype)


def _rmsnorm(x2d, g, out_dtype, bm=512):
    m = x2d.shape[0]
    return pl.pallas_call(
        _rmsnorm_kernel,
        grid=(m // bm,),
        in_specs=[
            pl.BlockSpec((bm, D_MODEL), lambda i: (i, 0)),
            pl.BlockSpec((1, D_MODEL), lambda i: (0, 0)),
        ],
        out_specs=pl.BlockSpec((bm, D_MODEL), lambda i: (i, 0)),
        out_shape=jax.ShapeDtypeStruct((m, D_MODEL), out_dtype),
        compiler_params=_params(("parallel",)),
        name="rmsnorm",
    )(x2d, g)


def _ffn_up_kernel(a_ref, w_ref, o_ref):
    u = jnp.dot(a_ref[...], w_ref[...], preferred_element_type=F32)
    o_ref[...] = jnp.square(jnp.maximum(u, 0.0)).astype(BF16)


def _ffn_up(h, w, bm=1024, bn=1024):
    m, k = h.shape
    n = w.shape[1]
    return pl.pallas_call(
        _ffn_up_kernel,
        grid=(n // bn, m // bm),
        in_specs=[
            pl.BlockSpec((bm, k), lambda j, i: (i, 0)),
            pl.BlockSpec((k, bn), lambda j, i: (0, j)),
        ],
        out_specs=pl.BlockSpec((bm, bn), lambda j, i: (i, j)),
        out_shape=jax.ShapeDtypeStruct((m, n), BF16),
        compiler_params=_params(("parallel", "parallel")),
        name="ffn_up",
    )(h, w)


def _ffn_down_kernel(a_ref, w_ref, x_ref, o_ref):
    acc = jnp.dot(a_ref[...], w_ref[...], preferred_element_type=F32)

    @pl.when(pl.program_id(2) == 0)
    def _first():
        o_ref[...] = x_ref[...] + acc

    @pl.when(pl.program_id(2) != 0)
    def _rest():
        o_ref[...] += acc


def _ffn_down(a, w, x1, bm=1024, bn=1024, bk=4096):
    m, k = a.shape
    n = w.shape[1]
    return pl.pallas_call(
        _ffn_down_kernel,
        grid=(n // bn, m // bm, k // bk),
        in_specs=[
            pl.BlockSpec((bm, bk), lambda j, i, kk: (i, kk)),
            pl.BlockSpec((bk, bn), lambda j, i, kk: (kk, j)),
            pl.BlockSpec((bm, bn), lambda j, i, kk: (i, j)),
        ],
        out_specs=pl.BlockSpec((bm, bn), lambda j, i, kk: (i, j)),
        out_shape=jax.ShapeDtypeStruct((m, n), F32),
        compiler_params=_params(("parallel", "parallel", "arbitrary")),
        name="ffn_down",
    )(a, w, x1)


def _ffn(x1, g, wu, wd, g_final, final_norm):
    h = _rmsnorm(x1, g, BF16)
    y = _ffn_down(_ffn_up(h, wu), wd, x1)
    return _rmsnorm(y, g_final, F32) if final_norm else y


def _layer(x2d, w, tables, rows, nseg, nchunks, init, g_final, final_norm):
    h, gc, gr = _norm_gates(x2d, w["g_mix"], w["wg"], w["gate_bias"])
    proj = _in_proj(h, w["w_main"])
    m_init = None if init is None else (init["m_tok"], init["C"], init["n"])
    hm, c_new, n_new, m_slab = _mlstm(proj, gc, gr, w["g_mh"], rows, nseg, nchunks, m_init)
    s_init = None if init is None else init["S"]
    hr, s_new = _retention(proj, *tables, w["g_rh"], rows, nseg, nchunks, s_init)
    x1 = _out_proj(hm, hr, w["w_out"], x2d)
    y = _ffn(x1, w["g_ffn"], w["w_up"], w["w_down"], g_final, final_norm)
    return y, c_new, n_new, m_slab, s_new


def kernel(x_prompt, x_sample, state_mlstm_C, state_mlstm_n, state_mlstm_m, state_ret_S, w_in, b_igate, b_fgate, g_mlstm_head, g_ret_head, w_out, g_norm_mix, g_norm_ffn, w_up, w_down, g_final):
    depth = w_in.shape[0]
    batch, seq, _ = x_prompt.shape
    dec_batch, dec_seq, _ = x_sample.shape
    assert seq % PROMPT_CHUNK == 0 and SAMPLE_ROWS % dec_seq == 0
    assert (dec_batch * dec_seq) % SAMPLE_ROWS == 0
    seg_s = SAMPLE_ROWS // dec_seq
    nchunks_p = seq // PROMPT_CHUNK

    cos_p, sin_p = _rotary_tables(jnp.arange(seq, dtype=F32) + 0.0)
    pos_s = jnp.arange(dec_seq, dtype=F32) + float(PAST_LEN)
    cos_s, sin_s = _rotary_tables(jnp.tile(pos_s, seg_s))
    tables_p = (cos_p, sin_p) + _retention_tables(PROMPT_CHUNK, 1)
    tables_s = (cos_s, sin_s) + _retention_tables(dec_seq, seg_s)

    yp = x_prompt.reshape(batch * seq, D_MODEL)
    ys = x_sample.reshape(dec_batch * dec_seq, D_MODEL)
    g_fin = g_final.reshape(1, D_MODEL)
    outs = {k: [] for k in ("pC", "pn", "pm", "pS", "sC", "sn", "sm", "sS")}
    for l in range(depth):
        wl = w_in[l]
        w_gate = wl[:, GATE_OFF:GATE_OFF + 2 * HM]
        w = {
            "w_main": jnp.concatenate([wl[:, :GATE_OFF], wl[:, GATE_OFF + 2 * HM:]], axis=1).astype(BF16),
            "wg": jnp.pad(w_gate, ((0, 0), (0, LANES - 2 * HM))).astype(BF16),
            "gate_bias": jnp.pad(jnp.concatenate([b_igate[l], b_fgate[l]]), (0, LANES - 2 * HM)).reshape(1, LANES),
            "g_mix": g_norm_mix[l].reshape(1, D_MODEL),
            "g_ffn": g_norm_ffn[l].reshape(1, D_MODEL),
            "g_mh": g_mlstm_head[l].reshape(1, HM * DV_M),
            "g_rh": g_ret_head[l].reshape(1, HR * DV_R),
            "w_out": w_out[l].astype(BF16),
            "w_up": w_up[l].astype(BF16),
            "w_down": w_down[l].astype(BF16),
        }
        last = l == depth - 1
        yp, c, n, m_slab, s = _layer(yp, w, tables_p, PROMPT_CHUNK, 1, nchunks_p, None, g_fin, last)
        outs["pC"].append(c)
        outs["pn"].append(n.reshape(batch, HM, DK_M))
        outs["pm"].append(m_slab[:, :, 0, 0])
        outs["pS"].append(s)

        m_tok = jnp.pad(jnp.repeat(state_mlstm_m[l], dec_seq, axis=0), ((0, 0), (0, LANES - HM)))
        init = {"m_tok": m_tok, "C": state_mlstm_C[l],
                "n": state_mlstm_n[l].reshape(dec_batch, HM, 1, DK_M), "S": state_ret_S[l]}
        ys, c, n, m_slab, s = _layer(ys, w, tables_s, SAMPLE_ROWS, seg_s, 1, init, g_fin, last)
        outs["sC"].append(c)
        outs["sn"].append(n.reshape(dec_batch, HM, DK_M))
        m_seg = m_slab[:, :, ::dec_seq, 0]
        outs["sm"].append(m_seg.transpose(0, 2, 1).reshape(dec_batch, HM))
        outs["sS"].append(s)

    def stack(xs):
        return xs[0][None] if len(xs) == 1 else jnp.stack(xs)

    return (yp.reshape(batch, seq, D_MODEL), ys.reshape(dec_batch, dec_seq, D_MODEL),
            stack(outs["pC"]), stack(outs["pn"]), stack(outs["pm"]), stack(outs["pS"]),
            stack(outs["sC"]), stack(outs["sn"]), stack(outs["sm"]), stack(outs["sS"]))
```

```python
import functools
import math

import jax
import jax.numpy as jnp
import numpy as np
from jax import lax
from jax.experimental import pallas as pl
from jax.experimental.pallas import tpu as pltpu

F32 = jnp.float32
BF16 = jnp.bfloat16

D_MODEL = 4096
HM, DK_M, DV_M = 4, 256, 512
HR, DK_R, DV_R = 8, 128, 256
D_FF = 4 * D_MODEL
PAST_LEN = 16384
GATE_SOFTCAP = 15.0
ROPE_BASE = 10000.0
EPS = 1e-6

LANES = 128
SUBLANES = 8
BF16_ROWS = 16
VMEM_LIMIT = 60 * 1024 * 1024

N_MAIN = 2 * HM * DK_M + 2 * HM * DV_M + 2 * HR * DK_R + 2 * HR * DV_R
GATE_OFF = 2 * HM * DK_M + 2 * HM * DV_M
OFF_QM, OFF_KM, OFF_VM, OFF_OM = 0, HM * DK_M, 2 * HM * DK_M, 2 * HM * DK_M + HM * DV_M
OFF_QR = GATE_OFF
OFF_KR = OFF_QR + HR * DK_R
OFF_VR = OFF_KR + HR * DK_R
OFF_GR = OFF_VR + HR * DV_R

PROMPT_CHUNK = 256
SAMPLE_ROWS = 128


_NN = (((1,), (0,)), ((), ()))
_NT = (((1,), (1,)), ((), ()))

GEMM_BM = 1024
GEMM_BN = 1024
GEMM_BN_CAST = 256
FFN_DOWN_BK = 4096


def _params(sem):
    return pltpu.CompilerParams(dimension_semantics=sem, vmem_limit_bytes=VMEM_LIMIT)


def _sigmoid(x):
    return 1.0 / (1.0 + jnp.exp(-x))


def _rms(x, g):
    return x * lax.rsqrt(jnp.mean(x * x, axis=-1, keepdims=True) + EPS) * g


def _norm_gates_kernel(x_ref, g_ref, wg_ref, bias_ref, h_ref, gc_ref, gr_ref):
    hb = _rms(x_ref[...], g_ref[...]).astype(BF16)
    h_ref[...] = hb
    z = lax.dot_general(hb, wg_ref[...], _NT, preferred_element_type=F32) + bias_ref[...]
    zc = GATE_SOFTCAP * jnp.tanh(z / GATE_SOFTCAP)
    logsig = jnp.minimum(zc, 0.0) - jnp.log1p(jnp.exp(-jnp.abs(zc)))
    lane = lax.broadcasted_iota(jnp.int32, zc.shape, 1)
    out = jnp.where(lane >= HM, logsig, zc)
    gc_ref[...] = out
    gr_ref[...] = out.T[:SUBLANES, :]


def _norm_gates(x2d, g, wg, bias, bm=512):
    m = x2d.shape[0]
    return pl.pallas_call(
        _norm_gates_kernel,
        grid=(m // bm,),
        in_specs=[
            pl.BlockSpec((bm, D_MODEL), lambda i: (i, 0)),
            pl.BlockSpec((1, D_MODEL), lambda i: (0, 0)),
            pl.BlockSpec((LANES, D_MODEL), lambda i: (0, 0)),
            pl.BlockSpec((1, LANES), lambda i: (0, 0)),
        ],
        out_specs=[
            pl.BlockSpec((bm, D_MODEL), lambda i: (i, 0)),
            pl.BlockSpec((bm, LANES), lambda i: (i, 0)),
            pl.BlockSpec((SUBLANES, bm), lambda i: (0, i)),
        ],
        out_shape=[
            jax.ShapeDtypeStruct((m, D_MODEL), BF16),
            jax.ShapeDtypeStruct((m, LANES), F32),
            jax.ShapeDtypeStruct((SUBLANES, m), F32),
        ],
        compiler_params=_params(("parallel",)),
        name="norm_gates",
    )(x2d, g, wg, bias)


def _gemm_kernel(*refs, n_a, nt, cast_w, shifted_from, epilogue):
    a_refs, w_ref = refs[:n_a], refs[n_a]
    rest = list(refs[n_a + 1:])
    w_hi_ref = rest.pop(0) if shifted_from is not None else None
    x_ref = rest.pop(0) if epilogue == "residual" else None
    o_ref = rest.pop(0)

    if cast_w:
        wb_ref = rest.pop(0)
        if shifted_from is None:
            wb_ref[...] = w_ref[...].astype(BF16)
        else:
            @pl.when(pl.program_id(0) < shifted_from)
            def _plain():
                wb_ref[...] = w_ref[...].astype(BF16)

            @pl.when(pl.program_id(0) >= shifted_from)
            def _shifted():
                w = jnp.concatenate([w_ref[SUBLANES:, :], w_hi_ref[...]], axis=0)
                wb_ref[...] = w.astype(BF16)
        w_ref = wb_ref

    acc, off = None, 0
    for a_ref in a_refs:
        ka = a_ref.shape[1]
        wk = w_ref[:, off:off + ka] if nt else w_ref[off:off + ka, :]
        part = lax.dot_general(a_ref[...], wk, _NT if nt else _NN, preferred_element_type=F32)
        acc = part if acc is None else acc + part
        off += ka

    if epilogue == "relu2":
        o_ref[...] = jnp.square(jnp.maximum(acc, 0.0)).astype(o_ref.dtype)
    elif epilogue == "residual":
        @pl.when(pl.program_id(2) == 0)
        def _first():
            o_ref[...] = x_ref[...] + acc

        @pl.when(pl.program_id(2) != 0)
        def _rest():
            o_ref[...] += acc
    else:
        o_ref[...] = acc


def _gemm(a_list, w, *, name, bm, bn, bk=None, nt=False, cast_w=False, shifted=False,
          epilogue=None, x=None, out_dtype=F32):
    m = a_list[0].shape[0]
    k = sum(a.shape[1] for a in a_list)
    n = N_MAIN if shifted else (w.shape[0] if nt else w.shape[1])
    bk = k if bk is None else bk
    k_steps = k // bk
    assert len(a_list) == 1 or k_steps == 1
    assert epilogue == "residual" or k_steps == 1
    assert not cast_w or m == bm
    in_specs = [pl.BlockSpec((bm, a.shape[1] if k_steps == 1 else bk), lambda j, i, kk: (i, kk))
                for a in a_list]
    w_blk = (bn, bk) if nt else (bk, bn)
    w_map = (lambda j, i, kk: (j, kk)) if nt else (lambda j, i, kk: (kk, j))
    in_specs.append(pl.BlockSpec(w_blk, w_map))
    args = [*a_list, w]
    shifted_from = None
    if shifted:
        assert nt and cast_w and k_steps == 1 and GATE_OFF % bn == 0
        shifted_from = GATE_OFF // bn
        in_specs.append(pl.BlockSpec((SUBLANES, bk), lambda j, i, kk: ((j + 1) * (bn // SUBLANES), 0)))
        args.append(w)
    if epilogue == "residual":
        in_specs.append(pl.BlockSpec((bm, bn), lambda j, i, kk: (i, j)))
        args.append(x)
    out_specs = [pl.BlockSpec((bm, bn), lambda j, i, kk: (i, j))]
    out_shape = [jax.ShapeDtypeStruct((m, n), out_dtype)]
    if cast_w:
        out_specs.append(pl.BlockSpec(w_blk, w_map))
        out_shape.append(jax.ShapeDtypeStruct((n, k) if nt else (k, n), BF16))
    res = pl.pallas_call(
        functools.partial(_gemm_kernel, n_a=len(a_list), nt=nt, cast_w=cast_w,
                          shifted_from=shifted_from, epilogue=epilogue),
        grid=(n // bn, m // bm, k_steps),
        in_specs=in_specs,
        out_specs=out_specs,
        out_shape=out_shape,
        compiler_params=_params(("parallel", "parallel", "arbitrary")),
        name=name,
    )(*args)
    return res if cast_w else res[0]


def _segment_masks(rows, nseg):
    ri = lax.broadcasted_iota(jnp.int32, (rows, rows), 0)
    ci = lax.broadcasted_iota(jnp.int32, (rows, rows), 1)
    if nseg == 1:
        same = None
        causal = ci <= ri
        upper = ri <= ci
    else:
        shift = int(math.log2(rows // nseg))
        same = lax.shift_right_logical(ri, shift) == lax.shift_right_logical(ci, shift)
        causal = (ci <= ri) & same
        upper = (ri <= ci) & same
    return same, causal, upper


def _state_dot(qb, state_bf16, nseg):
    if nseg == 1:
        return jnp.dot(qb, state_bf16[0], preferred_element_type=F32)
    rps = qb.shape[0] // nseg
    assert 2 * rps == BF16_ROWS
    outs = []
    for p in range(nseg // 2):
        qp = qb[p * BF16_ROWS:(p + 1) * BF16_ROWS]
        outs.append(jnp.dot(qp, state_bf16[2 * p], preferred_element_type=F32)[:rps])
        outs.append(jnp.dot(qp, state_bf16[2 * p + 1], preferred_element_type=F32)[rps:])
    return jnp.concatenate(outs, axis=0)


def _segment_lhs(xt, j, nseg):
    if nseg == 1:
        return xt.astype(BF16)
    rps = xt.shape[1] // nseg
    lane = lax.broadcasted_iota(jnp.int32, xt.shape, 1)
    keep = (lane >= j * rps) & (lane < (j + 1) * rps)
    return jnp.where(keep, xt, 0.0).astype(BF16)


def _mlstm_kernel(*refs, rows, nseg, has_init):
    if has_init:
        (q_ref, k_ref, v_ref, om_ref, gc_ref, gr_ref, gh_ref, m0_ref, c0_ref, n0_ref,
         h_ref, c_ref, n_ref, m_ref) = refs
    else:
        (q_ref, k_ref, v_ref, om_ref, gc_ref, gr_ref, gh_ref,
         h_ref, c_ref, n_ref, m_ref) = refs
    head = pl.program_id(1)
    rps = rows // nseg
    lane = lax.broadcasted_iota(jnp.int32, (rows, LANES), 1)
    sub = lax.broadcasted_iota(jnp.int32, (SUBLANES, rows), 0)

    def pick_col(x, idx):
        return jnp.sum(jnp.where(lane == idx, x, 0.0), axis=1, keepdims=True)

    def pick_row(x, idx):
        return jnp.sum(jnp.where(sub == idx, x, 0.0), axis=0, keepdims=True)

    if has_init:
        c_prev, n_prev = c0_ref, n0_ref
        m_col = pick_col(m0_ref[...], head)
    else:
        @pl.when(pl.program_id(2) == 0)
        def _zero_state():
            c_ref[...] = jnp.zeros_like(c_ref)
            n_ref[...] = jnp.zeros_like(n_ref)
            m_ref[...] = jnp.zeros_like(m_ref)
        c_prev, n_prev = c_ref, n_ref
        m_col = m_ref[:, 0:1]

    gc = gc_ref[...]
    gr = gr_ref[...]
    i_col, f_col = pick_col(gc, head), pick_col(gc, head + HM)
    i_row, f_row = pick_row(gr, head), pick_row(gr, head + HM)

    same, causal, upper = _segment_masks(rows, nseg)
    b_col = jnp.sum(jnp.where(causal, f_row, 0.0), axis=1, keepdims=True)
    b_row = jnp.sum(jnp.where(upper, f_col, 0.0), axis=0, keepdims=True)
    if same is None:
        bl_col = jnp.sum(f_row, axis=1, keepdims=True) + jnp.zeros_like(b_col)
        bl_row = jnp.sum(f_col, axis=0, keepdims=True) + jnp.zeros_like(b_row)
    else:
        bl_col = jnp.sum(jnp.where(same, f_row, 0.0), axis=1, keepdims=True)
        bl_row = jnp.sum(jnp.where(same, f_col, 0.0), axis=0, keepdims=True)

    dlog = b_col - b_row + i_row
    dmax = jnp.max(jnp.where(causal, dlog, -jnp.inf), axis=1, keepdims=True)
    inter = b_col + m_col
    mt = jnp.maximum(inter, dmax)
    dw = jnp.where(causal, jnp.exp(dlog - mt), 0.0)
    iw = jnp.exp(inter - mt)

    q = q_ref[...] * (DK_M ** -0.5)
    k = k_ref[...]
    qb, kb, vb = q.astype(BF16), k.astype(BF16), v_ref[...].astype(BF16)
    s = lax.dot_general(qb, kb, (((1,), (1,)), ((), ())), preferred_element_type=F32) * dw
    intra = jnp.dot(s.astype(BF16), vb, preferred_element_type=F32)
    q_c = _state_dot(qb, [c_prev[j].astype(BF16) for j in range(nseg)], nseg)
    n_tok = jnp.broadcast_to(n_prev[...], (nseg, rps, DK_M)).reshape(rows, DK_M)
    q_n = jnp.sum(q * n_tok, axis=1, keepdims=True)
    num = iw * q_c + intra
    den = iw * q_n + jnp.sum(s, axis=1, keepdims=True)
    hh = num * (1.0 / jnp.maximum(jnp.abs(den), jnp.exp(-mt)))
    h_ref[...] = (_rms(hh, gh_ref[...]) * _sigmoid(om_ref[...])).astype(BF16)

    wlog_col = bl_col - b_col + i_col
    wlog_row = bl_row - b_row + i_row
    if same is None:
        wmax = jnp.max(wlog_row, axis=1, keepdims=True) + jnp.zeros_like(b_col)
    else:
        wmax = jnp.max(jnp.where(same, wlog_row, -jnp.inf), axis=1, keepdims=True)
    m_new = jnp.maximum(bl_col + m_col, wmax)
    decay = jnp.exp(bl_col + m_col - m_new)
    kw = k * jnp.exp(wlog_col - m_new)
    kw_t = kw.T
    for j in range(nseg):
        d_j = decay[j * rps:j * rps + 1, :]
        upd = jnp.dot(_segment_lhs(kw_t, j, nseg), vb, preferred_element_type=F32)
        c_ref[j] = d_j * c_prev[j] + upd
        n_ref[j] = d_j * n_prev[j] + jnp.sum(kw[j * rps:(j + 1) * rps], axis=0, keepdims=True)
    m_ref[...] = jnp.broadcast_to(m_new, (rows, LANES))


def _mlstm(proj, gc, gr, g_head, rows, nseg, nchunks, init=None):
    m = proj.shape[0]
    groups = m // (rows * nchunks)
    has_init = init is not None
    assert not has_init or nchunks == 1
    row = lambda g, h, c: g * nchunks + c
    qk_blk, v_blk = DK_M, DV_M
    in_specs = [
        pl.BlockSpec((rows, DK_M), lambda g, h, c: (row(g, h, c), OFF_QM // qk_blk + h)),
        pl.BlockSpec((rows, DK_M), lambda g, h, c: (row(g, h, c), OFF_KM // qk_blk + h)),
        pl.BlockSpec((rows, DV_M), lambda g, h, c: (row(g, h, c), OFF_VM // v_blk + h)),
        pl.BlockSpec((rows, DV_M), lambda g, h, c: (row(g, h, c), OFF_OM // v_blk + h)),
        pl.BlockSpec((rows, LANES), lambda g, h, c: (row(g, h, c), 0)),
        pl.BlockSpec((SUBLANES, rows), lambda g, h, c: (0, row(g, h, c))),
        pl.BlockSpec((1, DV_M), lambda g, h, c: (0, h)),
    ]
    args = [proj, proj, proj, proj, gc, gr, g_head]
    if has_init:
        m0_tok, c0, n0 = init
        in_specs += [
            pl.BlockSpec((rows, LANES), lambda g, h, c: (g, 0)),
            pl.BlockSpec((nseg, None, DK_M, DV_M), lambda g, h, c: (g, h, 0, 0)),
            pl.BlockSpec((nseg, None, 1, DK_M), lambda g, h, c: (g, h, 0, 0)),
        ]
        args += [m0_tok, c0, n0]
    nb = groups * nseg
    return pl.pallas_call(
        functools.partial(_mlstm_kernel, rows=rows, nseg=nseg, has_init=has_init),
        grid=(groups, HM, nchunks),
        in_specs=in_specs,
        out_specs=[
            pl.BlockSpec((rows, DV_M), lambda g, h, c: (row(g, h, c), h)),
            pl.BlockSpec((nseg, None, DK_M, DV_M), lambda g, h, c: (g, h, 0, 0)),
            pl.BlockSpec((nseg, None, 1, DK_M), lambda g, h, c: (g, h, 0, 0)),
            pl.BlockSpec((None, None, rows, LANES), lambda g, h, c: (g, h, 0, 0)),
        ],
        out_shape=[
            jax.ShapeDtypeStruct((m, HM * DV_M), BF16),
            jax.ShapeDtypeStruct((nb, HM, DK_M, DV_M), F32),
            jax.ShapeDtypeStruct((nb, HM, 1, DK_M), F32),
            jax.ShapeDtypeStruct((groups, HM, rows, LANES), F32),
        ],
        compiler_params=_params(("parallel", "parallel", "arbitrary")),
        name="mlstm",
    )(*args)


def _ret_kernel(*refs, rows, nseg, has_init):
    if has_init:
        (q_ref, k_ref, v_ref, g_ref, cos_ref, sin_ref, dmat_ref, dec_ref, gh_ref, s0_ref,
         o_ref, s_ref) = refs
        s_prev = s0_ref
    else:
        (q_ref, k_ref, v_ref, g_ref, cos_ref, sin_ref, dmat_ref, dec_ref, gh_ref,
         o_ref, s_ref) = refs
        s_prev = s_ref

        @pl.when(pl.program_id(2) == 0)
        def _zero_state():
            s_ref[...] = jnp.zeros_like(s_ref)

    cosf, sins = cos_ref[...], sin_ref[...]
    half = DK_R // 2

    def rot(x):
        return x * cosf + pltpu.roll(x, half, 1) * sins

    qr = rot(q_ref[...])
    kr = rot(k_ref[...]) * (DK_R ** -0.5)
    dec = dec_ref[...]
    inter, kdec, sdec = dec[:, 0:1], dec[:, 1:2], dec[0:1, 2:3]
    qb, kb, vb = qr.astype(BF16), kr.astype(BF16), v_ref[...].astype(BF16)
    s = lax.dot_general(qb, kb, (((1,), (1,)), ((), ())), preferred_element_type=F32) * dmat_ref[...]
    q_s = _state_dot(qb, [s_prev[j].astype(BF16) for j in range(nseg)], nseg)
    o = jnp.dot(s.astype(BF16), vb, preferred_element_type=F32) + inter * q_s
    g = g_ref[...]
    o_ref[...] = (_rms(o, gh_ref[...]) * (g * _sigmoid(g))).astype(BF16)

    kd_t = (kr * kdec).T
    for j in range(nseg):
        upd = jnp.dot(_segment_lhs(kd_t, j, nseg), vb, preferred_element_type=F32)
        s_ref[j] = sdec * s_prev[j] + upd


def _retention(proj, cos_t, sin_t, dmat, dec, g_head, rows, nseg, nchunks, init=None):
    m = proj.shape[0]
    groups = m // (rows * nchunks)
    has_init = init is not None
    assert not has_init or nchunks == 1
    row = lambda g, h, c: g * nchunks + c
    in_specs = [
        pl.BlockSpec((rows, DK_R), lambda g, h, c: (row(g, h, c), OFF_QR // DK_R + h)),
        pl.BlockSpec((rows, DK_R), lambda g, h, c: (row(g, h, c), OFF_KR // DK_R + h)),
        pl.BlockSpec((rows, DV_R), lambda g, h, c: (row(g, h, c), OFF_VR // DV_R + h)),
        pl.BlockSpec((rows, DV_R), lambda g, h, c: (row(g, h, c), OFF_GR // DV_R + h)),
        pl.BlockSpec((rows, DK_R), lambda g, h, c: (c, 0)),
        pl.BlockSpec((rows, DK_R), lambda g, h, c: (c, 0)),
        pl.BlockSpec((None, rows, rows), lambda g, h, c: (h, 0, 0)),
        pl.BlockSpec((None, rows, LANES), lambda g, h, c: (h, 0, 0)),
        pl.BlockSpec((1, DV_R), lambda g, h, c: (0, h)),
    ]
    args = [proj, proj, proj, proj, cos_t, sin_t, dmat, dec, g_head]
    if has_init:
        in_specs.append(pl.BlockSpec((nseg, None, DK_R, DV_R), lambda g, h, c: (g, h, 0, 0)))
        args.append(init)
    nb = groups * nseg
    return pl.pallas_call(
        functools.partial(_ret_kernel, rows=rows, nseg=nseg, has_init=has_init),
        grid=(groups, HR, nchunks),
        in_specs=in_specs,
        out_specs=[
            pl.BlockSpec((rows, DV_R), lambda g, h, c: (row(g, h, c), h)),
            pl.BlockSpec((nseg, None, DK_R, DV_R), lambda g, h, c: (g, h, 0, 0)),
        ],
        out_shape=[
            jax.ShapeDtypeStruct((m, HR * DV_R), BF16),
            jax.ShapeDtypeStruct((nb, HR, DK_R, DV_R), F32),
        ],
        compiler_params=_params(("parallel", "parallel", "arbitrary")),
        name="retention",
    )(*args)


def _retention_tables(seg_len, nseg):
    rows = seg_len * nseg
    idx = (np.arange(rows) % seg_len).astype(np.float64)
    seg = np.arange(rows) // seg_len
    lg = np.log(1.0 - 2.0 ** (-5.0 - np.arange(HR, dtype=np.float64)))
    diff = idx[:, None] - idx[None, :]
    ok = (diff >= 0) & (seg[:, None] == seg[None, :])
    dmat = np.where(ok[None], np.exp(np.maximum(diff, 0.0)[None] * lg[:, None, None]), 0.0)
    dec = np.zeros((HR, rows, LANES))
    dec[:, :, 0] = np.exp((idx[None, :] + 1.0) * lg[:, None])
    dec[:, :, 1] = np.exp((seg_len - 1.0 - idx[None, :]) * lg[:, None])
    dec[:, :, 2] = np.exp(seg_len * lg)[:, None]
    return jnp.asarray(dmat, F32), jnp.asarray(dec, F32)


def _rotary_tables(pos):
    freqs = ROPE_BASE ** (-jnp.arange(0, DK_R, 2, dtype=F32) / DK_R)
    ang = pos[:, None] * freqs[None, :]
    cos, sin = jnp.cos(ang), jnp.sin(ang)
    return jnp.concatenate([cos, cos], axis=-1), jnp.concatenate([-sin, sin], axis=-1)


def _rmsnorm_kernel(x_ref, g_ref, o_ref):
    o_ref[...] = _rms(x_ref[...], g_ref[...]).astype(o_ref.dtype)


def _rmsnorm(x2d, g, out_dtype, bm=512):
    m = x2d.shape[0]
    return pl.pallas_call(
        _rmsnorm_kernel,
        grid=(m // bm,),
        in_specs=[
            pl.BlockSpec((bm, D_MODEL), lambda i: (i, 0)),
            pl.BlockSpec((1, D_MODEL), lambda i: (0, 0)),
        ],
        out_specs=pl.BlockSpec((bm, D_MODEL), lambda i: (i, 0)),
        out_shape=jax.ShapeDtypeStruct((m, D_MODEL), out_dtype),
        compiler_params=_params(("parallel",)),
        name="rmsnorm",
    )(x2d, g)


def _layer(x2d, w, wb, tables, rows, nseg, nchunks, init, g_final, final_norm):
    cast = wb is None
    assert not cast or x2d.shape[0] == GEMM_BM
    cfg = dict(bm=GEMM_BM, bn=GEMM_BN_CAST if cast else GEMM_BN, cast_w=cast)
    new_wb = {}

    def gemm(key, a_list, **kw):
        res = _gemm(a_list, w[key] if cast else wb[key], name=key, **cfg, **kw)
        if cast:
            res, new_wb[key] = res
        return res

    h, gc, gr = _norm_gates(x2d, w["g_mix"], w["wg"], w["gate_bias"])
    proj = gemm("in_proj", [h], nt=True, shifted=cast)
    m_init = None if init is None else (init["m_tok"], init["C"], init["n"])
    hm, c_new, n_new, m_slab = _mlstm(proj, gc, gr, w["g_mh"], rows, nseg, nchunks, m_init)
    s_init = None if init is None else init["S"]
    hr, s_new = _retention(proj, *tables, w["g_rh"], rows, nseg, nchunks, s_init)
    x1 = gemm("out_proj", [hm, hr], epilogue="residual", x=x2d)
    h2 = _rmsnorm(x1, w["g_ffn"], BF16)
    act = gemm("ffn_up", [h2], epilogue="relu2", out_dtype=BF16)
    y = gemm("ffn_down", [act], epilogue="residual", x=x1, bk=FFN_DOWN_BK)
    if final_norm:
        y = _rmsnorm(y, g_final, F32)
    return y, c_new, n_new, m_slab, s_new, (new_wb if cast else wb)


def kernel(x_prompt, x_sample, state_mlstm_C, state_mlstm_n, state_mlstm_m, state_ret_S, w_in, b_igate, b_fgate, g_mlstm_head, g_ret_head, w_out, g_norm_mix, g_norm_ffn, w_up, w_down, g_final):
    depth = w_in.shape[0]
    batch, seq, _ = x_prompt.shape
    dec_batch, dec_seq, _ = x_sample.shape
    assert seq % PROMPT_CHUNK == 0 and SAMPLE_ROWS % dec_seq == 0
    assert (dec_batch * dec_seq) % SAMPLE_ROWS == 0
    seg_s = SAMPLE_ROWS // dec_seq
    nchunks_p = seq // PROMPT_CHUNK

    cos_p, sin_p = _rotary_tables(jnp.arange(seq, dtype=F32) + 0.0)
    pos_s = jnp.arange(dec_seq, dtype=F32) + float(PAST_LEN)
    cos_s, sin_s = _rotary_tables(jnp.tile(pos_s, seg_s))
    tables_p = (cos_p, sin_p) + _retention_tables(PROMPT_CHUNK, 1)
    tables_s = (cos_s, sin_s) + _retention_tables(dec_seq, seg_s)

    yp = x_prompt.reshape(batch * seq, D_MODEL)
    ys = x_sample.reshape(dec_batch * dec_seq, D_MODEL)
    g_fin = g_final.reshape(1, D_MODEL)
    outs = {k: [] for k in ("pC", "pn", "pm", "pS", "sC", "sn", "sm", "sS")}
    for l in range(depth):
        w_in_t = w_in[l].T
        w_gate_t = w_in_t[GATE_OFF:GATE_OFF + 2 * HM]
        w = {
            "in_proj": w_in_t,
            "wg": jnp.pad(w_gate_t, ((0, LANES - 2 * HM), (0, 0))).astype(BF16),
            "gate_bias": jnp.pad(jnp.concatenate([b_igate[l], b_fgate[l]]), (0, LANES - 2 * HM)).reshape(1, LANES),
            "g_mix": g_norm_mix[l].reshape(1, D_MODEL),
            "g_ffn": g_norm_ffn[l].reshape(1, D_MODEL),
            "g_mh": g_mlstm_head[l].reshape(1, HM * DV_M),
            "g_rh": g_ret_head[l].reshape(1, HR * DV_R),
            "out_proj": w_out[l],
            "ffn_up": w_up[l],
            "ffn_down": w_down[l],
        }
        last = l == depth - 1
        m_tok = jnp.pad(jnp.repeat(state_mlstm_m[l], dec_seq, axis=0), ((0, 0), (0, LANES - HM)))
        init = {"m_tok": m_tok, "C": state_mlstm_C[l],
                "n": state_mlstm_n[l].reshape(dec_batch, HM, 1, DK_M), "S": state_ret_S[l]}
        ys, c, n, m_slab, s, wb = _layer(ys, w, None, tables_s, SAMPLE_ROWS, seg_s, 1, init, g_fin, last)
        outs["sC"].append(c)
        outs["sn"].append(n.reshape(dec_batch, HM, DK_M))
        m_seg = m_slab[:, :, ::dec_seq, 0]
        outs["sm"].append(m_seg.transpose(0, 2, 1).reshape(dec_batch, HM))
        outs["sS"].append(s)

        yp, c, n, m_slab, s, _ = _layer(yp, w, wb, tables_p, PROMPT_CHUNK, 1, nchunks_p, None, g_fin, last)
        outs["pC"].append(c)
        outs["pn"].append(n.reshape(batch, HM, DK_M))
        outs["pm"].append(m_slab[:, :, 0, 0])
        outs["pS"].append(s)

    def stack(xs):
        return xs[0][None] if len(xs) == 1 else jnp.stack(xs)

    return (yp.reshape(batch, seq, D_MODEL), ys.reshape(dec_batch, dec_seq, D_MODEL),
            stack(outs["pC"]), stack(outs["pn"]), stack(outs["pm"]), stack(outs["pS"]),
            stack(outs["sC"]), stack(outs["sn"]), stack(outs["sm"]), stack(outs["sS"]))
```

```python
import functools
import math

import jax
import jax.numpy as jnp
import numpy as np
from jax import lax
from jax.experimental import pallas as pl
from jax.experimental.pallas import tpu as pltpu

F32 = jnp.float32
BF16 = jnp.bfloat16

D_MODEL = 4096
HM, DK_M, DV_M = 4, 256, 512
HR, DK_R, DV_R = 8, 128, 256
D_FF = 4 * D_MODEL
PAST_LEN = 16384
GATE_SOFTCAP = 15.0
ROPE_BASE = 10000.0
EPS = 1e-6

LANES = 128
SUBLANES = 8
BF16_ROWS = 16
VMEM_LIMIT = 60 * 1024 * 1024

N_MAIN = 2 * HM * DK_M + 2 * HM * DV_M + 2 * HR * DK_R + 2 * HR * DV_R
GATE_OFF = 2 * HM * DK_M + 2 * HM * DV_M
OFF_QM, OFF_KM, OFF_VM, OFF_OM = 0, HM * DK_M, 2 * HM * DK_M, 2 * HM * DK_M + HM * DV_M
OFF_QR = GATE_OFF
OFF_KR = OFF_QR + HR * DK_R
OFF_VR = OFF_KR + HR * DK_R
OFF_GR = OFF_VR + HR * DV_R

PROMPT_CHUNK = 256
SAMPLE_ROWS = 128


_NN = (((1,), (0,)), ((), ()))
_NT = (((1,), (1,)), ((), ()))

GEMM_BM = 1024
GEMM_BN = 1024
GEMM_BN_CAST = 256
FFN_DOWN_BK = 4096
FFN_DOWN_BK_CAST = 1024


def _params(sem):
    return pltpu.CompilerParams(dimension_semantics=sem, vmem_limit_bytes=VMEM_LIMIT)


def _sigmoid(x):
    return 1.0 / (1.0 + jnp.exp(-x))


def _rms(x, g):
    return x * lax.rsqrt(jnp.mean(x * x, axis=-1, keepdims=True) + EPS) * g


def _norm_gates_kernel(x_ref, g_ref, wg_ref, bias_ref, h_ref, gc_ref, gr_ref):
    hb = _rms(x_ref[...], g_ref[...]).astype(BF16)
    h_ref[...] = hb
    z = lax.dot_general(hb, wg_ref[...], _NT, preferred_element_type=F32) + bias_ref[...]
    zc = GATE_SOFTCAP * jnp.tanh(z / GATE_SOFTCAP)
    logsig = jnp.minimum(zc, 0.0) - jnp.log1p(jnp.exp(-jnp.abs(zc)))
    lane = lax.broadcasted_iota(jnp.int32, zc.shape, 1)
    out = jnp.where(lane >= HM, logsig, zc)
    gc_ref[...] = out
    gr_ref[...] = out.T[:SUBLANES, :]


def _norm_gates(x2d, g, wg, bias, bm=512):
    m = x2d.shape[0]
    return pl.pallas_call(
        _norm_gates_kernel,
        grid=(m // bm,),
        in_specs=[
            pl.BlockSpec((bm, D_MODEL), lambda i: (i, 0)),
            pl.BlockSpec((1, D_MODEL), lambda i: (0, 0)),
            pl.BlockSpec((LANES, D_MODEL), lambda i: (0, 0)),
            pl.BlockSpec((1, LANES), lambda i: (0, 0)),
        ],
        out_specs=[
            pl.BlockSpec((bm, D_MODEL), lambda i: (i, 0)),
            pl.BlockSpec((bm, LANES), lambda i: (i, 0)),
            pl.BlockSpec((SUBLANES, bm), lambda i: (0, i)),
        ],
        out_shape=[
            jax.ShapeDtypeStruct((m, D_MODEL), BF16),
            jax.ShapeDtypeStruct((m, LANES), F32),
            jax.ShapeDtypeStruct((SUBLANES, m), F32),
        ],
        compiler_params=_params(("parallel",)),
        name="norm_gates",
    )(x2d, g, wg, bias)


def _gemm_kernel(*refs, n_a, nt, cast_w, shifted_from, epilogue):
    a_refs, w_ref = refs[:n_a], refs[n_a]
    rest = list(refs[n_a + 1:])
    w_hi_ref = rest.pop(0) if shifted_from is not None else None
    x_ref = rest.pop(0) if epilogue == "residual" else None
    o_ref = rest.pop(0)

    if cast_w:
        wb_ref = rest.pop(0)
        if shifted_from is None:
            wb_ref[...] = w_ref[...].astype(BF16)
        else:
            @pl.when(pl.program_id(0) < shifted_from)
            def _plain():
                wb_ref[...] = w_ref[...].astype(BF16)

            @pl.when(pl.program_id(0) >= shifted_from)
            def _shifted():
                w = jnp.concatenate([w_ref[SUBLANES:, :], w_hi_ref[...]], axis=0)
                wb_ref[...] = w.astype(BF16)
        w_ref = wb_ref

    acc, off = None, 0
    for a_ref in a_refs:
        ka = a_ref.shape[1]
        wk = w_ref[:, off:off + ka] if nt else w_ref[off:off + ka, :]
        part = lax.dot_general(a_ref[...], wk, _NT if nt else _NN, preferred_element_type=F32)
        acc = part if acc is None else acc + part
        off += ka

    if epilogue == "relu2":
        o_ref[...] = jnp.square(jnp.maximum(acc, 0.0)).astype(o_ref.dtype)
    elif epilogue == "residual":
        @pl.when(pl.program_id(2) == 0)
        def _first():
            o_ref[...] = x_ref[...] + acc

        @pl.when(pl.program_id(2) != 0)
        def _rest():
            o_ref[...] += acc
    else:
        o_ref[...] = acc


def _gemm(a_list, w, *, name, bm, bn, bk=None, nt=False, cast_w=False, shifted=False,
          epilogue=None, x=None, out_dtype=F32):
    m = a_list[0].shape[0]
    k = sum(a.shape[1] for a in a_list)
    n = N_MAIN if shifted else (w.shape[0] if nt else w.shape[1])
    bk = k if bk is None else bk
    k_steps = k // bk
    assert len(a_list) == 1 or k_steps == 1
    assert epilogue == "residual" or k_steps == 1
    assert not cast_w or m == bm
    in_specs = [pl.BlockSpec((bm, a.shape[1] if k_steps == 1 else bk), lambda j, i, kk: (i, kk))
                for a in a_list]
    w_blk = (bn, bk) if nt else (bk, bn)
    w_map = (lambda j, i, kk: (j, kk)) if nt else (lambda j, i, kk: (kk, j))
    in_specs.append(pl.BlockSpec(w_blk, w_map))
    args = [*a_list, w]
    shifted_from = None
    if shifted:
        assert nt and cast_w and k_steps == 1 and GATE_OFF % bn == 0
        shifted_from = GATE_OFF // bn
        in_specs.append(pl.BlockSpec((SUBLANES, bk), lambda j, i, kk: ((j + 1) * (bn // SUBLANES), 0)))
        args.append(w)
    if epilogue == "residual":
        in_specs.append(pl.BlockSpec((bm, bn), lambda j, i, kk: (i, j)))
        args.append(x)
    out_specs = [pl.BlockSpec((bm, bn), lambda j, i, kk: (i, j))]
    out_shape = [jax.ShapeDtypeStruct((m, n), out_dtype)]
    if cast_w:
        out_specs.append(pl.BlockSpec(w_blk, w_map))
        out_shape.append(jax.ShapeDtypeStruct((n, k) if nt else (k, n), BF16))
    res = pl.pallas_call(
        functools.partial(_gemm_kernel, n_a=len(a_list), nt=nt, cast_w=cast_w,
                          shifted_from=shifted_from, epilogue=epilogue),
        grid=(n // bn, m // bm, k_steps),
        in_specs=in_specs,
        out_specs=out_specs,
        out_shape=out_shape,
        compiler_params=_params(("parallel", "parallel", "arbitrary")),
        name=name,
    )(*args)
    return res if cast_w else res[0]


def _segment_masks(rows, nseg):
    ri = lax.broadcasted_iota(jnp.int32, (rows, rows), 0)
    ci = lax.broadcasted_iota(jnp.int32, (rows, rows), 1)
    if nseg == 1:
        same = None
        causal = ci <= ri
        upper = ri <= ci
    else:
        shift = int(math.log2(rows // nseg))
        same = lax.shift_right_logical(ri, shift) == lax.shift_right_logical(ci, shift)
        causal = (ci <= ri) & same
        upper = (ri <= ci) & same
    return same, causal, upper


def _state_dot(qb, state_bf16, nseg):
    if nseg == 1:
        return jnp.dot(qb, state_bf16[0], preferred_element_type=F32)
    rps = qb.shape[0] // nseg
    assert 2 * rps == BF16_ROWS
    outs = []
    for p in range(nseg // 2):
        qp = qb[p * BF16_ROWS:(p + 1) * BF16_ROWS]
        outs.append(jnp.dot(qp, state_bf16[2 * p], preferred_element_type=F32)[:rps])
        outs.append(jnp.dot(qp, state_bf16[2 * p + 1], preferred_element_type=F32)[rps:])
    return jnp.concatenate(outs, axis=0)


def _segment_lhs(xt, j, nseg):
    if nseg == 1:
        return xt.astype(BF16)
    rps = xt.shape[1] // nseg
    lane = lax.broadcasted_iota(jnp.int32, xt.shape, 1)
    keep = (lane >= j * rps) & (lane < (j + 1) * rps)
    return jnp.where(keep, xt, 0.0).astype(BF16)


def _mlstm_kernel(*refs, rows, nseg, has_init, nh):
    if has_init:
        (q_ref, k_ref, v_ref, om_ref, gc_ref, gr_ref, gh_ref, m0_ref, c0_ref, n0_ref,
         h_ref, c_ref, n_ref, m_ref) = refs
    else:
        (q_ref, k_ref, v_ref, om_ref, gc_ref, gr_ref, gh_ref,
         h_ref, c_ref, n_ref, m_ref) = refs
    for hh in range(nh):
        qk = slice(hh * DK_M, (hh + 1) * DK_M)
        vv = slice(hh * DV_M, (hh + 1) * DV_M)
        head_refs = [q_ref.at[:, qk], k_ref.at[:, qk], v_ref.at[:, vv], om_ref.at[:, vv],
                     gc_ref, gr_ref, gh_ref.at[:, vv]]
        if has_init:
            head_refs += [m0_ref, c0_ref.at[:, hh], n0_ref.at[:, hh]]
        head_refs += [h_ref.at[:, vv], c_ref.at[:, hh], n_ref.at[:, hh], m_ref.at[hh]]
        _mlstm_head(pl.program_id(1) * nh + hh, head_refs, rows, nseg, has_init)


def _mlstm_head(head, refs, rows, nseg, has_init):
    if has_init:
        (q_ref, k_ref, v_ref, om_ref, gc_ref, gr_ref, gh_ref, m0_ref, c0_ref, n0_ref,
         h_ref, c_ref, n_ref, m_ref) = refs
    else:
        (q_ref, k_ref, v_ref, om_ref, gc_ref, gr_ref, gh_ref,
         h_ref, c_ref, n_ref, m_ref) = refs
    rps = rows // nseg
    lane = lax.broadcasted_iota(jnp.int32, (rows, LANES), 1)
    sub = lax.broadcasted_iota(jnp.int32, (SUBLANES, rows), 0)

    def pick_col(x, idx):
        return jnp.sum(jnp.where(lane == idx, x, 0.0), axis=1, keepdims=True)

    def pick_row(x, idx):
        return jnp.sum(jnp.where(sub == idx, x, 0.0), axis=0, keepdims=True)

    if has_init:
        c_prev, n_prev = c0_ref, n0_ref
        m_col = pick_col(m0_ref[...], head)
    else:
        @pl.when(pl.program_id(2) == 0)
        def _zero_state():
            c_ref[...] = jnp.zeros(c_ref.shape, F32)
            n_ref[...] = jnp.zeros(n_ref.shape, F32)
            m_ref[...] = jnp.zeros(m_ref.shape, F32)
        c_prev, n_prev = c_ref, n_ref
        m_col = m_ref[:, 0:1]

    gc = gc_ref[...]
    gr = gr_ref[...]
    i_col, f_col = pick_col(gc, head), pick_col(gc, head + HM)
    i_row, f_row = pick_row(gr, head), pick_row(gr, head + HM)

    same, causal, upper = _segment_masks(rows, nseg)
    b_col = jnp.sum(jnp.where(causal, f_row, 0.0), axis=1, keepdims=True)
    b_row = jnp.sum(jnp.where(upper, f_col, 0.0), axis=0, keepdims=True)
    if same is None:
        bl_col = jnp.sum(f_row, axis=1, keepdims=True) + jnp.zeros_like(b_col)
        bl_row = jnp.sum(f_col, axis=0, keepdims=True) + jnp.zeros_like(b_row)
    else:
        bl_col = jnp.sum(jnp.where(same, f_row, 0.0), axis=1, keepdims=True)
        bl_row = jnp.sum(jnp.where(same, f_col, 0.0), axis=0, keepdims=True)

    dlog = b_col - b_row + i_row
    dmax = jnp.max(jnp.where(causal, dlog, -jnp.inf), axis=1, keepdims=True)
    inter = b_col + m_col
    mt = jnp.maximum(inter, dmax)
    dw = jnp.where(causal, jnp.exp(dlog - mt), 0.0)
    iw = jnp.exp(inter - mt)

    q = q_ref[...] * (DK_M ** -0.5)
    k = k_ref[...]
    qb, kb, vb = q.astype(BF16), k.astype(BF16), v_ref[...].astype(BF16)
    s = lax.dot_general(qb, kb, (((1,), (1,)), ((), ())), preferred_element_type=F32) * dw
    intra = jnp.dot(s.astype(BF16), vb, preferred_element_type=F32)
    q_c = _state_dot(qb, [c_prev[j].astype(BF16) for j in range(nseg)], nseg)
    n_tok = jnp.broadcast_to(n_prev[...], (nseg, rps, DK_M)).reshape(rows, DK_M)
    q_n = jnp.sum(q * n_tok, axis=1, keepdims=True)
    num = iw * q_c + intra
    den = iw * q_n + jnp.sum(s, axis=1, keepdims=True)
    hh = num * (1.0 / jnp.maximum(jnp.abs(den), jnp.exp(-mt)))
    h_ref[...] = (_rms(hh, gh_ref[...]) * _sigmoid(om_ref[...])).astype(BF16)

    wlog_col = bl_col - b_col + i_col
    wlog_row = bl_row - b_row + i_row
    if same is None:
        wmax = jnp.max(wlog_row, axis=1, keepdims=True) + jnp.zeros_like(b_col)
    else:
        wmax = jnp.max(jnp.where(same, wlog_row, -jnp.inf), axis=1, keepdims=True)
    m_new = jnp.maximum(bl_col + m_col, wmax)
    decay = jnp.exp(bl_col + m_col - m_new)
    kw = k * jnp.exp(wlog_col - m_new)
    kw_t = kw.T
    for j in range(nseg):
        d_j = decay[j * rps:j * rps + 1, :]
        upd = jnp.dot(_segment_lhs(kw_t, j, nseg), vb, preferred_element_type=F32)
        c_ref[j] = d_j * c_prev[j] + upd
        n_ref[j] = d_j * n_prev[j] + jnp.sum(kw[j * rps:(j + 1) * rps], axis=0, keepdims=True)
    m_ref[...] = jnp.broadcast_to(m_new, (rows, LANES))


def _mlstm(proj, gc, gr, g_head, rows, nseg, nchunks, nh, init=None):
    m = proj.shape[0]
    groups = m // (rows * nchunks)
    has_init = init is not None
    assert not has_init or nchunks == 1
    row = lambda g, h, c: g * nchunks + c
    qk_blk, v_blk = nh * DK_M, nh * DV_M
    in_specs = [
        pl.BlockSpec((rows, qk_blk), lambda g, h, c: (row(g, h, c), OFF_QM // qk_blk + h)),
        pl.BlockSpec((rows, qk_blk), lambda g, h, c: (row(g, h, c), OFF_KM // qk_blk + h)),
        pl.BlockSpec((rows, v_blk), lambda g, h, c: (row(g, h, c), OFF_VM // v_blk + h)),
        pl.BlockSpec((rows, v_blk), lambda g, h, c: (row(g, h, c), OFF_OM // v_blk + h)),
        pl.BlockSpec((rows, LANES), lambda g, h, c: (row(g, h, c), 0)),
        pl.BlockSpec((SUBLANES, rows), lambda g, h, c: (0, row(g, h, c))),
        pl.BlockSpec((1, v_blk), lambda g, h, c: (0, h)),
    ]
    args = [proj, proj, proj, proj, gc, gr, g_head]
    if has_init:
        m0_tok, c0, n0 = init
        in_specs += [
            pl.BlockSpec((rows, LANES), lambda g, h, c: (g, 0)),
            pl.BlockSpec((nseg, nh, DK_M, DV_M), lambda g, h, c: (g, h, 0, 0)),
            pl.BlockSpec((nseg, nh, 1, DK_M), lambda g, h, c: (g, h, 0, 0)),
        ]
        args += [m0_tok, c0, n0]
    nb = groups * nseg
    return pl.pallas_call(
        functools.partial(_mlstm_kernel, rows=rows, nseg=nseg, has_init=has_init, nh=nh),
        grid=(groups, HM // nh, nchunks),
        in_specs=in_specs,
        out_specs=[
            pl.BlockSpec((rows, v_blk), lambda g, h, c: (row(g, h, c), h)),
            pl.BlockSpec((nseg, nh, DK_M, DV_M), lambda g, h, c: (g, h, 0, 0)),
            pl.BlockSpec((nseg, nh, 1, DK_M), lambda g, h, c: (g, h, 0, 0)),
            pl.BlockSpec((None, nh, rows, LANES), lambda g, h, c: (g, h, 0, 0)),
        ],
        out_shape=[
            jax.ShapeDtypeStruct((m, HM * DV_M), BF16),
            jax.ShapeDtypeStruct((nb, HM, DK_M, DV_M), F32),
            jax.ShapeDtypeStruct((nb, HM, 1, DK_M), F32),
            jax.ShapeDtypeStruct((groups, HM, rows, LANES), F32),
        ],
        compiler_params=_params(("parallel", "parallel", "arbitrary")),
        name="mlstm",
    )(*args)


def _ret_kernel(*refs, rows, nseg, has_init, nh):
    if has_init:
        (q_ref, k_ref, v_ref, g_ref, cos_ref, sin_ref, dmat_ref, dec_ref, gh_ref, s0_ref,
         o_ref, s_ref) = refs
    else:
        (q_ref, k_ref, v_ref, g_ref, cos_ref, sin_ref, dmat_ref, dec_ref, gh_ref,
         o_ref, s_ref) = refs
    for hh in range(nh):
        qk = slice(hh * DK_R, (hh + 1) * DK_R)
        vv = slice(hh * DV_R, (hh + 1) * DV_R)
        head_refs = [q_ref.at[:, qk], k_ref.at[:, qk], v_ref.at[:, vv], g_ref.at[:, vv],
                     cos_ref, sin_ref, dmat_ref.at[hh], dec_ref.at[hh], gh_ref.at[:, vv]]
        if has_init:
            head_refs.append(s0_ref.at[:, hh])
        head_refs += [o_ref.at[:, vv], s_ref.at[:, hh]]
        _ret_head(head_refs, rows, nseg, has_init)


def _ret_head(refs, rows, nseg, has_init):
    if has_init:
        (q_ref, k_ref, v_ref, g_ref, cos_ref, sin_ref, dmat_ref, dec_ref, gh_ref, s0_ref,
         o_ref, s_ref) = refs
        s_prev = s0_ref
    else:
        (q_ref, k_ref, v_ref, g_ref, cos_ref, sin_ref, dmat_ref, dec_ref, gh_ref,
         o_ref, s_ref) = refs
        s_prev = s_ref

        @pl.when(pl.program_id(2) == 0)
        def _zero_state():
            s_ref[...] = jnp.zeros(s_ref.shape, F32)

    cosf, sins = cos_ref[...], sin_ref[...]
    half = DK_R // 2

    def rot(x):
        return x * cosf + pltpu.roll(x, half, 1) * sins

    qr = rot(q_ref[...])
    kr = rot(k_ref[...]) * (DK_R ** -0.5)
    dec = dec_ref[...]
    inter, kdec, sdec = dec[:, 0:1], dec[:, 1:2], dec[0:1, 2:3]
    qb, kb, vb = qr.astype(BF16), kr.astype(BF16), v_ref[...].astype(BF16)
    s = lax.dot_general(qb, kb, (((1,), (1,)), ((), ())), preferred_element_type=F32) * dmat_ref[...]
    q_s = _state_dot(qb, [s_prev[j].astype(BF16) for j in range(nseg)], nseg)
    o = jnp.dot(s.astype(BF16), vb, preferred_element_type=F32) + inter * q_s
    g = g_ref[...]
    o_ref[...] = (_rms(o, gh_ref[...]) * (g * _sigmoid(g))).astype(BF16)

    kd_t = (kr * kdec).T
    for j in range(nseg):
        upd = jnp.dot(_segment_lhs(kd_t, j, nseg), vb, preferred_element_type=F32)
        s_ref[j] = sdec * s_prev[j] + upd


def _retention(proj, cos_t, sin_t, dmat, dec, g_head, rows, nseg, nchunks, nh, init=None):
    m = proj.shape[0]
    groups = m // (rows * nchunks)
    has_init = init is not None
    assert not has_init or nchunks == 1
    row = lambda g, h, c: g * nchunks + c
    qk_blk, v_blk = nh * DK_R, nh * DV_R
    in_specs = [
        pl.BlockSpec((rows, qk_blk), lambda g, h, c: (row(g, h, c), OFF_QR // qk_blk + h)),
        pl.BlockSpec((rows, qk_blk), lambda g, h, c: (row(g, h, c), OFF_KR // qk_blk + h)),
        pl.BlockSpec((rows, v_blk), lambda g, h, c: (row(g, h, c), OFF_VR // v_blk + h)),
        pl.BlockSpec((rows, v_blk), lambda g, h, c: (row(g, h, c), OFF_GR // v_blk + h)),
        pl.BlockSpec((rows, DK_R), lambda g, h, c: (c, 0)),
        pl.BlockSpec((rows, DK_R), lambda g, h, c: (c, 0)),
        pl.BlockSpec((nh, rows, rows), lambda g, h, c: (h, 0, 0)),
        pl.BlockSpec((nh, rows, LANES), lambda g, h, c: (h, 0, 0)),
        pl.BlockSpec((1, v_blk), lambda g, h, c: (0, h)),
    ]
    args = [proj, proj, proj, proj, cos_t, sin_t, dmat, dec, g_head]
    if has_init:
        in_specs.append(pl.BlockSpec((nseg, nh, DK_R, DV_R), lambda g, h, c: (g, h, 0, 0)))
        args.append(init)
    nb = groups * nseg
    return pl.pallas_call(
        functools.partial(_ret_kernel, rows=rows, nseg=nseg, has_init=has_init, nh=nh),
        grid=(groups, HR // nh, nchunks),
        in_specs=in_specs,
        out_specs=[
            pl.BlockSpec((rows, v_blk), lambda g, h, c: (row(g, h, c), h)),
            pl.BlockSpec((nseg, nh, DK_R, DV_R), lambda g, h, c: (g, h, 0, 0)),
        ],
        out_shape=[
            jax.ShapeDtypeStruct((m, HR * DV_R), BF16),
            jax.ShapeDtypeStruct((nb, HR, DK_R, DV_R), F32),
        ],
        compiler_params=_params(("parallel", "parallel", "arbitrary")),
        name="retention",
    )(*args)


def _retention_tables(seg_len, nseg):
    rows = seg_len * nseg
    idx = (np.arange(rows) % seg_len).astype(np.float64)
    seg = np.arange(rows) // seg_len
    lg = np.log(1.0 - 2.0 ** (-5.0 - np.arange(HR, dtype=np.float64)))
    diff = idx[:, None] - idx[None, :]
    ok = (diff >= 0) & (seg[:, None] == seg[None, :])
    dmat = np.where(ok[None], np.exp(np.maximum(diff, 0.0)[None] * lg[:, None, None]), 0.0)
    dec = np.zeros((HR, rows, LANES))
    dec[:, :, 0] = np.exp((idx[None, :] + 1.0) * lg[:, None])
    dec[:, :, 1] = np.exp((seg_len - 1.0 - idx[None, :]) * lg[:, None])
    dec[:, :, 2] = np.exp(seg_len * lg)[:, None]
    return jnp.asarray(dmat, F32), jnp.asarray(dec, F32)


def _rotary_tables(pos):
    freqs = ROPE_BASE ** (-jnp.arange(0, DK_R, 2, dtype=F32) / DK_R)
    ang = pos[:, None] * freqs[None, :]
    cos, sin = jnp.cos(ang), jnp.sin(ang)
    return jnp.concatenate([cos, cos], axis=-1), jnp.concatenate([-sin, sin], axis=-1)


def _rmsnorm_kernel(x_ref, g_ref, o_ref):
    o_ref[...] = _rms(x_ref[...], g_ref[...]).astype(o_ref.dtype)


def _rmsnorm(x2d, g, out_dtype, bm=512):
    m = x2d.shape[0]
    return pl.pallas_call(
        _rmsnorm_kernel,
        grid=(m // bm,),
        in_specs=[
            pl.BlockSpec((bm, D_MODEL), lambda i: (i, 0)),
            pl.BlockSpec((1, D_MODEL), lambda i: (0, 0)),
        ],
        out_specs=pl.BlockSpec((bm, D_MODEL), lambda i: (i, 0)),
        out_shape=jax.ShapeDtypeStruct((m, D_MODEL), out_dtype),
        compiler_params=_params(("parallel",)),
        name="rmsnorm",
    )(x2d, g)


def _layer(x2d, w, wb, tables, rows, nseg, nchunks, init, g_final, final_norm):
    cast = wb is None
    assert not cast or x2d.shape[0] == GEMM_BM
    cfg = dict(bm=GEMM_BM, bn=GEMM_BN_CAST if cast else GEMM_BN, cast_w=cast)
    new_wb = {}

    def gemm(key, a_list, **kw):
        res = _gemm(a_list, w[key] if cast else wb[key], name=key, **{**cfg, **kw})
        if cast:
            res, new_wb[key] = res
        return res

    h, gc, gr = _norm_gates(x2d, w["g_mix"], w["wg"], w["gate_bias"])
    proj = gemm("in_proj", [h], nt=True, shifted=cast)
    m_init = None if init is None else (init["m_tok"], init["C"], init["n"])
    nh_m = HM if init is None else 1
    hm, c_new, n_new, m_slab = _mlstm(proj, gc, gr, w["g_mh"], rows, nseg, nchunks, nh_m, m_init)
    s_init = None if init is None else init["S"]
    nh_r = HR if init is None else 1
    hr, s_new = _retention(proj, *tables, w["g_rh"], rows, nseg, nchunks, nh_r, s_init)
    x1 = gemm("out_proj", [hm, hr], epilogue="residual", x=x2d)
    h2 = _rmsnorm(x1, w["g_ffn"], BF16)
    act = gemm("ffn_up", [h2], epilogue="relu2", out_dtype=BF16)
    down_cfg = dict(bn=GEMM_BN, bk=FFN_DOWN_BK_CAST) if cast else dict(bk=FFN_DOWN_BK)
    y = gemm("ffn_down", [act], epilogue="residual", x=x1, **down_cfg)
    if final_norm:
        y = _rmsnorm(y, g_final, F32)
    return y, c_new, n_new, m_slab, s_new, (new_wb if cast else wb)


def kernel(x_prompt, x_sample, state_mlstm_C, state_mlstm_n, state_mlstm_m, state_ret_S, w_in, b_igate, b_fgate, g_mlstm_head, g_ret_head, w_out, g_norm_mix, g_norm_ffn, w_up, w_down, g_final):
    depth = w_in.shape[0]
    batch, seq, _ = x_prompt.shape
    dec_batch, dec_seq, _ = x_sample.shape
    assert seq % PROMPT_CHUNK == 0 and SAMPLE_ROWS % dec_seq == 0
    assert (dec_batch * dec_seq) % SAMPLE_ROWS == 0
    seg_s = SAMPLE_ROWS // dec_seq
    nchunks_p = seq // PROMPT_CHUNK

    cos_p, sin_p = _rotary_tables(jnp.arange(seq, dtype=F32) + 0.0)
    pos_s = jnp.arange(dec_seq, dtype=F32) + float(PAST_LEN)
    cos_s, sin_s = _rotary_tables(jnp.tile(pos_s, seg_s))
    tables_p = (cos_p, sin_p) + _retention_tables(PROMPT_CHUNK, 1)
    tables_s = (cos_s, sin_s) + _retention_tables(dec_seq, seg_s)

    yp = x_prompt.reshape(batch * seq, D_MODEL)
    ys = x_sample.reshape(dec_batch * dec_seq, D_MODEL)
    g_fin = g_final.reshape(1, D_MODEL)
    outs = {k: [] for k in ("pC", "pn", "pm", "pS", "sC", "sn", "sm", "sS")}
    for l in range(depth):
        w_in_t = w_in[l].T
        w_gate_t = w_in_t[GATE_OFF:GATE_OFF + 2 * HM]
        w = {
            "in_proj": w_in_t,
            "wg": jnp.pad(w_gate_t, ((0, LANES - 2 * HM), (0, 0))).astype(BF16),
            "gate_bias": jnp.pad(jnp.concatenate([b_igate[l], b_fgate[l]]), (0, LANES - 2 * HM)).reshape(1, LANES),
            "g_mix": g_norm_mix[l].reshape(1, D_MODEL),
            "g_ffn": g_norm_ffn[l].reshape(1, D_MODEL),
            "g_mh": g_mlstm_head[l].reshape(1, HM * DV_M),
            "g_rh": g_ret_head[l].reshape(1, HR * DV_R),
            "out_proj": w_out[l],
            "ffn_up": w_up[l],
            "ffn_down": w_down[l],
        }
        last = l == depth - 1
        m_tok = jnp.pad(jnp.repeat(state_mlstm_m[l], dec_seq, axis=0), ((0, 0), (0, LANES - HM)))
        init = {"m_tok": m_tok, "C": state_mlstm_C[l],
                "n": state_mlstm_n[l].reshape(dec_batch, HM, 1, DK_M), "S": state_ret_S[l]}
        ys, c, n, m_slab, s, wb = _layer(ys, w, None, tables_s, SAMPLE_ROWS, seg_s, 1, init, g_fin, last)
        outs["sC"].append(c)
        outs["sn"].append(n.reshape(dec_batch, HM, DK_M))
        m_seg = m_slab[:, :, ::dec_seq, 0]
        outs["sm"].append(m_seg.transpose(0, 2, 1).reshape(dec_batch, HM))
        outs["sS"].append(s)

        yp, c, n, m_slab, s, _ = _layer(yp, w, wb, tables_p, PROMPT_CHUNK, 1, nchunks_p, None, g_fin, last)
        outs["pC"].append(c)
        outs["pn"].append(n.reshape(batch, HM, DK_M))
        outs["pm"].append(m_slab[:, :, 0, 0])
        outs["pS"].append(s)

    def stack(xs):
        return xs[0][None] if len(xs) == 1 else jnp.stack(xs)

    return (yp.reshape(batch, seq, D_MODEL), ys.reshape(dec_batch, dec_seq, D_MODEL),
            stack(outs["pC"]), stack(outs["pn"]), stack(outs["pm"]), stack(outs["pS"]),
            stack(outs["sC"]), stack(outs["sn"]), stack(outs["sm"]), stack(outs["sS"]))
```

```python
import functools
import math

import jax
import jax.numpy as jnp
import numpy as np
from jax import lax
from jax.experimental import pallas as pl
from jax.experimental.pallas import tpu as pltpu

F32 = jnp.float32
BF16 = jnp.bfloat16

D_MODEL = 4096
HM, DK_M, DV_M = 4, 256, 512
HR, DK_R, DV_R = 8, 128, 256
D_FF = 4 * D_MODEL
PAST_LEN = 16384
GATE_SOFTCAP = 15.0
ROPE_BASE = 10000.0
EPS = 1e-6

LANES = 128
SUBLANES = 8
BF16_ROWS = 16
VMEM_LIMIT = 60 * 1024 * 1024

N_MAIN = 2 * HM * DK_M + 2 * HM * DV_M + 2 * HR * DK_R + 2 * HR * DV_R
GATE_OFF = 2 * HM * DK_M + 2 * HM * DV_M
OFF_QM, OFF_KM, OFF_VM, OFF_OM = 0, HM * DK_M, 2 * HM * DK_M, 2 * HM * DK_M + HM * DV_M
OFF_QR = GATE_OFF
OFF_KR = OFF_QR + HR * DK_R
OFF_VR = OFF_KR + HR * DK_R
OFF_GR = OFF_VR + HR * DV_R

PROMPT_CHUNK = 256
SAMPLE_ROWS = 128


_NN = (((1,), (0,)), ((), ()))
_NT = (((1,), (1,)), ((), ()))

GEMM_BM = 1024
GEMM_BN = 1024
GEMM_BN_CAST = 256
GEMM_BN_CAST_WIDE = 512
ACC_CHUNK = 256
FFN_DOWN_BK = 4096
FFN_DOWN_BK_CAST = 1024


def _params(sem):
    return pltpu.CompilerParams(dimension_semantics=sem, vmem_limit_bytes=VMEM_LIMIT)


def _sigmoid(x):
    return 1.0 / (1.0 + jnp.exp(-x))


def _rms(x, g):
    return x * lax.rsqrt(jnp.mean(x * x, axis=-1, keepdims=True) + EPS) * g


def _norm_gates_kernel(x_ref, g_ref, wg_ref, bias_ref, h_ref, gc_ref, gr_ref):
    hb = _rms(x_ref[...], g_ref[...]).astype(BF16)
    h_ref[...] = hb
    z = lax.dot_general(hb, wg_ref[...], _NT, preferred_element_type=F32) + bias_ref[...]
    zc = GATE_SOFTCAP * jnp.tanh(z / GATE_SOFTCAP)
    logsig = jnp.minimum(zc, 0.0) - jnp.log1p(jnp.exp(-jnp.abs(zc)))
    lane = lax.broadcasted_iota(jnp.int32, zc.shape, 1)
    out = jnp.where(lane >= HM, logsig, zc)
    gc_ref[...] = out
    gr_ref[...] = out.T[:SUBLANES, :]


def _norm_gates(x2d, g, wg, bias, bm=512):
    m = x2d.shape[0]
    return pl.pallas_call(
        _norm_gates_kernel,
        grid=(m // bm,),
        in_specs=[
            pl.BlockSpec((bm, D_MODEL), lambda i: (i, 0)),
            pl.BlockSpec((1, D_MODEL), lambda i: (0, 0)),
            pl.BlockSpec((LANES, D_MODEL), lambda i: (0, 0)),
            pl.BlockSpec((1, LANES), lambda i: (0, 0)),
        ],
        out_specs=[
            pl.BlockSpec((bm, D_MODEL), lambda i: (i, 0)),
            pl.BlockSpec((bm, LANES), lambda i: (i, 0)),
            pl.BlockSpec((SUBLANES, bm), lambda i: (0, i)),
        ],
        out_shape=[
            jax.ShapeDtypeStruct((m, D_MODEL), BF16),
            jax.ShapeDtypeStruct((m, LANES), F32),
            jax.ShapeDtypeStruct((SUBLANES, m), F32),
        ],
        compiler_params=_params(("parallel",)),
        name="norm_gates",
    )(x2d, g, wg, bias)


def _gemm_kernel(*refs, n_a, nt, cast_w, shifted_from, epilogue):
    a_refs, w_ref = refs[:n_a], refs[n_a]
    rest = list(refs[n_a + 1:])
    w_hi_ref = rest.pop(0) if shifted_from is not None else None
    x_ref = rest.pop(0) if epilogue in ("residual", "residual_prenorm") else None
    g_ref = rest.pop(0) if epilogue == "residual_prenorm" else None
    ss_in_ref = rest.pop(0) if epilogue == "rownorm_relu2" else None
    o_ref = rest.pop(0)
    if epilogue == "residual_prenorm":
        xg_ref, ss_ref = rest.pop(0), rest.pop(0)

    if cast_w:
        wb_ref = rest.pop(0)
        if shifted_from is None:
            wb_ref[...] = w_ref[...].astype(BF16)
        else:
            @pl.when(pl.program_id(0) < shifted_from)
            def _plain():
                wb_ref[...] = w_ref[...].astype(BF16)

            @pl.when(pl.program_id(0) >= shifted_from)
            def _shifted():
                w = jnp.concatenate([w_ref[SUBLANES:, :], w_hi_ref[...]], axis=0)
                wb_ref[...] = w.astype(BF16)
        w_ref = wb_ref

    def product(cols):
        acc, off = None, 0
        for a_ref in a_refs:
            ka = a_ref.shape[1]
            wk = w_ref[cols, off:off + ka] if nt else w_ref[off:off + ka, cols]
            part = lax.dot_general(a_ref[...], wk, _NT if nt else _NN, preferred_element_type=F32)
            acc = part if acc is None else acc + part
            off += ka
        return acc

    if epilogue == "residual":
        @pl.when(pl.program_id(2) == 0)
        def _first():
            o_ref[...] = x_ref[...]

        bn = o_ref.shape[1]
        for c in range(0, bn, ACC_CHUNK):
            cols = slice(c, min(c + ACC_CHUNK, bn))
            o_ref[:, cols] += product(cols)
    elif epilogue == "residual_prenorm":
        bn = o_ref.shape[1]
        ss = jnp.zeros(ss_ref.shape, F32)
        for c in range(0, bn, ACC_CHUNK):
            cols = slice(c, min(c + ACC_CHUNK, bn))
            val = x_ref[:, cols] + product(cols)
            o_ref[:, cols] = val
            xg_ref[:, cols] = (val * g_ref[:, cols]).astype(BF16)
            sq = val * val
            for t in range(0, sq.shape[1], LANES):
                ss = ss + sq[:, t:t + LANES]
        ss_ref[...] = ss
    elif epilogue == "rownorm_relu2":
        mean_sq = jnp.sum(ss_in_ref[...], axis=1, keepdims=True) * (1.0 / D_MODEL)
        u = product(slice(None)) * lax.rsqrt(mean_sq + EPS)
        o_ref[...] = jnp.square(jnp.maximum(u, 0.0)).astype(o_ref.dtype)
    else:
        o_ref[...] = product(slice(None))


def _gemm(a_list, w, *, name, bm, bn, bk=None, nt=False, cast_w=False, shifted=False,
          epilogue=None, x=None, norm_g=None, row_ss=None, out_dtype=F32):
    m = a_list[0].shape[0]
    k = sum(a.shape[1] for a in a_list)
    n = N_MAIN if shifted else (w.shape[0] if nt else w.shape[1])
    bk = k if bk is None else bk
    k_steps = k // bk
    assert len(a_list) == 1 or k_steps == 1
    assert epilogue == "residual" or k_steps == 1
    assert not cast_w or m == bm
    assert epilogue != "rownorm_relu2" or k == D_MODEL
    in_specs = [pl.BlockSpec((bm, a.shape[1] if k_steps == 1 else bk), lambda j, i, kk: (i, kk))
                for a in a_list]
    w_blk = (bn, bk) if nt else (bk, bn)
    w_map = (lambda j, i, kk: (j, kk)) if nt else (lambda j, i, kk: (kk, j))
    in_specs.append(pl.BlockSpec(w_blk, w_map))
    args = [*a_list, w]
    shifted_from = None
    if shifted:
        assert nt and cast_w and k_steps == 1 and GATE_OFF % bn == 0
        shifted_from = GATE_OFF // bn
        in_specs.append(pl.BlockSpec((SUBLANES, bk), lambda j, i, kk: ((j + 1) * (bn // SUBLANES), 0)))
        args.append(w)
    if epilogue in ("residual", "residual_prenorm"):
        in_specs.append(pl.BlockSpec((bm, bn), lambda j, i, kk: (i, j)))
        args.append(x)
    if epilogue == "residual_prenorm":
        in_specs.append(pl.BlockSpec((1, bn), lambda j, i, kk: (0, j)))
        args.append(norm_g)
    if epilogue == "rownorm_relu2":
        in_specs.append(pl.BlockSpec((bm, row_ss.shape[1]), lambda j, i, kk: (i, 0)))
        args.append(row_ss)
    out_specs = [pl.BlockSpec((bm, bn), lambda j, i, kk: (i, j))]
    out_shape = [jax.ShapeDtypeStruct((m, n), out_dtype)]
    if epilogue == "residual_prenorm":
        out_specs += [pl.BlockSpec((bm, bn), lambda j, i, kk: (i, j)),
                      pl.BlockSpec((bm, LANES), lambda j, i, kk: (i, j))]
        out_shape += [jax.ShapeDtypeStruct((m, n), BF16),
                      jax.ShapeDtypeStruct((m, (n // bn) * LANES), F32)]
    if cast_w:
        out_specs.append(pl.BlockSpec(w_blk, w_map))
        out_shape.append(jax.ShapeDtypeStruct((n, k) if nt else (k, n), BF16))
    res = pl.pallas_call(
        functools.partial(_gemm_kernel, n_a=len(a_list), nt=nt, cast_w=cast_w,
                          shifted_from=shifted_from, epilogue=epilogue),
        grid=(n // bn, m // bm, k_steps),
        in_specs=in_specs,
        out_specs=out_specs,
        out_shape=out_shape,
        compiler_params=_params(("parallel", "parallel", "arbitrary")),
        name=name,
    )(*args)
    return list(res)


def _segment_masks(rows, nseg):
    ri = lax.broadcasted_iota(jnp.int32, (rows, rows), 0)
    ci = lax.broadcasted_iota(jnp.int32, (rows, rows), 1)
    if nseg == 1:
        same = None
        causal = ci <= ri
        upper = ri <= ci
    else:
        shift = int(math.log2(rows // nseg))
        same = lax.shift_right_logical(ri, shift) == lax.shift_right_logical(ci, shift)
        causal = (ci <= ri) & same
        upper = (ri <= ci) & same
    return same, causal, upper


def _state_dot(qb, state_bf16, nseg):
    if nseg == 1:
        return jnp.dot(qb, state_bf16[0], preferred_element_type=F32)
    rps = qb.shape[0] // nseg
    assert 2 * rps == BF16_ROWS
    outs = []
    for p in range(nseg // 2):
        qp = qb[p * BF16_ROWS:(p + 1) * BF16_ROWS]
        outs.append(jnp.dot(qp, state_bf16[2 * p], preferred_element_type=F32)[:rps])
        outs.append(jnp.dot(qp, state_bf16[2 * p + 1], preferred_element_type=F32)[rps:])
    return jnp.concatenate(outs, axis=0)


def _segment_lhs(xt, j, nseg):
    if nseg == 1:
        return xt.astype(BF16)
    rps = xt.shape[1] // nseg
    lane = lax.broadcasted_iota(jnp.int32, xt.shape, 1)
    keep = (lane >= j * rps) & (lane < (j + 1) * rps)
    return jnp.where(keep, xt, 0.0).astype(BF16)


def _mlstm_kernel(*refs, rows, nseg, has_init, nh):
    if has_init:
        (q_ref, k_ref, v_ref, om_ref, gc_ref, gr_ref, gh_ref, m0_ref, c0_ref, n0_ref,
         h_ref, c_ref, n_ref, m_ref) = refs
    else:
        (q_ref, k_ref, v_ref, om_ref, gc_ref, gr_ref, gh_ref,
         h_ref, c_ref, n_ref, m_ref) = refs
    for hh in range(nh):
        qk = slice(hh * DK_M, (hh + 1) * DK_M)
        vv = slice(hh * DV_M, (hh + 1) * DV_M)
        head_refs = [q_ref.at[:, qk], k_ref.at[:, qk], v_ref.at[:, vv], om_ref.at[:, vv],
                     gc_ref, gr_ref, gh_ref.at[:, vv]]
        if has_init:
            head_refs += [m0_ref, c0_ref.at[:, hh], n0_ref.at[:, hh]]
        head_refs += [h_ref.at[:, vv], c_ref.at[:, hh], n_ref.at[:, hh], m_ref.at[hh]]
        _mlstm_head(pl.program_id(1) * nh + hh, head_refs, rows, nseg, has_init)


def _mlstm_head(head, refs, rows, nseg, has_init):
    if has_init:
        (q_ref, k_ref, v_ref, om_ref, gc_ref, gr_ref, gh_ref, m0_ref, c0_ref, n0_ref,
         h_ref, c_ref, n_ref, m_ref) = refs
    else:
        (q_ref, k_ref, v_ref, om_ref, gc_ref, gr_ref, gh_ref,
         h_ref, c_ref, n_ref, m_ref) = refs
    rps = rows // nseg
    lane = lax.broadcasted_iota(jnp.int32, (rows, LANES), 1)
    sub = lax.broadcasted_iota(jnp.int32, (SUBLANES, rows), 0)

    def pick_col(x_ref, idx):
        return jnp.sum(jnp.where(lane == idx, x_ref[...], 0.0), axis=1, keepdims=True)

    def pick_row(x_ref, idx):
        return jnp.sum(jnp.where(sub == idx, x_ref[...], 0.0), axis=0, keepdims=True)

    if has_init:
        c_prev, n_prev = c0_ref, n0_ref
        m_col = pick_col(m0_ref, head)
    else:
        @pl.when(pl.program_id(2) == 0)
        def _zero_state():
            c_ref[...] = jnp.zeros(c_ref.shape, F32)
            n_ref[...] = jnp.zeros(n_ref.shape, F32)
            m_ref[...] = jnp.zeros(m_ref.shape, F32)
        c_prev, n_prev = c_ref, n_ref
        m_col = m_ref[:, 0:1]

    i_col, f_col = pick_col(gc_ref, head), pick_col(gc_ref, head + HM)
    i_row, f_row = pick_row(gr_ref, head), pick_row(gr_ref, head + HM)

    same, causal, upper = _segment_masks(rows, nseg)
    b_col = jnp.sum(jnp.where(causal, f_row, 0.0), axis=1, keepdims=True)
    b_row = jnp.sum(jnp.where(upper, f_col, 0.0), axis=0, keepdims=True)
    if same is None:
        bl_col = jnp.sum(f_row, axis=1, keepdims=True) + jnp.zeros_like(b_col)
        bl_row = jnp.sum(f_col, axis=0, keepdims=True) + jnp.zeros_like(b_row)
    else:
        bl_col = jnp.sum(jnp.where(same, f_row, 0.0), axis=1, keepdims=True)
        bl_row = jnp.sum(jnp.where(same, f_col, 0.0), axis=0, keepdims=True)

    dlog = b_col - b_row + i_row
    dmax = jnp.max(jnp.where(causal, dlog, -jnp.inf), axis=1, keepdims=True)
    inter = b_col + m_col
    mt = jnp.maximum(inter, dmax)
    dw = jnp.where(causal, jnp.exp(dlog - mt), 0.0)
    iw = jnp.exp(inter - mt)

    q = q_ref[...] * (DK_M ** -0.5)
    k = k_ref[...]
    qb, kb, vb = q.astype(BF16), k.astype(BF16), v_ref[...].astype(BF16)
    s = lax.dot_general(qb, kb, (((1,), (1,)), ((), ())), preferred_element_type=F32) * dw
    intra = jnp.dot(s.astype(BF16), vb, preferred_element_type=F32)
    q_c = _state_dot(qb, [c_prev[j].astype(BF16) for j in range(nseg)], nseg)
    n_tok = jnp.broadcast_to(n_prev[...], (nseg, rps, DK_M)).reshape(rows, DK_M)
    q_n = jnp.sum(q * n_tok, axis=1, keepdims=True)
    num = iw * q_c + intra
    den = iw * q_n + jnp.sum(s, axis=1, keepdims=True)
    hh = num * (1.0 / jnp.maximum(jnp.abs(den), jnp.exp(-mt)))
    h_ref[...] = (_rms(hh, gh_ref[...]) * _sigmoid(om_ref[...])).astype(BF16)

    wlog_col = bl_col - b_col + i_col
    wlog_row = bl_row - b_row + i_row
    if same is None:
        wmax = jnp.max(wlog_row, axis=1, keepdims=True) + jnp.zeros_like(b_col)
    else:
        wmax = jnp.max(jnp.where(same, wlog_row, -jnp.inf), axis=1, keepdims=True)
    m_new = jnp.maximum(bl_col + m_col, wmax)
    decay = jnp.exp(bl_col + m_col - m_new)
    kw = k * jnp.exp(wlog_col - m_new)
    kw_t = kw.T
    for j in range(nseg):
        d_j = decay[j * rps:j * rps + 1, :]
        upd = jnp.dot(_segment_lhs(kw_t, j, nseg), vb, preferred_element_type=F32)
        c_ref[j] = d_j * c_prev[j] + upd
        n_ref[j] = d_j * n_prev[j] + jnp.sum(kw[j * rps:(j + 1) * rps], axis=0, keepdims=True)
    m_ref[...] = jnp.broadcast_to(m_new, (rows, LANES))


def _mlstm(proj, gc, gr, g_head, rows, nseg, nchunks, nh, init=None):
    m = proj.shape[0]
    groups = m // (rows * nchunks)
    has_init = init is not None
    assert not has_init or nchunks == 1
    row = lambda g, h, c: g * nchunks + c
    qk_blk, v_blk = nh * DK_M, nh * DV_M
    in_specs = [
        pl.BlockSpec((rows, qk_blk), lambda g, h, c: (row(g, h, c), OFF_QM // qk_blk + h)),
        pl.BlockSpec((rows, qk_blk), lambda g, h, c: (row(g, h, c), OFF_KM // qk_blk + h)),
        pl.BlockSpec((rows, v_blk), lambda g, h, c: (row(g, h, c), OFF_VM // v_blk + h)),
        pl.BlockSpec((rows, v_blk), lambda g, h, c: (row(g, h, c), OFF_OM // v_blk + h)),
        pl.BlockSpec((rows, LANES), lambda g, h, c: (row(g, h, c), 0)),
        pl.BlockSpec((SUBLANES, rows), lambda g, h, c: (0, row(g, h, c))),
        pl.BlockSpec((1, v_blk), lambda g, h, c: (0, h)),
    ]
    args = [proj, proj, proj, proj, gc, gr, g_head]
    if has_init:
        m0_tok, c0, n0 = init
        in_specs += [
            pl.BlockSpec((rows, LANES), lambda g, h, c: (g, 0)),
            pl.BlockSpec((nseg, nh, DK_M, DV_M), lambda g, h, c: (g, h, 0, 0)),
            pl.BlockSpec((nseg, nh, 1, DK_M), lambda g, h, c: (g, h, 0, 0)),
        ]
        args += [m0_tok, c0, n0]
    nb = groups * nseg
    return pl.pallas_call(
        functools.partial(_mlstm_kernel, rows=rows, nseg=nseg, has_init=has_init, nh=nh),
        grid=(groups, HM // nh, nchunks),
        in_specs=in_specs,
        out_specs=[
            pl.BlockSpec((rows, v_blk), lambda g, h, c: (row(g, h, c), h)),
            pl.BlockSpec((nseg, nh, DK_M, DV_M), lambda g, h, c: (g, h, 0, 0)),
            pl.BlockSpec((nseg, nh, 1, DK_M), lambda g, h, c: (g, h, 0, 0)),
            pl.BlockSpec((None, nh, rows, LANES), lambda g, h, c: (g, h, 0, 0)),
        ],
        out_shape=[
            jax.ShapeDtypeStruct((m, HM * DV_M), BF16),
            jax.ShapeDtypeStruct((nb, HM, DK_M, DV_M), F32),
            jax.ShapeDtypeStruct((nb, HM, 1, DK_M), F32),
            jax.ShapeDtypeStruct((groups, HM, rows, LANES), F32),
        ],
        compiler_params=_params(("parallel", "parallel", "arbitrary")),
        name="mlstm",
    )(*args)


def _ret_kernel(*refs, rows, nseg, has_init, nh):
    if has_init:
        (q_ref, k_ref, v_ref, g_ref, cos_ref, sin_ref, dmat_ref, dec_ref, gh_ref, s0_ref,
         o_ref, s_ref) = refs
    else:
        (q_ref, k_ref, v_ref, g_ref, cos_ref, sin_ref, dmat_ref, dec_ref, gh_ref,
         o_ref, s_ref) = refs
    for hh in range(nh):
        qk = slice(hh * DK_R, (hh + 1) * DK_R)
        vv = slice(hh * DV_R, (hh + 1) * DV_R)
        head_refs = [q_ref.at[:, qk], k_ref.at[:, qk], v_ref.at[:, vv], g_ref.at[:, vv],
                     cos_ref, sin_ref, dmat_ref.at[hh], dec_ref.at[hh], gh_ref.at[:, vv]]
        if has_init:
            head_refs.append(s0_ref.at[:, hh])
        head_refs += [o_ref.at[:, vv], s_ref.at[:, hh]]
        _ret_head(head_refs, rows, nseg, has_init)


def _ret_head(refs, rows, nseg, has_init):
    if has_init:
        (q_ref, k_ref, v_ref, g_ref, cos_ref, sin_ref, dmat_ref, dec_ref, gh_ref, s0_ref,
         o_ref, s_ref) = refs
        s_prev = s0_ref
    else:
        (q_ref, k_ref, v_ref, g_ref, cos_ref, sin_ref, dmat_ref, dec_ref, gh_ref,
         o_ref, s_ref) = refs
        s_prev = s_ref

        @pl.when(pl.program_id(2) == 0)
        def _zero_state():
            s_ref[...] = jnp.zeros(s_ref.shape, F32)

    cosf, sins = cos_ref[...], sin_ref[...]
    half = DK_R // 2

    def rot(x):
        return x * cosf + pltpu.roll(x, half, 1) * sins

    qr = rot(q_ref[...])
    kr = rot(k_ref[...]) * (DK_R ** -0.5)
    dec = dec_ref[...]
    inter, kdec, sdec = dec[:, 0:1], dec[:, 1:2], dec[0:1, 2:3]
    qb, kb, vb = qr.astype(BF16), kr.astype(BF16), v_ref[...].astype(BF16)
    s = lax.dot_general(qb, kb, (((1,), (1,)), ((), ())), preferred_element_type=F32) * dmat_ref[...]
    q_s = _state_dot(qb, [s_prev[j].astype(BF16) for j in range(nseg)], nseg)
    o = jnp.dot(s.astype(BF16), vb, preferred_element_type=F32) + inter * q_s
    g = g_ref[...]
    o_ref[...] = (_rms(o, gh_ref[...]) * (g * _sigmoid(g))).astype(BF16)

    kd_t = (kr * kdec).T
    for j in range(nseg):
        upd = jnp.dot(_segment_lhs(kd_t, j, nseg), vb, preferred_element_type=F32)
        s_ref[j] = sdec * s_prev[j] + upd


def _retention(proj, cos_t, sin_t, dmat, dec, g_head, rows, nseg, nchunks, nh, init=None):
    m = proj.shape[0]
    groups = m // (rows * nchunks)
    has_init = init is not None
    assert not has_init or nchunks == 1
    row = lambda g, h, c: g * nchunks + c
    qk_blk, v_blk = nh * DK_R, nh * DV_R
    in_specs = [
        pl.BlockSpec((rows, qk_blk), lambda g, h, c: (row(g, h, c), OFF_QR // qk_blk + h)),
        pl.BlockSpec((rows, qk_blk), lambda g, h, c: (row(g, h, c), OFF_KR // qk_blk + h)),
        pl.BlockSpec((rows, v_blk), lambda g, h, c: (row(g, h, c), OFF_VR // v_blk + h)),
        pl.BlockSpec((rows, v_blk), lambda g, h, c: (row(g, h, c), OFF_GR // v_blk + h)),
        pl.BlockSpec((rows, DK_R), lambda g, h, c: (c, 0)),
        pl.BlockSpec((rows, DK_R), lambda g, h, c: (c, 0)),
        pl.BlockSpec((nh, rows, rows), lambda g, h, c: (h, 0, 0)),
        pl.BlockSpec((nh, rows, LANES), lambda g, h, c: (h, 0, 0)),
        pl.BlockSpec((1, v_blk), lambda g, h, c: (0, h)),
    ]
    args = [proj, proj, proj, proj, cos_t, sin_t, dmat, dec, g_head]
    if has_init:
        in_specs.append(pl.BlockSpec((nseg, nh, DK_R, DV_R), lambda g, h, c: (g, h, 0, 0)))
        args.append(init)
    nb = groups * nseg
    return pl.pallas_call(
        functools.partial(_ret_kernel, rows=rows, nseg=nseg, has_init=has_init, nh=nh),
        grid=(groups, HR // nh, nchunks),
        in_specs=in_specs,
        out_specs=[
            pl.BlockSpec((rows, v_blk), lambda g, h, c: (row(g, h, c), h)),
            pl.BlockSpec((nseg, nh, DK_R, DV_R), lambda g, h, c: (g, h, 0, 0)),
        ],
        out_shape=[
            jax.ShapeDtypeStruct((m, HR * DV_R), BF16),
            jax.ShapeDtypeStruct((nb, HR, DK_R, DV_R), F32),
        ],
        compiler_params=_params(("parallel", "parallel", "arbitrary")),
        name="retention",
    )(*args)


def _retention_tables(seg_len, nseg):
    rows = seg_len * nseg
    idx = (np.arange(rows) % seg_len).astype(np.float64)
    seg = np.arange(rows) // seg_len
    lg = np.log(1.0 - 2.0 ** (-5.0 - np.arange(HR, dtype=np.float64)))
    diff = idx[:, None] - idx[None, :]
    ok = (diff >= 0) & (seg[:, None] == seg[None, :])
    dmat = np.where(ok[None], np.exp(np.maximum(diff, 0.0)[None] * lg[:, None, None]), 0.0)
    dec = np.zeros((HR, rows, LANES))
    dec[:, :, 0] = np.exp((idx[None, :] + 1.0) * lg[:, None])
    dec[:, :, 1] = np.exp((seg_len - 1.0 - idx[None, :]) * lg[:, None])
    dec[:, :, 2] = np.exp(seg_len * lg)[:, None]
    return jnp.asarray(dmat, F32), jnp.asarray(dec, F32)


def _rotary_tables(pos):
    freqs = ROPE_BASE ** (-jnp.arange(0, DK_R, 2, dtype=F32) / DK_R)
    ang = pos[:, None] * freqs[None, :]
    cos, sin = jnp.cos(ang), jnp.sin(ang)
    return jnp.concatenate([cos, cos], axis=-1), jnp.concatenate([-sin, sin], axis=-1)


def _rmsnorm_kernel(x_ref, g_ref, o_ref):
    o_ref[...] = _rms(x_ref[...], g_ref[...]).astype(o_ref.dtype)


def _rmsnorm(x2d, g, out_dtype, bm=512):
    m = x2d.shape[0]
    return pl.pallas_call(
        _rmsnorm_kernel,
        grid=(m // bm,),
        in_specs=[
            pl.BlockSpec((bm, D_MODEL), lambda i: (i, 0)),
            pl.BlockSpec((1, D_MODEL), lambda i: (0, 0)),
        ],
        out_specs=pl.BlockSpec((bm, D_MODEL), lambda i: (i, 0)),
        out_shape=jax.ShapeDtypeStruct((m, D_MODEL), out_dtype),
        compiler_params=_params(("parallel",)),
        name="rmsnorm",
    )(x2d, g)


def _layer(x2d, w, wb, tables, rows, nseg, nchunks, init, g_final, final_norm):
    cast = wb is None
    assert not cast or x2d.shape[0] == GEMM_BM
    cfg = dict(bm=GEMM_BM, bn=GEMM_BN_CAST if cast else GEMM_BN, cast_w=cast)
    new_wb = {}

    def gemm(key, a_list, **kw):
        res = _gemm(a_list, w[key] if cast else wb[key], name=key, **{**cfg, **kw})
        if cast:
            new_wb[key] = res.pop()
        return res[0] if len(res) == 1 else res

    h, gc, gr = _norm_gates(x2d, w["g_mix"], w["wg"], w["gate_bias"])
    proj = gemm("in_proj", [h], nt=True, shifted=cast)
    m_init = None if init is None else (init["m_tok"], init["C"], init["n"])
    nh_m = HM if init is None else 1
    hm, c_new, n_new, m_slab = _mlstm(proj, gc, gr, w["g_mh"], rows, nseg, nchunks, nh_m, m_init)
    s_init = None if init is None else init["S"]
    nh_r = HR if init is None else 1
    hr, s_new = _retention(proj, *tables, w["g_rh"], rows, nseg, nchunks, nh_r, s_init)
    wide = dict(bn=GEMM_BN_CAST_WIDE) if cast else {}
    x1, x1g, x1_ss = gemm("out_proj", [hm, hr], epilogue="residual_prenorm", x=x2d,
                          norm_g=w["g_ffn"], **wide)
    act = gemm("ffn_up", [x1g], epilogue="rownorm_relu2", row_ss=x1_ss, out_dtype=BF16, **wide)
    down_cfg = dict(bn=GEMM_BN, bk=FFN_DOWN_BK_CAST) if cast else dict(bk=FFN_DOWN_BK)
    y = gemm("ffn_down", [act], epilogue="residual", x=x1, **down_cfg)
    if final_norm:
        y = _rmsnorm(y, g_final, F32)
    return y, c_new, n_new, m_slab, s_new, (new_wb if cast else wb)


def kernel(x_prompt, x_sample, state_mlstm_C, state_mlstm_n, state_mlstm_m, state_ret_S, w_in, b_igate, b_fgate, g_mlstm_head, g_ret_head, w_out, g_norm_mix, g_norm_ffn, w_up, w_down, g_final):
    depth = w_in.shape[0]
    batch, seq, _ = x_prompt.shape
    dec_batch, dec_seq, _ = x_sample.shape
    assert seq % PROMPT_CHUNK == 0 and SAMPLE_ROWS % dec_seq == 0
    assert (dec_batch * dec_seq) % SAMPLE_ROWS == 0
    seg_s = SAMPLE_ROWS // dec_seq
    nchunks_p = seq // PROMPT_CHUNK

    cos_p, sin_p = _rotary_tables(jnp.arange(seq, dtype=F32) + 0.0)
    pos_s = jnp.arange(dec_seq, dtype=F32) + float(PAST_LEN)
    cos_s, sin_s = _rotary_tables(jnp.tile(pos_s, seg_s))
    tables_p = (cos_p, sin_p) + _retention_tables(PROMPT_CHUNK, 1)
    tables_s = (cos_s, sin_s) + _retention_tables(dec_seq, seg_s)

    yp = x_prompt.reshape(batch * seq, D_MODEL)
    ys = x_sample.reshape(dec_batch * dec_seq, D_MODEL)
    g_fin = g_final.reshape(1, D_MODEL)
    outs = {k: [] for k in ("pC", "pn", "pm", "pS", "sC", "sn", "sm", "sS")}
    for l in range(depth):
        w_in_t = w_in[l].T
        w_gate_t = w_in_t[GATE_OFF:GATE_OFF + 2 * HM]
        w = {
            "in_proj": w_in_t,
            "wg": jnp.pad(w_gate_t, ((0, LANES - 2 * HM), (0, 0))).astype(BF16),
            "gate_bias": jnp.pad(jnp.concatenate([b_igate[l], b_fgate[l]]), (0, LANES - 2 * HM)).reshape(1, LANES),
            "g_mix": g_norm_mix[l].reshape(1, D_MODEL),
            "g_ffn": g_norm_ffn[l].reshape(1, D_MODEL),
            "g_mh": g_mlstm_head[l].reshape(1, HM * DV_M),
            "g_rh": g_ret_head[l].reshape(1, HR * DV_R),
            "out_proj": w_out[l],
            "ffn_up": w_up[l],
            "ffn_down": w_down[l],
        }
        last = l == depth - 1
        m_tok = jnp.pad(jnp.repeat(state_mlstm_m[l], dec_seq, axis=0), ((0, 0), (0, LANES - HM)))
        init = {"m_tok": m_tok, "C": state_mlstm_C[l],
                "n": state_mlstm_n[l].reshape(dec_batch, HM, 1, DK_M), "S": state_ret_S[l]}
        ys, c, n, m_slab, s, wb = _layer(ys, w, None, tables_s, SAMPLE_ROWS, seg_s, 1, init, g_fin, last)
        outs["sC"].append(c)
        outs["sn"].append(n.reshape(dec_batch, HM, DK_M))
        m_seg = m_slab[:, :, ::dec_seq, 0]
        outs["sm"].append(m_seg.transpose(0, 2, 1).reshape(dec_batch, HM))
        outs["sS"].append(s)

        yp, c, n, m_slab, s, _ = _layer(yp, w, wb, tables_p, PROMPT_CHUNK, 1, nchunks_p, None, g_fin, last)
        outs["pC"].append(c)
        outs["pn"].append(n.reshape(batch, HM, DK_M))
        outs["pm"].append(m_slab[:, :, 0, 0])
        outs["pS"].append(s)

    def stack(xs):
        return xs[0][None] if len(xs) == 1 else jnp.stack(xs)

    return (yp.reshape(batch, seq, D_MODEL), ys.reshape(dec_batch, dec_seq, D_MODEL),
            stack(outs["pC"]), stack(outs["pn"]), stack(outs["pm"]), stack(outs["pS"]),
            stack(outs["sC"]), stack(outs["sn"]), stack(outs["sm"]), stack(outs["sS"]))
```

```python
import functools
import math

import jax
import jax.numpy as jnp
import numpy as np
from jax import lax
from jax.experimental import pallas as pl
from jax.experimental.pallas import tpu as pltpu

F32 = jnp.float32
BF16 = jnp.bfloat16

D_MODEL = 4096
HM, DK_M, DV_M = 4, 256, 512
HR, DK_R, DV_R = 8, 128, 256
D_FF = 4 * D_MODEL
PAST_LEN = 16384
GATE_SOFTCAP = 15.0
ROPE_BASE = 10000.0
EPS = 1e-6

LANES = 128
SUBLANES = 8
BF16_ROWS = 16
VMEM_LIMIT = 60 * 1024 * 1024

N_MAIN = 2 * HM * DK_M + 2 * HM * DV_M + 2 * HR * DK_R + 2 * HR * DV_R
GATE_OFF = 2 * HM * DK_M + 2 * HM * DV_M
OFF_QM, OFF_KM, OFF_VM, OFF_OM = 0, HM * DK_M, 2 * HM * DK_M, 2 * HM * DK_M + HM * DV_M
OFF_QR = GATE_OFF
OFF_KR = OFF_QR + HR * DK_R
OFF_VR = OFF_KR + HR * DK_R
OFF_GR = OFF_VR + HR * DV_R

PROMPT_CHUNK = 256
SAMPLE_ROWS = 128
RET_HEADS_WITH_STATE = 4


_NN = (((1,), (0,)), ((), ()))
_NT = (((1,), (1,)), ((), ()))

GEMM_BM = 1024
GEMM_BN = 1024
GEMM_BN_CAST = 512
ACC_CHUNK = 256
FFN_DOWN_BK = 4096
FFN_DOWN_BK_CAST = 2048


def _params(sem):
    return pltpu.CompilerParams(dimension_semantics=sem, vmem_limit_bytes=VMEM_LIMIT)


def _sigmoid(x):
    return 1.0 / (1.0 + jnp.exp(-x))


def _rms(x, g):
    return x * lax.rsqrt(jnp.mean(x * x, axis=-1, keepdims=True) + EPS) * g


def _norm_gates_kernel(x_ref, g_ref, wg_ref, bias_ref, h_ref, gc_ref, gr_ref):
    hb = _rms(x_ref[...], g_ref[...]).astype(BF16)
    h_ref[...] = hb
    z = lax.dot_general(hb, wg_ref[...], _NT, preferred_element_type=F32) + bias_ref[...]
    zc = GATE_SOFTCAP * jnp.tanh(z / GATE_SOFTCAP)
    logsig = jnp.minimum(zc, 0.0) - jnp.log1p(jnp.exp(-jnp.abs(zc)))
    lane = lax.broadcasted_iota(jnp.int32, zc.shape, 1)
    out = jnp.where(lane >= HM, logsig, zc)
    gc_ref[...] = out
    gr_ref[...] = out.T[:SUBLANES, :]


def _norm_gates(x2d, g, wg, bias, bm=512):
    m = x2d.shape[0]
    return pl.pallas_call(
        _norm_gates_kernel,
        grid=(m // bm,),
        in_specs=[
            pl.BlockSpec((bm, D_MODEL), lambda i: (i, 0)),
            pl.BlockSpec((1, D_MODEL), lambda i: (0, 0)),
            pl.BlockSpec((LANES, D_MODEL), lambda i: (0, 0)),
            pl.BlockSpec((1, LANES), lambda i: (0, 0)),
        ],
        out_specs=[
            pl.BlockSpec((bm, D_MODEL), lambda i: (i, 0)),
            pl.BlockSpec((bm, LANES), lambda i: (i, 0)),
            pl.BlockSpec((SUBLANES, bm), lambda i: (0, i)),
        ],
        out_shape=[
            jax.ShapeDtypeStruct((m, D_MODEL), BF16),
            jax.ShapeDtypeStruct((m, LANES), F32),
            jax.ShapeDtypeStruct((SUBLANES, m), F32),
        ],
        compiler_params=_params(("parallel",)),
        name="norm_gates",
    )(x2d, g, wg, bias)


def _gemm_kernel(*refs, n_a, nt, cast_w, shifted_from, epilogue):
    a_refs, w_ref = refs[:n_a], refs[n_a]
    rest = list(refs[n_a + 1:])
    w_hi_ref = rest.pop(0) if shifted_from is not None else None
    x_ref = rest.pop(0) if epilogue in ("residual", "residual_prenorm") else None
    g_ref = rest.pop(0) if epilogue == "residual_prenorm" else None
    ss_in_ref = rest.pop(0) if epilogue == "rownorm_relu2" else None
    o_ref = rest.pop(0)
    if epilogue == "residual_prenorm":
        xg_ref, ss_ref = rest.pop(0), rest.pop(0)

    wb_ref = rest.pop(0) if cast_w else None

    def product(cols, shift=0):
        w_src = w_ref
        if cast_w:
            w_src = wb_ref
            if nt:
                lo, hi = cols.start + shift, cols.stop + shift
                if hi <= w_ref.shape[0]:
                    blk = w_ref[lo:hi, :]
                else:
                    blk = jnp.concatenate([w_ref[lo:, :], w_hi_ref[...]], axis=0)
                wb_ref[cols, :] = blk.astype(BF16)
            else:
                wb_ref[:, cols] = w_ref[:, cols].astype(BF16)
        acc, off = None, 0
        for a_ref in a_refs:
            ka = a_ref.shape[1]
            wk = w_src[cols, off:off + ka] if nt else w_src[off:off + ka, cols]
            part = lax.dot_general(a_ref[...], wk, _NT if nt else _NN, preferred_element_type=F32)
            acc = part if acc is None else acc + part
            off += ka
        return acc

    if epilogue == "residual":
        @pl.when(pl.program_id(2) == 0)
        def _first():
            o_ref[...] = x_ref[...]

        bn = o_ref.shape[1]
        for c in range(0, bn, ACC_CHUNK):
            cols = slice(c, min(c + ACC_CHUNK, bn))
            o_ref[:, cols] += product(cols)
    elif epilogue == "residual_prenorm":
        bn = o_ref.shape[1]
        ss = jnp.zeros(ss_ref.shape, F32)
        for c in range(0, bn, ACC_CHUNK):
            cols = slice(c, min(c + ACC_CHUNK, bn))
            val = x_ref[:, cols] + product(cols)
            o_ref[:, cols] = val
            xg_ref[:, cols] = (val * g_ref[:, cols]).astype(BF16)
            sq = val * val
            for t in range(0, sq.shape[1], LANES):
                ss = ss + sq[:, t:t + LANES]
        ss_ref[...] = ss
    elif epilogue == "rownorm_relu2":
        mean_sq = jnp.sum(ss_in_ref[...], axis=1, keepdims=True) * (1.0 / D_MODEL)
        scale = lax.rsqrt(mean_sq + EPS)
        bn = o_ref.shape[1]
        for c in range(0, bn, ACC_CHUNK):
            cols = slice(c, min(c + ACC_CHUNK, bn))
            u = product(cols) * scale
            o_ref[:, cols] = jnp.square(jnp.maximum(u, 0.0)).astype(o_ref.dtype)
    else:
        bn = o_ref.shape[1]
        step = ACC_CHUNK if cast_w else bn

        def run(shift):
            for c in range(0, bn, step):
                cols = slice(c, min(c + step, bn))
                o_ref[:, cols] = product(cols, shift)

        if shifted_from is None:
            run(0)
        else:
            pl.when(pl.program_id(0) < shifted_from)(functools.partial(run, 0))
            pl.when(pl.program_id(0) >= shifted_from)(functools.partial(run, SUBLANES))


def _gemm(a_list, w, *, name, bm, bn, bk=None, nt=False, cast_w=False, shifted=False,
          epilogue=None, x=None, norm_g=None, row_ss=None, out_dtype=F32):
    m = a_list[0].shape[0]
    k = sum(a.shape[1] for a in a_list)
    n = N_MAIN if shifted else (w.shape[0] if nt else w.shape[1])
    bk = k if bk is None else bk
    k_steps = k // bk
    assert len(a_list) == 1 or k_steps == 1
    assert epilogue == "residual" or k_steps == 1
    assert not cast_w or m == bm
    assert epilogue != "rownorm_relu2" or k == D_MODEL
    in_specs = [pl.BlockSpec((bm, a.shape[1] if k_steps == 1 else bk), lambda j, i, kk: (i, kk))
                for a in a_list]
    w_blk = (bn, bk) if nt else (bk, bn)
    w_map = (lambda j, i, kk: (j, kk)) if nt else (lambda j, i, kk: (kk, j))
    in_specs.append(pl.BlockSpec(w_blk, w_map))
    args = [*a_list, w]
    shifted_from = None
    if shifted:
        assert nt and cast_w and k_steps == 1 and GATE_OFF % bn == 0
        shifted_from = GATE_OFF // bn
        in_specs.append(pl.BlockSpec((SUBLANES, bk), lambda j, i, kk: ((j + 1) * (bn // SUBLANES), 0)))
        args.append(w)
    if epilogue in ("residual", "residual_prenorm"):
        in_specs.append(pl.BlockSpec((bm, bn), lambda j, i, kk: (i, j)))
        args.append(x)
    if epilogue == "residual_prenorm":
        in_specs.append(pl.BlockSpec((1, bn), lambda j, i, kk: (0, j)))
        args.append(norm_g)
    if epilogue == "rownorm_relu2":
        in_specs.append(pl.BlockSpec((bm, row_ss.shape[1]), lambda j, i, kk: (i, 0)))
        args.append(row_ss)
    out_specs = [pl.BlockSpec((bm, bn), lambda j, i, kk: (i, j))]
    out_shape = [jax.ShapeDtypeStruct((m, n), out_dtype)]
    if epilogue == "residual_prenorm":
        out_specs += [pl.BlockSpec((bm, bn), lambda j, i, kk: (i, j)),
                      pl.BlockSpec((bm, LANES), lambda j, i, kk: (i, j))]
        out_shape += [jax.ShapeDtypeStruct((m, n), BF16),
                      jax.ShapeDtypeStruct((m, (n // bn) * LANES), F32)]
    if cast_w:
        out_specs.append(pl.BlockSpec(w_blk, w_map))
        out_shape.append(jax.ShapeDtypeStruct((n, k) if nt else (k, n), BF16))
    res = pl.pallas_call(
        functools.partial(_gemm_kernel, n_a=len(a_list), nt=nt, cast_w=cast_w,
                          shifted_from=shifted_from, epilogue=epilogue),
        grid=(n // bn, m // bm, k_steps),
        in_specs=in_specs,
        out_specs=out_specs,
        out_shape=out_shape,
        compiler_params=_params(("parallel", "parallel", "arbitrary")),
        name=name,
    )(*args)
    return list(res)


def _segment_masks(rows, nseg):
    ri = lax.broadcasted_iota(jnp.int32, (rows, rows), 0)
    ci = lax.broadcasted_iota(jnp.int32, (rows, rows), 1)
    if nseg == 1:
        same = None
        causal = ci <= ri
        upper = ri <= ci
    else:
        shift = int(math.log2(rows // nseg))
        same = lax.shift_right_logical(ri, shift) == lax.shift_right_logical(ci, shift)
        causal = (ci <= ri) & same
        upper = (ri <= ci) & same
    return same, causal, upper


def _state_dot(qb, state_bf16, nseg):
    if nseg == 1:
        return jnp.dot(qb, state_bf16[0], preferred_element_type=F32)
    rps = qb.shape[0] // nseg
    assert 2 * rps == BF16_ROWS
    outs = []
    for p in range(nseg // 2):
        qp = qb[p * BF16_ROWS:(p + 1) * BF16_ROWS]
        outs.append(jnp.dot(qp, state_bf16[2 * p], preferred_element_type=F32)[:rps])
        outs.append(jnp.dot(qp, state_bf16[2 * p + 1], preferred_element_type=F32)[rps:])
    return jnp.concatenate(outs, axis=0)


def _segment_lhs(xt, j, nseg):
    if nseg == 1:
        return xt.astype(BF16)
    rps = xt.shape[1] // nseg
    lane = lax.broadcasted_iota(jnp.int32, xt.shape, 1)
    keep = (lane >= j * rps) & (lane < (j + 1) * rps)
    return jnp.where(keep, xt, 0.0).astype(BF16)


def _mlstm_kernel(*refs, rows, nseg, has_init, nh):
    if has_init:
        (q_ref, k_ref, v_ref, om_ref, gc_ref, gr_ref, gh_ref, m0_ref, c0_ref, n0_ref,
         h_ref, c_ref, n_ref, m_ref) = refs
    else:
        (q_ref, k_ref, v_ref, om_ref, gc_ref, gr_ref, gh_ref,
         h_ref, c_ref, n_ref, m_ref) = refs
    for hh in range(nh):
        qk = slice(hh * DK_M, (hh + 1) * DK_M)
        vv = slice(hh * DV_M, (hh + 1) * DV_M)
        head_refs = [q_ref.at[:, qk], k_ref.at[:, qk], v_ref.at[:, vv], om_ref.at[:, vv],
                     gc_ref, gr_ref, gh_ref.at[:, vv]]
        if has_init:
            head_refs += [m0_ref, c0_ref.at[:, hh], n0_ref.at[:, hh]]
        head_refs += [h_ref.at[:, vv], c_ref.at[:, hh], n_ref.at[:, hh], m_ref.at[hh]]
        _mlstm_head(pl.program_id(1) * nh + hh, head_refs, rows, nseg, has_init)


def _mlstm_head(head, refs, rows, nseg, has_init):
    if has_init:
        (q_ref, k_ref, v_ref, om_ref, gc_ref, gr_ref, gh_ref, m0_ref, c0_ref, n0_ref,
         h_ref, c_ref, n_ref, m_ref) = refs
    else:
        (q_ref, k_ref, v_ref, om_ref, gc_ref, gr_ref, gh_ref,
         h_ref, c_ref, n_ref, m_ref) = refs
    rps = rows // nseg
    lane = lax.broadcasted_iota(jnp.int32, (rows, LANES), 1)
    sub = lax.broadcasted_iota(jnp.int32, (SUBLANES, rows), 0)

    def pick_col(x_ref, idx):
        return jnp.sum(jnp.where(lane == idx, x_ref[...], 0.0), axis=1, keepdims=True)

    def pick_row(x_ref, idx):
        return jnp.sum(jnp.where(sub == idx, x_ref[...], 0.0), axis=0, keepdims=True)

    if has_init:
        c_prev, n_prev = c0_ref, n0_ref
        m_col = pick_col(m0_ref, head)
    else:
        @pl.when(pl.program_id(2) == 0)
        def _zero_state():
            c_ref[...] = jnp.zeros(c_ref.shape, F32)
            n_ref[...] = jnp.zeros(n_ref.shape, F32)
            m_ref[...] = jnp.zeros(m_ref.shape, F32)
        c_prev, n_prev = c_ref, n_ref
        m_col = m_ref[:, 0:1]

    i_col, f_col = pick_col(gc_ref, head), pick_col(gc_ref, head + HM)
    i_row, f_row = pick_row(gr_ref, head), pick_row(gr_ref, head + HM)

    same, causal, upper = _segment_masks(rows, nseg)
    b_col = jnp.sum(jnp.where(causal, f_row, 0.0), axis=1, keepdims=True)
    b_row = jnp.sum(jnp.where(upper, f_col, 0.0), axis=0, keepdims=True)
    if same is None:
        bl_col = jnp.sum(f_row, axis=1, keepdims=True) + jnp.zeros_like(b_col)
        bl_row = jnp.sum(f_col, axis=0, keepdims=True) + jnp.zeros_like(b_row)
    else:
        bl_col = jnp.sum(jnp.where(same, f_row, 0.0), axis=1, keepdims=True)
        bl_row = jnp.sum(jnp.where(same, f_col, 0.0), axis=0, keepdims=True)

    dlog = b_col - b_row + i_row
    dmax = jnp.max(jnp.where(causal, dlog, -jnp.inf), axis=1, keepdims=True)
    inter = b_col + m_col
    mt = jnp.maximum(inter, dmax)
    dw = jnp.where(causal, jnp.exp(dlog - mt), 0.0)
    iw = jnp.exp(inter - mt)

    q = q_ref[...] * (DK_M ** -0.5)
    k = k_ref[...]
    qb, kb, vb = q.astype(BF16), k.astype(BF16), v_ref[...].astype(BF16)
    s = lax.dot_general(qb, kb, (((1,), (1,)), ((), ())), preferred_element_type=F32) * dw
    intra = jnp.dot(s.astype(BF16), vb, preferred_element_type=F32)
    q_c = _state_dot(qb, [c_prev[j].astype(BF16) for j in range(nseg)], nseg)
    n_tok = jnp.broadcast_to(n_prev[...], (nseg, rps, DK_M)).reshape(rows, DK_M)
    q_n = jnp.sum(q * n_tok, axis=1, keepdims=True)
    num = iw * q_c + intra
    den = iw * q_n + jnp.sum(s, axis=1, keepdims=True)
    hh = num * (1.0 / jnp.maximum(jnp.abs(den), jnp.exp(-mt)))
    h_ref[...] = (_rms(hh, gh_ref[...]) * _sigmoid(om_ref[...])).astype(BF16)

    wlog_col = bl_col - b_col + i_col
    wlog_row = bl_row - b_row + i_row
    if same is None:
        wmax = jnp.max(wlog_row, axis=1, keepdims=True) + jnp.zeros_like(b_col)
    else:
        wmax = jnp.max(jnp.where(same, wlog_row, -jnp.inf), axis=1, keepdims=True)
    m_new = jnp.maximum(bl_col + m_col, wmax)
    decay = jnp.exp(bl_col + m_col - m_new)
    kw = k * jnp.exp(wlog_col - m_new)
    kw_t = kw.T
    for j in range(nseg):
        d_j = decay[j * rps:j * rps + 1, :]
        upd = jnp.dot(_segment_lhs(kw_t, j, nseg), vb, preferred_element_type=F32)
        c_ref[j] = d_j * c_prev[j] + upd
        n_ref[j] = d_j * n_prev[j] + jnp.sum(kw[j * rps:(j + 1) * rps], axis=0, keepdims=True)
    m_ref[...] = jnp.broadcast_to(m_new, (rows, LANES))


def _mlstm(proj, gc, gr, g_head, rows, nseg, nchunks, nh, init=None):
    m = proj.shape[0]
    groups = m // (rows * nchunks)
    has_init = init is not None
    assert not has_init or nchunks == 1
    row = lambda g, h, c: g * nchunks + c
    qk_blk, v_blk = nh * DK_M, nh * DV_M
    in_specs = [
        pl.BlockSpec((rows, qk_blk), lambda g, h, c: (row(g, h, c), OFF_QM // qk_blk + h)),
        pl.BlockSpec((rows, qk_blk), lambda g, h, c: (row(g, h, c), OFF_KM // qk_blk + h)),
        pl.BlockSpec((rows, v_blk), lambda g, h, c: (row(g, h, c), OFF_VM // v_blk + h)),
        pl.BlockSpec((rows, v_blk), lambda g, h, c: (row(g, h, c), OFF_OM // v_blk + h)),
        pl.BlockSpec((rows, LANES), lambda g, h, c: (row(g, h, c), 0)),
        pl.BlockSpec((SUBLANES, rows), lambda g, h, c: (0, row(g, h, c))),
        pl.BlockSpec((1, v_blk), lambda g, h, c: (0, h)),
    ]
    args = [proj, proj, proj, proj, gc, gr, g_head]
    if has_init:
        m0_tok, c0, n0 = init
        in_specs += [
            pl.BlockSpec((rows, LANES), lambda g, h, c: (g, 0)),
            pl.BlockSpec((nseg, nh, DK_M, DV_M), lambda g, h, c: (g, h, 0, 0)),
            pl.BlockSpec((nseg, nh, 1, DK_M), lambda g, h, c: (g, h, 0, 0)),
        ]
        args += [m0_tok, c0, n0]
    nb = groups * nseg
    return pl.pallas_call(
        functools.partial(_mlstm_kernel, rows=rows, nseg=nseg, has_init=has_init, nh=nh),
        grid=(groups, HM // nh, nchunks),
        in_specs=in_specs,
        out_specs=[
            pl.BlockSpec((rows, v_blk), lambda g, h, c: (row(g, h, c), h)),
            pl.BlockSpec((nseg, nh, DK_M, DV_M), lambda g, h, c: (g, h, 0, 0)),
            pl.BlockSpec((nseg, nh, 1, DK_M), lambda g, h, c: (g, h, 0, 0)),
            pl.BlockSpec((None, nh, rows, LANES), lambda g, h, c: (g, h, 0, 0)),
        ],
        out_shape=[
            jax.ShapeDtypeStruct((m, HM * DV_M), BF16),
            jax.ShapeDtypeStruct((nb, HM, DK_M, DV_M), F32),
            jax.ShapeDtypeStruct((nb, HM, 1, DK_M), F32),
            jax.ShapeDtypeStruct((groups, HM, rows, LANES), F32),
        ],
        compiler_params=_params(("parallel", "parallel", "arbitrary")),
        name="mlstm",
    )(*args)


def _ret_kernel(*refs, rows, nseg, has_init, nh):
    if has_init:
        (q_ref, k_ref, v_ref, g_ref, cos_ref, sin_ref, dmat_ref, dec_ref, gh_ref, s0_ref,
         o_ref, s_ref) = refs
    else:
        (q_ref, k_ref, v_ref, g_ref, cos_ref, sin_ref, dmat_ref, dec_ref, gh_ref,
         o_ref, s_ref) = refs
    for hh in range(nh):
        qk = slice(hh * DK_R, (hh + 1) * DK_R)
        vv = slice(hh * DV_R, (hh + 1) * DV_R)
        head_refs = [q_ref.at[:, qk], k_ref.at[:, qk], v_ref.at[:, vv], g_ref.at[:, vv],
                     cos_ref, sin_ref, dmat_ref.at[hh], dec_ref.at[hh], gh_ref.at[:, vv]]
        if has_init:
            head_refs.append(s0_ref.at[:, hh])
        head_refs += [o_ref.at[:, vv], s_ref.at[:, hh]]
        _ret_head(head_refs, rows, nseg, has_init)


def _ret_head(refs, rows, nseg, has_init):
    if has_init:
        (q_ref, k_ref, v_ref, g_ref, cos_ref, sin_ref, dmat_ref, dec_ref, gh_ref, s0_ref,
         o_ref, s_ref) = refs
        s_prev = s0_ref
    else:
        (q_ref, k_ref, v_ref, g_ref, cos_ref, sin_ref, dmat_ref, dec_ref, gh_ref,
         o_ref, s_ref) = refs
        s_prev = s_ref

        @pl.when(pl.program_id(2) == 0)
        def _zero_state():
            s_ref[...] = jnp.zeros(s_ref.shape, F32)

    cosf, sins = cos_ref[...], sin_ref[...]
    half = DK_R // 2

    def rot(x):
        return x * cosf + pltpu.roll(x, half, 1) * sins

    qr = rot(q_ref[...])
    kr = rot(k_ref[...]) * (DK_R ** -0.5)
    dec = dec_ref[...]
    inter, kdec, sdec = dec[:, 0:1], dec[:, 1:2], dec[0:1, 2:3]
    qb, kb, vb = qr.astype(BF16), kr.astype(BF16), v_ref[...].astype(BF16)
    s = lax.dot_general(qb, kb, (((1,), (1,)), ((), ())), preferred_element_type=F32) * dmat_ref[...]
    q_s = _state_dot(qb, [s_prev[j].astype(BF16) for j in range(nseg)], nseg)
    o = jnp.dot(s.astype(BF16), vb, preferred_element_type=F32) + inter * q_s
    g = g_ref[...]
    o_ref[...] = (_rms(o, gh_ref[...]) * (g * _sigmoid(g))).astype(BF16)

    kd_t = (kr * kdec).T
    for j in range(nseg):
        upd = jnp.dot(_segment_lhs(kd_t, j, nseg), vb, preferred_element_type=F32)
        s_ref[j] = sdec * s_prev[j] + upd


def _retention(proj, cos_t, sin_t, dmat, dec, g_head, rows, nseg, nchunks, nh, init=None):
    m = proj.shape[0]
    groups = m // (rows * nchunks)
    has_init = init is not None
    assert not has_init or nchunks == 1
    row = lambda g, h, c: g * nchunks + c
    qk_blk, v_blk = nh * DK_R, nh * DV_R
    in_specs = [
        pl.BlockSpec((rows, qk_blk), lambda g, h, c: (row(g, h, c), OFF_QR // qk_blk + h)),
        pl.BlockSpec((rows, qk_blk), lambda g, h, c: (row(g, h, c), OFF_KR // qk_blk + h)),
        pl.BlockSpec((rows, v_blk), lambda g, h, c: (row(g, h, c), OFF_VR // v_blk + h)),
        pl.BlockSpec((rows, v_blk), lambda g, h, c: (row(g, h, c), OFF_GR // v_blk + h)),
        pl.BlockSpec((rows, DK_R), lambda g, h, c: (c, 0)),
        pl.BlockSpec((rows, DK_R), lambda g, h, c: (c, 0)),
        pl.BlockSpec((nh, rows, rows), lambda g, h, c: (h, 0, 0)),
        pl.BlockSpec((nh, rows, LANES), lambda g, h, c: (h, 0, 0)),
        pl.BlockSpec((1, v_blk), lambda g, h, c: (0, h)),
    ]
    args = [proj, proj, proj, proj, cos_t, sin_t, dmat, dec, g_head]
    if has_init:
        in_specs.append(pl.BlockSpec((nseg, nh, DK_R, DV_R), lambda g, h, c: (g, h, 0, 0)))
        args.append(init)
    nb = groups * nseg
    return pl.pallas_call(
        functools.partial(_ret_kernel, rows=rows, nseg=nseg, has_init=has_init, nh=nh),
        grid=(groups, HR // nh, nchunks),
        in_specs=in_specs,
        out_specs=[
            pl.BlockSpec((rows, v_blk), lambda g, h, c: (row(g, h, c), h)),
            pl.BlockSpec((nseg, nh, DK_R, DV_R), lambda g, h, c: (g, h, 0, 0)),
        ],
        out_shape=[
            jax.ShapeDtypeStruct((m, HR * DV_R), BF16),
            jax.ShapeDtypeStruct((nb, HR, DK_R, DV_R), F32),
        ],
        compiler_params=_params(("parallel", "parallel", "arbitrary")),
        name="retention",
    )(*args)


def _retention_tables(seg_len, nseg):
    rows = seg_len * nseg
    idx = (np.arange(rows) % seg_len).astype(np.float64)
    seg = np.arange(rows) // seg_len
    lg = np.log(1.0 - 2.0 ** (-5.0 - np.arange(HR, dtype=np.float64)))
    diff = idx[:, None] - idx[None, :]
    ok = (diff >= 0) & (seg[:, None] == seg[None, :])
    dmat = np.where(ok[None], np.exp(np.maximum(diff, 0.0)[None] * lg[:, None, None]), 0.0)
    dec = np.zeros((HR, rows, LANES))
    dec[:, :, 0] = np.exp((idx[None, :] + 1.0) * lg[:, None])
    dec[:, :, 1] = np.exp((seg_len - 1.0 - idx[None, :]) * lg[:, None])
    dec[:, :, 2] = np.exp(seg_len * lg)[:, None]
    return jnp.asarray(dmat, F32), jnp.asarray(dec, F32)


def _rotary_tables(pos):
    freqs = ROPE_BASE ** (-jnp.arange(0, DK_R, 2, dtype=F32) / DK_R)
    ang = pos[:, None] * freqs[None, :]
    cos, sin = jnp.cos(ang), jnp.sin(ang)
    return jnp.concatenate([cos, cos], axis=-1), jnp.concatenate([-sin, sin], axis=-1)


def _rmsnorm_kernel(x_ref, g_ref, o_ref):
    o_ref[...] = _rms(x_ref[...], g_ref[...]).astype(o_ref.dtype)


def _rmsnorm(x2d, g, out_dtype, bm=512):
    m = x2d.shape[0]
    return pl.pallas_call(
        _rmsnorm_kernel,
        grid=(m // bm,),
        in_specs=[
            pl.BlockSpec((bm, D_MODEL), lambda i: (i, 0)),
            pl.BlockSpec((1, D_MODEL), lambda i: (0, 0)),
        ],
        out_specs=pl.BlockSpec((bm, D_MODEL), lambda i: (i, 0)),
        out_shape=jax.ShapeDtypeStruct((m, D_MODEL), out_dtype),
        compiler_params=_params(("parallel",)),
        name="rmsnorm",
    )(x2d, g)


def _layer(x2d, w, wb, tables, rows, nseg, nchunks, init, g_final, final_norm):
    cast = wb is None
    assert not cast or x2d.shape[0] == GEMM_BM
    cfg = dict(bm=GEMM_BM, bn=GEMM_BN_CAST if cast else GEMM_BN, cast_w=cast)
    new_wb = {}

    def gemm(key, a_list, **kw):
        res = _gemm(a_list, w[key] if cast else wb[key], name=key, **{**cfg, **kw})
        if cast:
            new_wb[key] = res.pop()
        return res[0] if len(res) == 1 else res

    h, gc, gr = _norm_gates(x2d, w["g_mix"], w["wg"], w["gate_bias"])
    proj = gemm("in_proj", [h], nt=True, shifted=cast)
    m_init = None if init is None else (init["m_tok"], init["C"], init["n"])
    nh_m = HM if init is None else 1
    hm, c_new, n_new, m_slab = _mlstm(proj, gc, gr, w["g_mh"], rows, nseg, nchunks, nh_m, m_init)
    s_init = None if init is None else init["S"]
    nh_r = HR if init is None else RET_HEADS_WITH_STATE
    hr, s_new = _retention(proj, *tables, w["g_rh"], rows, nseg, nchunks, nh_r, s_init)
    x1, x1g, x1_ss = gemm("out_proj", [hm, hr], epilogue="residual_prenorm", x=x2d,
                          norm_g=w["g_ffn"])
    act = gemm("ffn_up", [x1g], epilogue="rownorm_relu2", row_ss=x1_ss, out_dtype=BF16)
    down_cfg = dict(bn=GEMM_BN, bk=FFN_DOWN_BK_CAST) if cast else dict(bk=FFN_DOWN_BK)
    y = gemm("ffn_down", [act], epilogue="residual", x=x1, **down_cfg)
    if final_norm:
        y = _rmsnorm(y, g_final, F32)
    return y, c_new, n_new, m_slab, s_new, (new_wb if cast else wb)


def kernel(x_prompt, x_sample, state_mlstm_C, state_mlstm_n, state_mlstm_m, state_ret_S, w_in, b_igate, b_fgate, g_mlstm_head, g_ret_head, w_out, g_norm_mix, g_norm_ffn, w_up, w_down, g_final):
    depth = w_in.shape[0]
    batch, seq, _ = x_prompt.shape
    dec_batch, dec_seq, _ = x_sample.shape
    assert seq % PROMPT_CHUNK == 0 and SAMPLE_ROWS % dec_seq == 0
    assert (dec_batch * dec_seq) % SAMPLE_ROWS == 0
    seg_s = SAMPLE_ROWS // dec_seq
    nchunks_p = seq // PROMPT_CHUNK

    cos_p, sin_p = _rotary_tables(jnp.arange(seq, dtype=F32) + 0.0)
    pos_s = jnp.arange(dec_seq, dtype=F32) + float(PAST_LEN)
    cos_s, sin_s = _rotary_tables(jnp.tile(pos_s, seg_s))
    tables_p = (cos_p, sin_p) + _retention_tables(PROMPT_CHUNK, 1)
    tables_s = (cos_s, sin_s) + _retention_tables(dec_seq, seg_s)

    yp = x_prompt.reshape(batch * seq, D_MODEL)
    ys = x_sample.reshape(dec_batch * dec_seq, D_MODEL)
    g_fin = g_final.reshape(1, D_MODEL)
    outs = {k: [] for k in ("pC", "pn", "pm", "pS", "sC", "sn", "sm", "sS")}
    for l in range(depth):
        w_in_t = w_in[l].T
        w_gate_t = w_in_t[GATE_OFF:GATE_OFF + 2 * HM]
        w = {
            "in_proj": w_in_t,
            "wg": jnp.pad(w_gate_t, ((0, LANES - 2 * HM), (0, 0))).astype(BF16),
            "gate_bias": jnp.pad(jnp.concatenate([b_igate[l], b_fgate[l]]), (0, LANES - 2 * HM)).reshape(1, LANES),
            "g_mix": g_norm_mix[l].reshape(1, D_MODEL),
            "g_ffn": g_norm_ffn[l].reshape(1, D_MODEL),
            "g_mh": g_mlstm_head[l].reshape(1, HM * DV_M),
            "g_rh": g_ret_head[l].reshape(1, HR * DV_R),
            "out_proj": w_out[l],
            "ffn_up": w_up[l],
            "ffn_down": w_down[l],
        }
        last = l == depth - 1
        m_tok = jnp.pad(jnp.repeat(state_mlstm_m[l], dec_seq, axis=0), ((0, 0), (0, LANES - HM)))
        init = {"m_tok": m_tok, "C": state_mlstm_C[l],
                "n": state_mlstm_n[l].reshape(dec_batch, HM, 1, DK_M), "S": state_ret_S[l]}
        ys, c, n, m_slab, s, wb = _layer(ys, w, None, tables_s, SAMPLE_ROWS, seg_s, 1, init, g_fin, last)
        outs["sC"].append(c)
        outs["sn"].append(n.reshape(dec_batch, HM, DK_M))
        m_seg = m_slab[:, :, ::dec_seq, 0]
        outs["sm"].append(m_seg.transpose(0, 2, 1).reshape(dec_batch, HM))
        outs["sS"].append(s)

        yp, c, n, m_slab, s, _ = _layer(yp, w, wb, tables_p, PROMPT_CHUNK, 1, nchunks_p, None, g_fin, last)
        outs["pC"].append(c)
        outs["pn"].append(n.reshape(batch, HM, DK_M))
        outs["pm"].append(m_slab[:, :, 0, 0])
        outs["pS"].append(s)

    def stack(xs):
        return xs[0][None] if len(xs) == 1 else jnp.stack(xs)

    return (yp.reshape(batch, seq, D_MODEL), ys.reshape(dec_batch, dec_seq, D_MODEL),
            stack(outs["pC"]), stack(outs["pn"]), stack(outs["pm"]), stack(outs["pS"]),
            stack(outs["sC"]), stack(outs["sn"]), stack(outs["sm"]), stack(outs["sS"]))
```

```python
import functools
import math

import jax
import jax.numpy as jnp
import numpy as np
from jax import lax
from jax.experimental import pallas as pl
from jax.experimental.pallas import tpu as pltpu

F32 = jnp.float32
BF16 = jnp.bfloat16

D_MODEL = 4096
HM, DK_M, DV_M = 4, 256, 512
HR, DK_R, DV_R = 8, 128, 256
D_FF = 4 * D_MODEL
PAST_LEN = 16384
GATE_SOFTCAP = 15.0
ROPE_BASE = 10000.0
EPS = 1e-6

LANES = 128
SUBLANES = 8
BF16_ROWS = 16
VMEM_LIMIT = 60 * 1024 * 1024

N_MAIN = 2 * HM * DK_M + 2 * HM * DV_M + 2 * HR * DK_R + 2 * HR * DV_R
GATE_OFF = 2 * HM * DK_M + 2 * HM * DV_M
OFF_QM, OFF_KM, OFF_VM, OFF_OM = 0, HM * DK_M, 2 * HM * DK_M, 2 * HM * DK_M + HM * DV_M
OFF_QR = GATE_OFF
OFF_KR = OFF_QR + HR * DK_R
OFF_VR = OFF_KR + HR * DK_R
OFF_GR = OFF_VR + HR * DV_R

PROMPT_CHUNK = 256
PROMPT_CHUNKS_PER_STEP = 1
HEADS_PER_REGION = 2
SAMPLE_ROWS = 128
RET_HEADS_WITH_STATE = 4


_NN = (((1,), (0,)), ((), ()))
_NT = (((1,), (1,)), ((), ()))

GEMM_BM = 1024
GEMM_BN = 1024
GEMM_BN_CAST = 512
ACC_CHUNK = 256
FFN_DOWN_BK = 4096
FFN_DOWN_BK_CAST = 2048


def _params(sem):
    return pltpu.CompilerParams(dimension_semantics=sem, vmem_limit_bytes=VMEM_LIMIT)


def _sigmoid(x):
    return 1.0 / (1.0 + jnp.exp(-x))


def _rms(x, g):
    return x * lax.rsqrt(jnp.mean(x * x, axis=-1, keepdims=True) + EPS) * g


def _norm_gates_kernel(x_ref, g_ref, wg_ref, bias_ref, h_ref, gc_ref, gr_ref):
    hb = _rms(x_ref[...], g_ref[...]).astype(BF16)
    h_ref[...] = hb
    z = lax.dot_general(hb, wg_ref[...], _NT, preferred_element_type=F32) + bias_ref[...]
    zc = GATE_SOFTCAP * jnp.tanh(z / GATE_SOFTCAP)
    logsig = jnp.minimum(zc, 0.0) - jnp.log1p(jnp.exp(-jnp.abs(zc)))
    lane = lax.broadcasted_iota(jnp.int32, zc.shape, 1)
    out = jnp.where(lane >= HM, logsig, zc)
    gc_ref[...] = out
    gr_ref[...] = out.T[:SUBLANES, :]


def _norm_gates(x2d, g, wg, bias, bm=512):
    m = x2d.shape[0]
    return pl.pallas_call(
        _norm_gates_kernel,
        grid=(m // bm,),
        in_specs=[
            pl.BlockSpec((bm, D_MODEL), lambda i: (i, 0)),
            pl.BlockSpec((1, D_MODEL), lambda i: (0, 0)),
            pl.BlockSpec((LANES, D_MODEL), lambda i: (0, 0)),
            pl.BlockSpec((1, LANES), lambda i: (0, 0)),
        ],
        out_specs=[
            pl.BlockSpec((bm, D_MODEL), lambda i: (i, 0)),
            pl.BlockSpec((bm, LANES), lambda i: (i, 0)),
            pl.BlockSpec((SUBLANES, bm), lambda i: (0, i)),
        ],
        out_shape=[
            jax.ShapeDtypeStruct((m, D_MODEL), BF16),
            jax.ShapeDtypeStruct((m, LANES), F32),
            jax.ShapeDtypeStruct((SUBLANES, m), F32),
        ],
        compiler_params=_params(("parallel",)),
        name="norm_gates",
    )(x2d, g, wg, bias)


def _gemm_kernel(*refs, n_a, nt, cast_w, shifted_from, epilogue):
    a_refs, w_ref = refs[:n_a], refs[n_a]
    rest = list(refs[n_a + 1:])
    w_hi_ref = rest.pop(0) if shifted_from is not None else None
    x_ref = rest.pop(0) if epilogue in ("residual", "residual_prenorm") else None
    g_ref = rest.pop(0) if epilogue == "residual_prenorm" else None
    ss_in_ref = rest.pop(0) if epilogue == "rownorm_relu2" else None
    o_ref = rest.pop(0)
    if epilogue == "residual_prenorm":
        xg_ref, ss_ref = rest.pop(0), rest.pop(0)

    wb_ref = rest.pop(0) if cast_w else None

    def product(cols, shift=0):
        w_src = w_ref
        if cast_w:
            w_src = wb_ref
            if nt:
                lo, hi = cols.start + shift, cols.stop + shift
                if hi <= w_ref.shape[0]:
                    blk = w_ref[lo:hi, :]
                else:
                    blk = jnp.concatenate([w_ref[lo:, :], w_hi_ref[...]], axis=0)
                wb_ref[cols, :] = blk.astype(BF16)
            else:
                wb_ref[:, cols] = w_ref[:, cols].astype(BF16)
        acc, off = None, 0
        for a_ref in a_refs:
            ka = a_ref.shape[1]
            wk = w_src[cols, off:off + ka] if nt else w_src[off:off + ka, cols]
            part = lax.dot_general(a_ref[...], wk, _NT if nt else _NN, preferred_element_type=F32)
            acc = part if acc is None else acc + part
            off += ka
        return acc

    if epilogue == "residual":
        @pl.when(pl.program_id(2) == 0)
        def _first():
            o_ref[...] = x_ref[...]

        bn = o_ref.shape[1]
        for c in range(0, bn, ACC_CHUNK):
            cols = slice(c, min(c + ACC_CHUNK, bn))
            o_ref[:, cols] += product(cols)
    elif epilogue == "residual_prenorm":
        bn = o_ref.shape[1]
        ss = jnp.zeros(ss_ref.shape, F32)
        for c in range(0, bn, ACC_CHUNK):
            cols = slice(c, min(c + ACC_CHUNK, bn))
            val = x_ref[:, cols] + product(cols)
            o_ref[:, cols] = val
            xg_ref[:, cols] = (val * g_ref[:, cols]).astype(BF16)
            sq = val * val
            for t in range(0, sq.shape[1], LANES):
                ss = ss + sq[:, t:t + LANES]
        ss_ref[...] = ss
    elif epilogue == "rownorm_relu2":
        mean_sq = jnp.sum(ss_in_ref[...], axis=1, keepdims=True) * (1.0 / D_MODEL)
        scale = lax.rsqrt(mean_sq + EPS)
        bn = o_ref.shape[1]
        for c in range(0, bn, ACC_CHUNK):
            cols = slice(c, min(c + ACC_CHUNK, bn))
            u = product(cols) * scale
            o_ref[:, cols] = jnp.square(jnp.maximum(u, 0.0)).astype(o_ref.dtype)
    else:
        bn = o_ref.shape[1]
        step = ACC_CHUNK if cast_w else bn

        def run(shift):
            for c in range(0, bn, step):
                cols = slice(c, min(c + step, bn))
                o_ref[:, cols] = product(cols, shift)

        if shifted_from is None:
            run(0)
        else:
            pl.when(pl.program_id(0) < shifted_from)(functools.partial(run, 0))
            pl.when(pl.program_id(0) >= shifted_from)(functools.partial(run, SUBLANES))


def _gemm(a_list, w, *, name, bm, bn, bk=None, nt=False, cast_w=False, shifted=False,
          epilogue=None, x=None, norm_g=None, row_ss=None, out_dtype=F32):
    m = a_list[0].shape[0]
    k = sum(a.shape[1] for a in a_list)
    n = N_MAIN if shifted else (w.shape[0] if nt else w.shape[1])
    bk = k if bk is None else bk
    k_steps = k // bk
    assert len(a_list) == 1 or k_steps == 1
    assert epilogue == "residual" or k_steps == 1
    assert not cast_w or m == bm
    assert epilogue != "rownorm_relu2" or k == D_MODEL
    in_specs = [pl.BlockSpec((bm, a.shape[1] if k_steps == 1 else bk), lambda j, i, kk: (i, kk))
                for a in a_list]
    w_blk = (bn, bk) if nt else (bk, bn)
    w_map = (lambda j, i, kk: (j, kk)) if nt else (lambda j, i, kk: (kk, j))
    in_specs.append(pl.BlockSpec(w_blk, w_map))
    args = [*a_list, w]
    shifted_from = None
    if shifted:
        assert nt and cast_w and k_steps == 1 and GATE_OFF % bn == 0
        shifted_from = GATE_OFF // bn
        in_specs.append(pl.BlockSpec((SUBLANES, bk), lambda j, i, kk: ((j + 1) * (bn // SUBLANES), 0)))
        args.append(w)
    if epilogue in ("residual", "residual_prenorm"):
        in_specs.append(pl.BlockSpec((bm, bn), lambda j, i, kk: (i, j)))
        args.append(x)
    if epilogue == "residual_prenorm":
        in_specs.append(pl.BlockSpec((1, bn), lambda j, i, kk: (0, j)))
        args.append(norm_g)
    if epilogue == "rownorm_relu2":
        in_specs.append(pl.BlockSpec((bm, row_ss.shape[1]), lambda j, i, kk: (i, 0)))
        args.append(row_ss)
    out_specs = [pl.BlockSpec((bm, bn), lambda j, i, kk: (i, j))]
    out_shape = [jax.ShapeDtypeStruct((m, n), out_dtype)]
    if epilogue == "residual_prenorm":
        out_specs += [pl.BlockSpec((bm, bn), lambda j, i, kk: (i, j)),
                      pl.BlockSpec((bm, LANES), lambda j, i, kk: (i, j))]
        out_shape += [jax.ShapeDtypeStruct((m, n), BF16),
                      jax.ShapeDtypeStruct((m, (n // bn) * LANES), F32)]
    if cast_w:
        out_specs.append(pl.BlockSpec(w_blk, w_map))
        out_shape.append(jax.ShapeDtypeStruct((n, k) if nt else (k, n), BF16))
    res = pl.pallas_call(
        functools.partial(_gemm_kernel, n_a=len(a_list), nt=nt, cast_w=cast_w,
                          shifted_from=shifted_from, epilogue=epilogue),
        grid=(n // bn, m // bm, k_steps),
        in_specs=in_specs,
        out_specs=out_specs,
        out_shape=out_shape,
        compiler_params=_params(("parallel", "parallel", "arbitrary")),
        name=name,
    )(*args)
    return list(res)


def _segment_masks(rows, nseg):
    ri = lax.broadcasted_iota(jnp.int32, (rows, rows), 0)
    ci = lax.broadcasted_iota(jnp.int32, (rows, rows), 1)
    if nseg == 1:
        same = None
        causal = ci <= ri
        upper = ri <= ci
    else:
        shift = int(math.log2(rows // nseg))
        same = lax.shift_right_logical(ri, shift) == lax.shift_right_logical(ci, shift)
        causal = (ci <= ri) & same
        upper = (ri <= ci) & same
    return same, causal, upper


def _state_dot(qb, state_bf16, nseg):
    if nseg == 1:
        return jnp.dot(qb, state_bf16[0], preferred_element_type=F32)
    rps = qb.shape[0] // nseg
    assert 2 * rps == BF16_ROWS
    outs = []
    for p in range(nseg // 2):
        qp = qb[p * BF16_ROWS:(p + 1) * BF16_ROWS]
        outs.append(jnp.dot(qp, state_bf16[2 * p], preferred_element_type=F32)[:rps])
        outs.append(jnp.dot(qp, state_bf16[2 * p + 1], preferred_element_type=F32)[rps:])
    return jnp.concatenate(outs, axis=0)


def _segment_lhs(xt, j, nseg):
    if nseg == 1:
        return xt.astype(BF16)
    rps = xt.shape[1] // nseg
    lane = lax.broadcasted_iota(jnp.int32, xt.shape, 1)
    keep = (lane >= j * rps) & (lane < (j + 1) * rps)
    return jnp.where(keep, xt, 0.0).astype(BF16)


def _mlstm_kernel(*refs, rows, nseg, has_init, nh, nsub):
    if has_init:
        (q_ref, k_ref, v_ref, om_ref, gc_ref, gr_ref, gh_ref, m0_ref, c0_ref, n0_ref,
         h_ref, c_ref, n_ref, m_ref) = refs
    else:
        (q_ref, k_ref, v_ref, om_ref, gc_ref, gr_ref, gh_ref,
         h_ref, c_ref, n_ref, m_ref) = refs
    for sc in range(nsub):
        rs = slice(sc * rows, (sc + 1) * rows)
        for hh in range(nh):
            qk = slice(hh * DK_M, (hh + 1) * DK_M)
            vv = slice(hh * DV_M, (hh + 1) * DV_M)
            head_refs = [q_ref.at[rs, qk], k_ref.at[rs, qk], v_ref.at[rs, vv], om_ref.at[rs, vv],
                         gc_ref.at[rs, :], gr_ref.at[:, rs], gh_ref.at[:, vv]]
            if has_init:
                head_refs += [m0_ref, c0_ref.at[:, hh], n0_ref.at[:, hh]]
            head_refs += [h_ref.at[rs, vv], c_ref.at[:, hh], n_ref.at[:, hh], m_ref.at[hh]]
            zero = []
            if not has_init and sc == 0 and hh % HEADS_PER_REGION == 0:
                grp = slice(hh, min(hh + HEADS_PER_REGION, nh))
                zero = [c_ref.at[:, grp], n_ref.at[:, grp], m_ref.at[grp]]
            _mlstm_head(pl.program_id(1) * nh + hh, head_refs, rows, nseg, has_init, zero)


def _zero_at_sequence_start(state_refs):
    if state_refs:
        @pl.when(pl.program_id(2) == 0)
        def _zero_state():
            for ref in state_refs:
                ref[...] = jnp.zeros(ref.shape, F32)


def _mlstm_head(head, refs, rows, nseg, has_init, zero_refs):
    if has_init:
        (q_ref, k_ref, v_ref, om_ref, gc_ref, gr_ref, gh_ref, m0_ref, c0_ref, n0_ref,
         h_ref, c_ref, n_ref, m_ref) = refs
    else:
        (q_ref, k_ref, v_ref, om_ref, gc_ref, gr_ref, gh_ref,
         h_ref, c_ref, n_ref, m_ref) = refs
    rps = rows // nseg
    lane = lax.broadcasted_iota(jnp.int32, (rows, LANES), 1)
    sub = lax.broadcasted_iota(jnp.int32, (SUBLANES, rows), 0)

    def pick_col(x_ref, idx):
        return jnp.sum(jnp.where(lane == idx, x_ref[...], 0.0), axis=1, keepdims=True)

    def pick_row(x_ref, idx):
        return jnp.sum(jnp.where(sub == idx, x_ref[...], 0.0), axis=0, keepdims=True)

    if has_init:
        c_prev, n_prev = c0_ref, n0_ref
        m_col = pick_col(m0_ref, head)
    else:
        _zero_at_sequence_start(zero_refs)
        c_prev, n_prev = c_ref, n_ref
        m_col = m_ref[:, 0:1]

    i_col, f_col = pick_col(gc_ref, head), pick_col(gc_ref, head + HM)
    i_row, f_row = pick_row(gr_ref, head), pick_row(gr_ref, head + HM)

    same, causal, upper = _segment_masks(rows, nseg)
    b_col = jnp.sum(jnp.where(causal, f_row, 0.0), axis=1, keepdims=True)
    b_row = jnp.sum(jnp.where(upper, f_col, 0.0), axis=0, keepdims=True)
    if same is None:
        bl_col = jnp.sum(f_row, axis=1, keepdims=True) + jnp.zeros_like(b_col)
        bl_row = jnp.sum(f_col, axis=0, keepdims=True) + jnp.zeros_like(b_row)
    else:
        bl_col = jnp.sum(jnp.where(same, f_row, 0.0), axis=1, keepdims=True)
        bl_row = jnp.sum(jnp.where(same, f_col, 0.0), axis=0, keepdims=True)

    dlog = jnp.where(causal, b_col - b_row + i_row, -jnp.inf)
    dmax = jnp.max(dlog, axis=1, keepdims=True)
    inter = b_col + m_col
    mt = jnp.maximum(inter, dmax)
    dw = jnp.exp(dlog - mt)
    iw = jnp.exp(inter - mt)

    q = q_ref[...] * (DK_M ** -0.5)
    k = k_ref[...]
    qb, kb, vb = q.astype(BF16), k.astype(BF16), v_ref[...].astype(BF16)
    s = lax.dot_general(qb, kb, (((1,), (1,)), ((), ())), preferred_element_type=F32) * dw
    intra = jnp.dot(s.astype(BF16), vb, preferred_element_type=F32)
    q_c = _state_dot(qb, [c_prev[j].astype(BF16) for j in range(nseg)], nseg)
    n_tok = jnp.broadcast_to(n_prev[...], (nseg, rps, DK_M)).reshape(rows, DK_M)
    q_n = jnp.sum(q * n_tok, axis=1, keepdims=True)
    num = iw * q_c + intra
    den = iw * q_n + jnp.sum(s, axis=1, keepdims=True)
    hh = num * (1.0 / jnp.maximum(jnp.abs(den), jnp.exp(-mt)))
    h_ref[...] = (_rms(hh, gh_ref[...]) * _sigmoid(om_ref[...])).astype(BF16)

    wlog_col = bl_col - b_col + i_col
    wlog_row = bl_row - b_row + i_row
    if same is None:
        wmax = jnp.max(wlog_row, axis=1, keepdims=True) + jnp.zeros_like(b_col)
    else:
        wmax = jnp.max(jnp.where(same, wlog_row, -jnp.inf), axis=1, keepdims=True)
    m_new = jnp.maximum(bl_col + m_col, wmax)
    decay = jnp.exp(bl_col + m_col - m_new)
    kw = k * jnp.exp(wlog_col - m_new)
    kw_t = kw.T
    for j in range(nseg):
        d_j = decay[j * rps:j * rps + 1, :]
        upd = jnp.dot(_segment_lhs(kw_t, j, nseg), vb, preferred_element_type=F32)
        c_ref[j] = d_j * c_prev[j] + upd
        n_ref[j] = d_j * n_prev[j] + jnp.sum(kw[j * rps:(j + 1) * rps], axis=0, keepdims=True)
    m_ref[...] = jnp.broadcast_to(m_new, (rows, LANES))


def _mlstm_parts(proj, gc, gr, g_head, rows, nseg, nchunks, nh, nsub, init=None):
    m = proj.shape[0]
    groups = m // (rows * nchunks)
    has_init = init is not None
    assert not has_init or (nchunks == 1 and nsub == 1)
    steps = nchunks // nsub
    brows = rows * nsub
    row = lambda g, h, c: g * steps + c
    qk_blk, v_blk = nh * DK_M, nh * DV_M
    in_specs = [
        pl.BlockSpec((brows, qk_blk), lambda g, h, c: (row(g, h, c), OFF_QM // qk_blk + h)),
        pl.BlockSpec((brows, qk_blk), lambda g, h, c: (row(g, h, c), OFF_KM // qk_blk + h)),
        pl.BlockSpec((brows, v_blk), lambda g, h, c: (row(g, h, c), OFF_VM // v_blk + h)),
        pl.BlockSpec((brows, v_blk), lambda g, h, c: (row(g, h, c), OFF_OM // v_blk + h)),
        pl.BlockSpec((brows, LANES), lambda g, h, c: (row(g, h, c), 0)),
        pl.BlockSpec((SUBLANES, brows), lambda g, h, c: (0, row(g, h, c))),
        pl.BlockSpec((1, v_blk), lambda g, h, c: (0, h)),
    ]
    args = [proj, proj, proj, proj, gc, gr, g_head]
    if has_init:
        m0_tok, c0, n0 = init
        in_specs += [
            pl.BlockSpec((rows, LANES), lambda g, h, c: (g, 0)),
            pl.BlockSpec((nseg, nh, DK_M, DV_M), lambda g, h, c: (g, h, 0, 0)),
            pl.BlockSpec((nseg, nh, 1, DK_M), lambda g, h, c: (g, h, 0, 0)),
        ]
        args += [m0_tok, c0, n0]
    nb = groups * nseg
    return dict(
        kernel=functools.partial(_mlstm_kernel, rows=rows, nseg=nseg, has_init=has_init, nh=nh,
                                 nsub=nsub),
        grid=(groups, HM // nh, steps),
        in_specs=in_specs,
        args=args,
        out_specs=[
            pl.BlockSpec((brows, v_blk), lambda g, h, c: (row(g, h, c), h)),
            pl.BlockSpec((nseg, nh, DK_M, DV_M), lambda g, h, c: (g, h, 0, 0)),
            pl.BlockSpec((nseg, nh, 1, DK_M), lambda g, h, c: (g, h, 0, 0)),
            pl.BlockSpec((None, nh, rows, LANES), lambda g, h, c: (g, h, 0, 0)),
        ],
        out_shape=[
            jax.ShapeDtypeStruct((m, HM * DV_M), BF16),
            jax.ShapeDtypeStruct((nb, HM, DK_M, DV_M), F32),
            jax.ShapeDtypeStruct((nb, HM, 1, DK_M), F32),
            jax.ShapeDtypeStruct((groups, HM, rows, LANES), F32),
        ],
    )


def _run_mixers(parts, name):
    grid = parts[0]["grid"]
    assert all(p["grid"] == grid for p in parts)
    n_in = [len(p["in_specs"]) for p in parts]
    n_out = [len(p["out_specs"]) for p in parts]

    def body(*refs):
        ins, outs = refs[:sum(n_in)], refs[sum(n_in):]
        i = o = 0
        for p, ni, no in zip(parts, n_in, n_out):
            p["kernel"](*ins[i:i + ni], *outs[o:o + no])
            i, o = i + ni, o + no

    res = pl.pallas_call(
        body,
        grid=grid,
        in_specs=[s for p in parts for s in p["in_specs"]],
        out_specs=[s for p in parts for s in p["out_specs"]],
        out_shape=[s for p in parts for s in p["out_shape"]],
        compiler_params=_params(("parallel", "parallel", "arbitrary")),
        name=name,
    )(*[a for p in parts for a in p["args"]])
    outs, o = [], 0
    for no in n_out:
        outs.append(res[o:o + no])
        o += no
    return outs


def _ret_kernel(*refs, rows, nseg, has_init, nh, nsub):
    if has_init:
        (q_ref, k_ref, v_ref, g_ref, cos_ref, sin_ref, dmat_ref, dec_ref, gh_ref, s0_ref,
         o_ref, s_ref) = refs
    else:
        (q_ref, k_ref, v_ref, g_ref, cos_ref, sin_ref, dmat_ref, dec_ref, gh_ref,
         o_ref, s_ref) = refs
    for sc in range(nsub):
        rs = slice(sc * rows, (sc + 1) * rows)
        for hh in range(nh):
            qk = slice(hh * DK_R, (hh + 1) * DK_R)
            vv = slice(hh * DV_R, (hh + 1) * DV_R)
            head_refs = [q_ref.at[rs, qk], k_ref.at[rs, qk], v_ref.at[rs, vv], g_ref.at[rs, vv],
                         cos_ref.at[rs, :], sin_ref.at[rs, :], dmat_ref.at[hh], dec_ref.at[hh],
                         gh_ref.at[:, vv]]
            if has_init:
                head_refs.append(s0_ref.at[:, hh])
            head_refs += [o_ref.at[rs, vv], s_ref.at[:, hh]]
            zero = []
            if not has_init and sc == 0 and hh % HEADS_PER_REGION == 0:
                zero = [s_ref.at[:, hh:min(hh + HEADS_PER_REGION, nh)]]
            _ret_head(head_refs, rows, nseg, has_init, zero)


def _ret_head(refs, rows, nseg, has_init, zero_refs):
    if has_init:
        (q_ref, k_ref, v_ref, g_ref, cos_ref, sin_ref, dmat_ref, dec_ref, gh_ref, s0_ref,
         o_ref, s_ref) = refs
        s_prev = s0_ref
    else:
        (q_ref, k_ref, v_ref, g_ref, cos_ref, sin_ref, dmat_ref, dec_ref, gh_ref,
         o_ref, s_ref) = refs
        s_prev = s_ref
        _zero_at_sequence_start(zero_refs)

    cosf, sins = cos_ref[...], sin_ref[...]
    half = DK_R // 2

    def rot(x):
        return x * cosf + pltpu.roll(x, half, 1) * sins

    qr = rot(q_ref[...])
    kr = rot(k_ref[...]) * (DK_R ** -0.5)
    dec = dec_ref[...]
    inter, kdec, sdec = dec[:, 0:1], dec[:, 1:2], dec[0:1, 2:3]
    qb, kb, vb = qr.astype(BF16), kr.astype(BF16), v_ref[...].astype(BF16)
    s = lax.dot_general(qb, kb, (((1,), (1,)), ((), ())), preferred_element_type=F32) * dmat_ref[...]
    q_s = _state_dot(qb, [s_prev[j].astype(BF16) for j in range(nseg)], nseg)
    o = jnp.dot(s.astype(BF16), vb, preferred_element_type=F32) + inter * q_s
    g = g_ref[...]
    o_ref[...] = (_rms(o, gh_ref[...]) * (g * _sigmoid(g))).astype(BF16)

    kd_t = (kr * kdec).T
    for j in range(nseg):
        upd = jnp.dot(_segment_lhs(kd_t, j, nseg), vb, preferred_element_type=F32)
        s_ref[j] = sdec * s_prev[j] + upd


def _ret_parts(proj, cos_t, sin_t, dmat, dec, g_head, rows, nseg, nchunks, nh, nsub, init=None):
    m = proj.shape[0]
    groups = m // (rows * nchunks)
    has_init = init is not None
    assert not has_init or (nchunks == 1 and nsub == 1)
    steps = nchunks // nsub
    brows = rows * nsub
    row = lambda g, h, c: g * steps + c
    qk_blk, v_blk = nh * DK_R, nh * DV_R
    in_specs = [
        pl.BlockSpec((brows, qk_blk), lambda g, h, c: (row(g, h, c), OFF_QR // qk_blk + h)),
        pl.BlockSpec((brows, qk_blk), lambda g, h, c: (row(g, h, c), OFF_KR // qk_blk + h)),
        pl.BlockSpec((brows, v_blk), lambda g, h, c: (row(g, h, c), OFF_VR // v_blk + h)),
        pl.BlockSpec((brows, v_blk), lambda g, h, c: (row(g, h, c), OFF_GR // v_blk + h)),
        pl.BlockSpec((brows, DK_R), lambda g, h, c: (c, 0)),
        pl.BlockSpec((brows, DK_R), lambda g, h, c: (c, 0)),
        pl.BlockSpec((nh, rows, rows), lambda g, h, c: (h, 0, 0)),
        pl.BlockSpec((nh, rows, LANES), lambda g, h, c: (h, 0, 0)),
        pl.BlockSpec((1, v_blk), lambda g, h, c: (0, h)),
    ]
    args = [proj, proj, proj, proj, cos_t, sin_t, dmat, dec, g_head]
    if has_init:
        in_specs.append(pl.BlockSpec((nseg, nh, DK_R, DV_R), lambda g, h, c: (g, h, 0, 0)))
        args.append(init)
    nb = groups * nseg
    return dict(
        kernel=functools.partial(_ret_kernel, rows=rows, nseg=nseg, has_init=has_init, nh=nh,
                                 nsub=nsub),
        grid=(groups, HR // nh, steps),
        in_specs=in_specs,
        args=args,
        out_specs=[
            pl.BlockSpec((brows, v_blk), lambda g, h, c: (row(g, h, c), h)),
            pl.BlockSpec((nseg, nh, DK_R, DV_R), lambda g, h, c: (g, h, 0, 0)),
        ],
        out_shape=[
            jax.ShapeDtypeStruct((m, HR * DV_R), BF16),
            jax.ShapeDtypeStruct((nb, HR, DK_R, DV_R), F32),
        ],
    )


def _retention_tables(seg_len, nseg):
    rows = seg_len * nseg
    idx = (np.arange(rows) % seg_len).astype(np.float64)
    seg = np.arange(rows) // seg_len
    lg = np.log(1.0 - 2.0 ** (-5.0 - np.arange(HR, dtype=np.float64)))
    diff = idx[:, None] - idx[None, :]
    ok = (diff >= 0) & (seg[:, None] == seg[None, :])
    dmat = np.where(ok[None], np.exp(np.maximum(diff, 0.0)[None] * lg[:, None, None]), 0.0)
    dec = np.zeros((HR, rows, LANES))
    dec[:, :, 0] = np.exp((idx[None, :] + 1.0) * lg[:, None])
    dec[:, :, 1] = np.exp((seg_len - 1.0 - idx[None, :]) * lg[:, None])
    dec[:, :, 2] = np.exp(seg_len * lg)[:, None]
    return jnp.asarray(dmat, F32), jnp.asarray(dec, F32)


def _rotary_tables(pos):
    freqs = ROPE_BASE ** (-jnp.arange(0, DK_R, 2, dtype=F32) / DK_R)
    ang = pos[:, None] * freqs[None, :]
    cos, sin = jnp.cos(ang), jnp.sin(ang)
    return jnp.concatenate([cos, cos], axis=-1), jnp.concatenate([-sin, sin], axis=-1)


def _rmsnorm_kernel(x_ref, g_ref, o_ref):
    o_ref[...] = _rms(x_ref[...], g_ref[...]).astype(o_ref.dtype)


def _rmsnorm(x2d, g, out_dtype, bm=512):
    m = x2d.shape[0]
    return pl.pallas_call(
        _rmsnorm_kernel,
        grid=(m // bm,),
        in_specs=[
            pl.BlockSpec((bm, D_MODEL), lambda i: (i, 0)),
            pl.BlockSpec((1, D_MODEL), lambda i: (0, 0)),
        ],
        out_specs=pl.BlockSpec((bm, D_MODEL), lambda i: (i, 0)),
        out_shape=jax.ShapeDtypeStruct((m, D_MODEL), out_dtype),
        compiler_params=_params(("parallel",)),
        name="rmsnorm",
    )(x2d, g)


def _layer(x2d, w, wb, tables, rows, nseg, nchunks, init, g_final, final_norm):
    cast = wb is None
    assert not cast or x2d.shape[0] == GEMM_BM
    cfg = dict(bm=GEMM_BM, bn=GEMM_BN_CAST if cast else GEMM_BN, cast_w=cast)
    new_wb = {}

    def gemm(key, a_list, **kw):
        res = _gemm(a_list, w[key] if cast else wb[key], name=key, **{**cfg, **kw})
        if cast:
            new_wb[key] = res.pop()
        return res[0] if len(res) == 1 else res

    h, gc, gr = _norm_gates(x2d, w["g_mix"], w["wg"], w["gate_bias"])
    proj = gemm("in_proj", [h], nt=True, shifted=cast)
    if init is None:
        nsub = math.gcd(nchunks, PROMPT_CHUNKS_PER_STEP)
        (hm, c_new, n_new, m_slab), (hr, s_new) = _run_mixers(
            [_mlstm_parts(proj, gc, gr, w["g_mh"], rows, nseg, nchunks, HM, nsub),
             _ret_parts(proj, *tables, w["g_rh"], rows, nseg, nchunks, HR, nsub)], "mixers")
    else:
        m_init = (init["m_tok"], init["C"], init["n"])
        (hm, c_new, n_new, m_slab), = _run_mixers(
            [_mlstm_parts(proj, gc, gr, w["g_mh"], rows, nseg, nchunks, 1, 1, m_init)], "mlstm")
        (hr, s_new), = _run_mixers(
            [_ret_parts(proj, *tables, w["g_rh"], rows, nseg, nchunks, RET_HEADS_WITH_STATE, 1,
                        init["S"])], "retention")
    x1, x1g, x1_ss = gemm("out_proj", [hm, hr], epilogue="residual_prenorm", x=x2d,
                          norm_g=w["g_ffn"])
    act = gemm("ffn_up", [x1g], epilogue="rownorm_relu2", row_ss=x1_ss, out_dtype=BF16)
    down_cfg = dict(bn=GEMM_BN, bk=FFN_DOWN_BK_CAST) if cast else dict(bk=FFN_DOWN_BK)
    y = gemm("ffn_down", [act], epilogue="residual", x=x1, **down_cfg)
    if final_norm:
        y = _rmsnorm(y, g_final, F32)
    return y, c_new, n_new, m_slab, s_new, (new_wb if cast else wb)


def kernel(x_prompt, x_sample, state_mlstm_C, state_mlstm_n, state_mlstm_m, state_ret_S, w_in, b_igate, b_fgate, g_mlstm_head, g_ret_head, w_out, g_norm_mix, g_norm_ffn, w_up, w_down, g_final):
    depth = w_in.shape[0]
    batch, seq, _ = x_prompt.shape
    dec_batch, dec_seq, _ = x_sample.shape
    assert seq % PROMPT_CHUNK == 0 and SAMPLE_ROWS % dec_seq == 0
    assert (dec_batch * dec_seq) % SAMPLE_ROWS == 0
    seg_s = SAMPLE_ROWS // dec_seq
    nchunks_p = seq // PROMPT_CHUNK

    cos_p, sin_p = _rotary_tables(jnp.arange(seq, dtype=F32) + 0.0)
    pos_s = jnp.arange(dec_seq, dtype=F32) + float(PAST_LEN)
    cos_s, sin_s = _rotary_tables(jnp.tile(pos_s, seg_s))
    tables_p = (cos_p, sin_p) + _retention_tables(PROMPT_CHUNK, 1)
    tables_s = (cos_s, sin_s) + _retention_tables(dec_seq, seg_s)

    yp = x_prompt.reshape(batch * seq, D_MODEL)
    ys = x_sample.reshape(dec_batch * dec_seq, D_MODEL)
    g_fin = g_final.reshape(1, D_MODEL)
    outs = {k: [] for k in ("pC", "pn", "pm", "pS", "sC", "sn", "sm", "sS")}
    for l in range(depth):
        w_in_t = w_in[l].T
        w_gate_t = w_in_t[GATE_OFF:GATE_OFF + 2 * HM]
        w = {
            "in_proj": w_in_t,
            "wg": jnp.pad(w_gate_t, ((0, LANES - 2 * HM), (0, 0))).astype(BF16),
            "gate_bias": jnp.pad(jnp.concatenate([b_igate[l], b_fgate[l]]), (0, LANES - 2 * HM)).reshape(1, LANES),
            "g_mix": g_norm_mix[l].reshape(1, D_MODEL),
            "g_ffn": g_norm_ffn[l].reshape(1, D_MODEL),
            "g_mh": g_mlstm_head[l].reshape(1, HM * DV_M),
            "g_rh": g_ret_head[l].reshape(1, HR * DV_R),
            "out_proj": w_out[l],
            "ffn_up": w_up[l],
            "ffn_down": w_down[l],
        }
        last = l == depth - 1
        m_tok = jnp.pad(jnp.repeat(state_mlstm_m[l], dec_seq, axis=0), ((0, 0), (0, LANES - HM)))
        init = {"m_tok": m_tok, "C": state_mlstm_C[l],
                "n": state_mlstm_n[l].reshape(dec_batch, HM, 1, DK_M), "S": state_ret_S[l]}
        ys, c, n, m_slab, s, wb = _layer(ys, w, None, tables_s, SAMPLE_ROWS, seg_s, 1, init, g_fin, last)
        outs["sC"].append(c)
        outs["sn"].append(n.reshape(dec_batch, HM, DK_M))
        m_seg = m_slab[:, :, ::dec_seq, 0]
        outs["sm"].append(m_seg.transpose(0, 2, 1).reshape(dec_batch, HM))
        outs["sS"].append(s)

        yp, c, n, m_slab, s, _ = _layer(yp, w, wb, tables_p, PROMPT_CHUNK, 1, nchunks_p, None, g_fin, last)
        outs["pC"].append(c)
        outs["pn"].append(n.reshape(batch, HM, DK_M))
        outs["pm"].append(m_slab[:, :, 0, 0])
        outs["pS"].append(s)

    def stack(xs):
        return xs[0][None] if len(xs) == 1 else jnp.stack(xs)

    return (yp.reshape(batch, seq, D_MODEL), ys.reshape(dec_batch, dec_seq, D_MODEL),
            stack(outs["pC"]), stack(outs["pn"]), stack(outs["pm"]), stack(outs["pS"]),
            stack(outs["sC"]), stack(outs["sn"]), stack(outs["sm"]), stack(outs["sS"]))
```

```python
import functools
import math

import jax
import jax.numpy as jnp
import numpy as np
from jax import lax
from jax.experimental import pallas as pl
from jax.experimental.pallas import tpu as pltpu

F32 = jnp.float32
BF16 = jnp.bfloat16

D_MODEL = 4096
HM, DK_M, DV_M = 4, 256, 512
HR, DK_R, DV_R = 8, 128, 256
PAST_LEN = 16384
GATE_SOFTCAP = 15.0
ROPE_BASE = 10000.0
EPS = 1e-6

LANES = 128
SUBLANES = 8
BF16_ROWS = 16
VMEM_LIMIT = 60 * 1024 * 1024

N_MAIN = 2 * HM * DK_M + 2 * HM * DV_M + 2 * HR * DK_R + 2 * HR * DV_R
GATE_OFF = 2 * HM * DK_M + 2 * HM * DV_M
OFF_QM, OFF_KM, OFF_VM, OFF_OM = 0, HM * DK_M, 2 * HM * DK_M, 2 * HM * DK_M + HM * DV_M
OFF_QR = GATE_OFF
OFF_KR = OFF_QR + HR * DK_R
OFF_VR = OFF_KR + HR * DK_R
OFF_GR = OFF_VR + HR * DV_R

PROMPT_CHUNK = 256
PROMPT_CHUNKS_PER_STEP = 1
HEADS_PER_REGION = 2
SAMPLE_ROWS = 128
RET_HEADS_WITH_STATE = 4


_NN = (((1,), (0,)), ((), ()))
_NT = (((1,), (1,)), ((), ()))

GEMM_BM = 1024
GEMM_BN = 1024
GEMM_BN_CAST = 512
ACC_CHUNK = 256
FFN_DOWN_BK = 4096
FFN_DOWN_BK_CAST = 2048


def _params(sem):
    return pltpu.CompilerParams(dimension_semantics=sem, vmem_limit_bytes=VMEM_LIMIT)


def _sigmoid(x):
    return 1.0 / (1.0 + jnp.exp(-x))


def _rms(x, g):
    return x * lax.rsqrt(jnp.mean(x * x, axis=-1, keepdims=True) + EPS) * g


def _norm_gates_kernel(x_ref, g_ref, wg_ref, bias_ref, h_ref, gc_ref, gr_ref):
    hb = _rms(x_ref[...], g_ref[...]).astype(BF16)
    h_ref[...] = hb
    z = lax.dot_general(hb, wg_ref[...], _NT, preferred_element_type=F32) + bias_ref[...]
    zc = GATE_SOFTCAP * jnp.tanh(z / GATE_SOFTCAP)
    logsig = jnp.minimum(zc, 0.0) - jnp.log1p(jnp.exp(-jnp.abs(zc)))
    lane = lax.broadcasted_iota(jnp.int32, zc.shape, 1)
    out = jnp.where(lane >= HM, logsig, zc)
    gc_ref[...] = out
    gr_ref[...] = out.T[:SUBLANES, :]


def _norm_gates(x2d, g, wg, bias, bm=512):
    assert 2 * HM == SUBLANES
    m = x2d.shape[0]
    return pl.pallas_call(
        _norm_gates_kernel,
        grid=(m // bm,),
        in_specs=[
            pl.BlockSpec((bm, D_MODEL), lambda i: (i, 0)),
            pl.BlockSpec((1, D_MODEL), lambda i: (0, 0)),
            pl.BlockSpec((LANES, D_MODEL), lambda i: (0, 0)),
            pl.BlockSpec((1, LANES), lambda i: (0, 0)),
        ],
        out_specs=[
            pl.BlockSpec((bm, D_MODEL), lambda i: (i, 0)),
            pl.BlockSpec((bm, LANES), lambda i: (i, 0)),
            pl.BlockSpec((SUBLANES, bm), lambda i: (0, i)),
        ],
        out_shape=[
            jax.ShapeDtypeStruct((m, D_MODEL), BF16),
            jax.ShapeDtypeStruct((m, LANES), F32),
            jax.ShapeDtypeStruct((SUBLANES, m), F32),
        ],
        compiler_params=_params(("parallel",)),
        name="norm_gates",
    )(x2d, g, wg, bias)


def _gemm_kernel(*refs, n_a, nt, cast_w, shifted_from, epilogue):
    a_refs, w_ref = refs[:n_a], refs[n_a]
    rest = list(refs[n_a + 1:])
    w_hi_ref = rest.pop(0) if shifted_from is not None else None
    x_ref = rest.pop(0) if epilogue in ("residual", "residual_prenorm") else None
    g_ref = rest.pop(0) if epilogue == "residual_prenorm" else None
    ss_in_ref = rest.pop(0) if epilogue == "rownorm_relu2" else None
    o_ref = rest.pop(0)
    if epilogue == "residual_prenorm":
        xg_ref, ss_ref = rest.pop(0), rest.pop(0)

    wb_ref = rest.pop(0) if cast_w else None

    def product(cols, shift=0):
        w_src = w_ref
        if cast_w:
            w_src = wb_ref
            if nt:
                lo, hi = cols.start + shift, cols.stop + shift
                if hi <= w_ref.shape[0]:
                    blk = w_ref[lo:hi, :]
                else:
                    blk = jnp.concatenate([w_ref[lo:, :], w_hi_ref[...]], axis=0)
                wb_ref[cols, :] = blk.astype(BF16)
            else:
                wb_ref[:, cols] = w_ref[:, cols].astype(BF16)
        acc, off = None, 0
        for a_ref in a_refs:
            ka = a_ref.shape[1]
            wk = w_src[cols, off:off + ka] if nt else w_src[off:off + ka, cols]
            part = lax.dot_general(a_ref[...], wk, _NT if nt else _NN, preferred_element_type=F32)
            acc = part if acc is None else acc + part
            off += ka
        return acc

    if epilogue == "residual":
        @pl.when(pl.program_id(2) == 0)
        def _first():
            o_ref[...] = x_ref[...]

        bn = o_ref.shape[1]
        for c in range(0, bn, ACC_CHUNK):
            cols = slice(c, min(c + ACC_CHUNK, bn))
            o_ref[:, cols] += product(cols)
    elif epilogue == "residual_prenorm":
        bn = o_ref.shape[1]
        ss = jnp.zeros(ss_ref.shape, F32)
        for c in range(0, bn, ACC_CHUNK):
            cols = slice(c, min(c + ACC_CHUNK, bn))
            val = x_ref[:, cols] + product(cols)
            o_ref[:, cols] = val
            xg_ref[:, cols] = (val * g_ref[:, cols]).astype(BF16)
            sq = val * val
            for t in range(0, sq.shape[1], LANES):
                ss = ss + sq[:, t:t + LANES]
        ss_ref[...] = ss
    elif epilogue == "rownorm_relu2":
        mean_sq = jnp.sum(ss_in_ref[...], axis=1, keepdims=True) * (1.0 / D_MODEL)
        scale = lax.rsqrt(mean_sq + EPS)
        bn = o_ref.shape[1]
        for c in range(0, bn, ACC_CHUNK):
            cols = slice(c, min(c + ACC_CHUNK, bn))
            u = product(cols) * scale
            o_ref[:, cols] = jnp.square(jnp.maximum(u, 0.0)).astype(o_ref.dtype)
    else:
        bn = o_ref.shape[1]
        step = ACC_CHUNK if cast_w else bn

        def run(shift):
            for c in range(0, bn, step):
                cols = slice(c, min(c + step, bn))
                o_ref[:, cols] = product(cols, shift)

        if shifted_from is None:
            run(0)
        else:
            pl.when(pl.program_id(0) < shifted_from)(functools.partial(run, 0))
            pl.when(pl.program_id(0) >= shifted_from)(functools.partial(run, SUBLANES))


def _gemm(a_list, w, *, name, bm, bn, bk=None, nt=False, cast_w=False, shifted=False,
          epilogue=None, x=None, norm_g=None, row_ss=None, out_dtype=F32):
    m = a_list[0].shape[0]
    k = sum(a.shape[1] for a in a_list)
    n = N_MAIN if shifted else (w.shape[0] if nt else w.shape[1])
    bk = k if bk is None else bk
    k_steps = k // bk
    assert len(a_list) == 1 or k_steps == 1
    assert epilogue == "residual" or k_steps == 1
    assert not cast_w or m == bm
    assert epilogue != "rownorm_relu2" or k == D_MODEL
    in_specs = [pl.BlockSpec((bm, a.shape[1] if k_steps == 1 else bk), lambda j, i, kk: (i, kk))
                for a in a_list]
    w_blk = (bn, bk) if nt else (bk, bn)
    w_map = (lambda j, i, kk: (j, kk)) if nt else (lambda j, i, kk: (kk, j))
    in_specs.append(pl.BlockSpec(w_blk, w_map))
    args = [*a_list, w]
    shifted_from = None
    if shifted:
        assert nt and cast_w and k_steps == 1 and GATE_OFF % bn == 0
        shifted_from = GATE_OFF // bn
        in_specs.append(pl.BlockSpec((SUBLANES, bk), lambda j, i, kk: ((j + 1) * (bn // SUBLANES), 0)))
        args.append(w)
    if epilogue in ("residual", "residual_prenorm"):
        in_specs.append(pl.BlockSpec((bm, bn), lambda j, i, kk: (i, j)))
        args.append(x)
    if epilogue == "residual_prenorm":
        in_specs.append(pl.BlockSpec((1, bn), lambda j, i, kk: (0, j)))
        args.append(norm_g)
    if epilogue == "rownorm_relu2":
        in_specs.append(pl.BlockSpec((bm, row_ss.shape[1]), lambda j, i, kk: (i, 0)))
        args.append(row_ss)
    out_specs = [pl.BlockSpec((bm, bn), lambda j, i, kk: (i, j))]
    out_shape = [jax.ShapeDtypeStruct((m, n), out_dtype)]
    if epilogue == "residual_prenorm":
        out_specs += [pl.BlockSpec((bm, bn), lambda j, i, kk: (i, j)),
                      pl.BlockSpec((bm, LANES), lambda j, i, kk: (i, j))]
        out_shape += [jax.ShapeDtypeStruct((m, n), BF16),
                      jax.ShapeDtypeStruct((m, (n // bn) * LANES), F32)]
    if cast_w:
        out_specs.append(pl.BlockSpec(w_blk, w_map))
        out_shape.append(jax.ShapeDtypeStruct((n, k) if nt else (k, n), BF16))
    res = pl.pallas_call(
        functools.partial(_gemm_kernel, n_a=len(a_list), nt=nt, cast_w=cast_w,
                          shifted_from=shifted_from, epilogue=epilogue),
        grid=(n // bn, m // bm, k_steps),
        in_specs=in_specs,
        out_specs=out_specs,
        out_shape=out_shape,
        compiler_params=_params(("parallel", "parallel", "arbitrary")),
        name=name,
    )(*args)
    return list(res)


def _segment_masks(rows, nseg):
    ri = lax.broadcasted_iota(jnp.int32, (rows, rows), 0)
    ci = lax.broadcasted_iota(jnp.int32, (rows, rows), 1)
    if nseg == 1:
        same = None
        causal = ci <= ri
        upper = ri <= ci
    else:
        shift = int(math.log2(rows // nseg))
        same = lax.shift_right_logical(ri, shift) == lax.shift_right_logical(ci, shift)
        causal = (ci <= ri) & same
        upper = (ri <= ci) & same
    return same, causal, upper


def _state_dot(qb, state_bf16, nseg):
    if nseg == 1:
        return jnp.dot(qb, state_bf16[0], preferred_element_type=F32)
    rps = qb.shape[0] // nseg
    assert 2 * rps == BF16_ROWS
    outs = []
    for p in range(nseg // 2):
        qp = qb[p * BF16_ROWS:(p + 1) * BF16_ROWS]
        outs.append(jnp.dot(qp, state_bf16[2 * p], preferred_element_type=F32)[:rps])
        outs.append(jnp.dot(qp, state_bf16[2 * p + 1], preferred_element_type=F32)[rps:])
    return jnp.concatenate(outs, axis=0)


def _segment_lhs(xt, j, nseg):
    if nseg == 1:
        return xt.astype(BF16)
    rps = xt.shape[1] // nseg
    lane = lax.broadcasted_iota(jnp.int32, xt.shape, 1)
    keep = (lane >= j * rps) & (lane < (j + 1) * rps)
    return jnp.where(keep, xt, 0.0).astype(BF16)


def _col_view(ref, rs, off, width, idx):
    return ref.at[rs, off + idx * width:off + (idx + 1) * width]


def _mlstm_jobs(refs, rows, nseg, has_init, nh, nsub, shared_rows):
    if shared_rows:
        p_ref, refs = refs[0], refs[1:]
        cols = [(p_ref, OFF_QM), (p_ref, OFF_KM), (p_ref, OFF_VM), (p_ref, OFF_OM)]
    else:
        cols, refs = [(r, 0) for r in refs[:4]], refs[4:]
    if has_init:
        gc_ref, gr_ref, gh_ref, m0_ref, c0_ref, n0_ref, h_ref, c_ref, n_ref, m_ref = refs
    else:
        gc_ref, gr_ref, gh_ref, h_ref, c_ref, n_ref, m_ref = refs
    (q_ref, q0), (k_ref, k0), (v_ref, v0), (om_ref, om0) = cols
    jobs = []
    for sc in range(nsub):
        rs = slice(sc * rows, (sc + 1) * rows)
        chunk_jobs = []
        for hh in range(nh):
            vv = slice(hh * DV_M, (hh + 1) * DV_M)
            head_refs = [_col_view(q_ref, rs, q0, DK_M, hh), _col_view(k_ref, rs, k0, DK_M, hh),
                         _col_view(v_ref, rs, v0, DV_M, hh), _col_view(om_ref, rs, om0, DV_M, hh),
                         gc_ref.at[rs, :], gr_ref.at[:, rs], gh_ref.at[:, vv]]
            if has_init:
                head_refs += [m0_ref, c0_ref.at[:, hh], n0_ref.at[:, hh]]
            state = [c_ref.at[:, hh], n_ref.at[:, hh], m_ref.at[hh]]
            head_refs += [h_ref.at[rs, vv]] + state
            zero = state if (not has_init and sc == 0) else []
            chunk_jobs.append((zero, functools.partial(
                _mlstm_head, pl.program_id(1) * nh + hh, head_refs, rows, nseg, has_init)))
        jobs.append(chunk_jobs)
    return jobs


def _mlstm_head(head, refs, rows, nseg, has_init):
    if has_init:
        (q_ref, k_ref, v_ref, om_ref, gc_ref, gr_ref, gh_ref, m0_ref, c0_ref, n0_ref,
         h_ref, c_ref, n_ref, m_ref) = refs
    else:
        (q_ref, k_ref, v_ref, om_ref, gc_ref, gr_ref, gh_ref,
         h_ref, c_ref, n_ref, m_ref) = refs
    rps = rows // nseg
    lane = lax.broadcasted_iota(jnp.int32, (rows, LANES), 1)
    sub = lax.broadcasted_iota(jnp.int32, (SUBLANES, rows), 0)

    def pick_col(x_ref, idx):
        return jnp.sum(jnp.where(lane == idx, x_ref[...], 0.0), axis=1, keepdims=True)

    def pick_row(x_ref, idx):
        return jnp.sum(jnp.where(sub == idx, x_ref[...], 0.0), axis=0, keepdims=True)

    if has_init:
        c_prev, n_prev = c0_ref, n0_ref
        m_col = pick_col(m0_ref, head)
    else:
        c_prev, n_prev = c_ref, n_ref
        m_col = m_ref[:, 0:1]

    i_col, f_col = pick_col(gc_ref, head), pick_col(gc_ref, head + HM)
    i_row, f_row = pick_row(gr_ref, head), pick_row(gr_ref, head + HM)

    same, causal, upper = _segment_masks(rows, nseg)
    b_col = jnp.sum(jnp.where(causal, f_row, 0.0), axis=1, keepdims=True)
    b_row = jnp.sum(jnp.where(upper, f_col, 0.0), axis=0, keepdims=True)
    if same is None:
        bl_col = jnp.sum(f_row, axis=1, keepdims=True) + jnp.zeros_like(b_col)
        bl_row = jnp.sum(f_col, axis=0, keepdims=True) + jnp.zeros_like(b_row)
    else:
        bl_col = jnp.sum(jnp.where(same, f_row, 0.0), axis=1, keepdims=True)
        bl_row = jnp.sum(jnp.where(same, f_col, 0.0), axis=0, keepdims=True)

    dlog = jnp.where(causal, b_col - b_row + i_row, -jnp.inf)
    dmax = jnp.max(dlog, axis=1, keepdims=True)
    inter = b_col + m_col
    mt = jnp.maximum(inter, dmax)
    dw = jnp.exp(dlog - mt)
    iw = jnp.exp(inter - mt)

    q = q_ref[...] * (DK_M ** -0.5)
    k = k_ref[...]
    qb, kb, vb = q.astype(BF16), k.astype(BF16), v_ref[...].astype(BF16)
    s = lax.dot_general(qb, kb, (((1,), (1,)), ((), ())), preferred_element_type=F32) * dw
    intra = jnp.dot(s.astype(BF16), vb, preferred_element_type=F32)
    q_c = _state_dot(qb, [c_prev[j].astype(BF16) for j in range(nseg)], nseg)
    n_tok = jnp.broadcast_to(n_prev[...], (nseg, rps, DK_M)).reshape(rows, DK_M)
    q_n = jnp.sum(q * n_tok, axis=1, keepdims=True)
    num = iw * q_c + intra
    den = iw * q_n + jnp.sum(s, axis=1, keepdims=True)
    hh = num * (1.0 / jnp.maximum(jnp.abs(den), jnp.exp(-mt)))
    h_ref[...] = (_rms(hh, gh_ref[...]) * _sigmoid(om_ref[...])).astype(BF16)

    wlog_col = bl_col - b_col + i_col
    wlog_row = bl_row - b_row + i_row
    if same is None:
        wmax = jnp.max(wlog_row, axis=1, keepdims=True) + jnp.zeros_like(b_col)
    else:
        wmax = jnp.max(jnp.where(same, wlog_row, -jnp.inf), axis=1, keepdims=True)
    m_new = jnp.maximum(bl_col + m_col, wmax)
    decay = jnp.exp(bl_col + m_col - m_new)
    kw = k * jnp.exp(wlog_col - m_new)
    kw_t = kw.T
    for j in range(nseg):
        d_j = decay[j * rps:j * rps + 1, :]
        upd = jnp.dot(_segment_lhs(kw_t, j, nseg), vb, preferred_element_type=F32)
        c_ref[j] = d_j * c_prev[j] + upd
        n_ref[j] = d_j * n_prev[j] + jnp.sum(kw[j * rps:(j + 1) * rps], axis=0, keepdims=True)
    m_ref[...] = jnp.broadcast_to(m_new, (rows, LANES))


def _mlstm_parts(proj, gc, gr, g_head, rows, nseg, nchunks, nh, nsub, init=None,
                 shared_rows=False):
    m = proj.shape[0]
    groups = m // (rows * nchunks)
    has_init = init is not None
    assert not has_init or (nchunks == 1 and nsub == 1)
    steps = nchunks // nsub
    brows = rows * nsub
    row = lambda g, h, c: g * steps + c
    qk_blk, v_blk = nh * DK_M, nh * DV_M
    in_specs = [
        pl.BlockSpec((brows, qk_blk), lambda g, h, c: (row(g, h, c), OFF_QM // qk_blk + h)),
        pl.BlockSpec((brows, qk_blk), lambda g, h, c: (row(g, h, c), OFF_KM // qk_blk + h)),
        pl.BlockSpec((brows, v_blk), lambda g, h, c: (row(g, h, c), OFF_VM // v_blk + h)),
        pl.BlockSpec((brows, v_blk), lambda g, h, c: (row(g, h, c), OFF_OM // v_blk + h)),
        pl.BlockSpec((brows, LANES), lambda g, h, c: (row(g, h, c), 0)),
        pl.BlockSpec((SUBLANES, brows), lambda g, h, c: (0, row(g, h, c))),
        pl.BlockSpec((1, v_blk), lambda g, h, c: (0, h)),
    ]
    args = [proj, proj, proj, proj, gc, gr, g_head]
    if shared_rows:
        assert nh == HM
        in_specs, args = in_specs[4:], args[4:]
    if has_init:
        m0_tok, c0, n0 = init
        in_specs += [
            pl.BlockSpec((rows, LANES), lambda g, h, c: (g, 0)),
            pl.BlockSpec((nseg, nh, DK_M, DV_M), lambda g, h, c: (g, h, 0, 0)),
            pl.BlockSpec((nseg, nh, 1, DK_M), lambda g, h, c: (g, h, 0, 0)),
        ]
        args += [m0_tok, c0, n0]
    nb = groups * nseg
    return dict(
        jobs=functools.partial(_mlstm_jobs, rows=rows, nseg=nseg, has_init=has_init, nh=nh,
                               nsub=nsub, shared_rows=shared_rows),
        grid=(groups, HM // nh, steps),
        in_specs=in_specs,
        args=args,
        out_specs=[
            pl.BlockSpec((brows, v_blk), lambda g, h, c: (row(g, h, c), h)),
            pl.BlockSpec((nseg, nh, DK_M, DV_M), lambda g, h, c: (g, h, 0, 0)),
            pl.BlockSpec((nseg, nh, 1, DK_M), lambda g, h, c: (g, h, 0, 0)),
            pl.BlockSpec((None, nh, rows, LANES), lambda g, h, c: (g, h, 0, 0)),
        ],
        out_shape=[
            jax.ShapeDtypeStruct((m, HM * DV_M), BF16),
            jax.ShapeDtypeStruct((nb, HM, DK_M, DV_M), F32),
            jax.ShapeDtypeStruct((nb, HM, 1, DK_M), F32),
            jax.ShapeDtypeStruct((groups, HM, rows, LANES), F32),
        ],
    )


def _regions(chunk_jobs_per_part):
    return [jobs[i:i + HEADS_PER_REGION]
            for jobs in chunk_jobs_per_part for i in range(0, len(jobs), HEADS_PER_REGION)]


def _run_mixers(parts, name, shared=None):
    grid = parts[0]["grid"]
    assert all(p["grid"] == grid for p in parts)
    n_in = [len(p["in_specs"]) for p in parts]
    n_out = [len(p["out_specs"]) for p in parts]
    n_shared = 0 if shared is None else 1

    def body(*refs):
        common, refs = list(refs[:n_shared]), refs[n_shared:]
        ins, outs = refs[:sum(n_in)], refs[sum(n_in):]
        jobs, i, o = [], 0, 0
        for p, ni, no in zip(parts, n_in, n_out):
            jobs.append(p["jobs"](common + list(ins[i:i + ni]) + list(outs[o:o + no])))
            i, o = i + ni, o + no
        for chunk in range(len(jobs[0])):
            for region in _regions([j[chunk] for j in jobs]):
                zero_refs = [r for zero, _ in region for r in zero]
                if zero_refs:
                    @pl.when(pl.program_id(2) == 0)
                    def _zero_state():
                        for ref in zero_refs:
                            ref[...] = jnp.zeros(ref.shape, F32)
                for _, job in region:
                    job()

    res = pl.pallas_call(
        body,
        grid=grid,
        in_specs=[s for _, s in [shared][:n_shared]] + [s for p in parts for s in p["in_specs"]],
        out_specs=[s for p in parts for s in p["out_specs"]],
        out_shape=[s for p in parts for s in p["out_shape"]],
        compiler_params=_params(("parallel", "parallel", "arbitrary")),
        name=name,
    )(*[a for a, _ in [shared][:n_shared]], *[a for p in parts for a in p["args"]])
    outs, o = [], 0
    for no in n_out:
        outs.append(res[o:o + no])
        o += no
    return outs


def _ret_jobs(refs, rows, nseg, has_init, nh, nsub, shared_rows):
    if shared_rows:
        p_ref, refs = refs[0], refs[1:]
        cols = [(p_ref, OFF_QR), (p_ref, OFF_KR), (p_ref, OFF_VR), (p_ref, OFF_GR)]
    else:
        cols, refs = [(r, 0) for r in refs[:4]], refs[4:]
    if has_init:
        cos_ref, sin_ref, dmat_ref, dec_ref, gh_ref, s0_ref, o_ref, s_ref = refs
    else:
        cos_ref, sin_ref, dmat_ref, dec_ref, gh_ref, o_ref, s_ref = refs
    (q_ref, q0), (k_ref, k0), (v_ref, v0), (g_ref, g0) = cols
    jobs = []
    for sc in range(nsub):
        rs = slice(sc * rows, (sc + 1) * rows)
        chunk_jobs = []
        for hh in range(nh):
            vv = slice(hh * DV_R, (hh + 1) * DV_R)
            head_refs = [_col_view(q_ref, rs, q0, DK_R, hh), _col_view(k_ref, rs, k0, DK_R, hh),
                         _col_view(v_ref, rs, v0, DV_R, hh), _col_view(g_ref, rs, g0, DV_R, hh),
                         cos_ref.at[rs, :], sin_ref.at[rs, :], dmat_ref.at[hh], dec_ref.at[hh],
                         gh_ref.at[:, vv]]
            if has_init:
                head_refs.append(s0_ref.at[:, hh])
            head_refs += [o_ref.at[rs, vv], s_ref.at[:, hh]]
            zero = [s_ref.at[:, hh]] if (not has_init and sc == 0) else []
            chunk_jobs.append((zero, functools.partial(_ret_head, head_refs, rows, nseg, has_init)))
        jobs.append(chunk_jobs)
    return jobs


def _ret_head(refs, rows, nseg, has_init):
    if has_init:
        (q_ref, k_ref, v_ref, g_ref, cos_ref, sin_ref, dmat_ref, dec_ref, gh_ref, s0_ref,
         o_ref, s_ref) = refs
        s_prev = s0_ref
    else:
        (q_ref, k_ref, v_ref, g_ref, cos_ref, sin_ref, dmat_ref, dec_ref, gh_ref,
         o_ref, s_ref) = refs
        s_prev = s_ref

    cosf, sins = cos_ref[...], sin_ref[...]
    half = DK_R // 2

    def rot(x):
        return x * cosf + pltpu.roll(x, half, 1) * sins

    qr = rot(q_ref[...])
    kr = rot(k_ref[...]) * (DK_R ** -0.5)
    dec = dec_ref[...]
    inter, kdec, sdec = dec[:, 0:1], dec[:, 1:2], dec[0:1, 2:3]
    qb, kb, vb = qr.astype(BF16), kr.astype(BF16), v_ref[...].astype(BF16)
    s = lax.dot_general(qb, kb, (((1,), (1,)), ((), ())), preferred_element_type=F32) * dmat_ref[...]
    q_s = _state_dot(qb, [s_prev[j].astype(BF16) for j in range(nseg)], nseg)
    o = jnp.dot(s.astype(BF16), vb, preferred_element_type=F32) + inter * q_s
    g = g_ref[...]
    o_ref[...] = (_rms(o, gh_ref[...]) * (g * _sigmoid(g))).astype(BF16)

    kd_t = (kr * kdec).T
    for j in range(nseg):
        upd = jnp.dot(_segment_lhs(kd_t, j, nseg), vb, preferred_element_type=F32)
        s_ref[j] = sdec * s_prev[j] + upd


def _ret_parts(proj, cos_t, sin_t, dmat, dec, g_head, rows, nseg, nchunks, nh, nsub, init=None,
               shared_rows=False):
    m = proj.shape[0]
    groups = m // (rows * nchunks)
    has_init = init is not None
    assert not has_init or (nchunks == 1 and nsub == 1)
    steps = nchunks // nsub
    brows = rows * nsub
    row = lambda g, h, c: g * steps + c
    qk_blk, v_blk = nh * DK_R, nh * DV_R
    in_specs = [
        pl.BlockSpec((brows, qk_blk), lambda g, h, c: (row(g, h, c), OFF_QR // qk_blk + h)),
        pl.BlockSpec((brows, qk_blk), lambda g, h, c: (row(g, h, c), OFF_KR // qk_blk + h)),
        pl.BlockSpec((brows, v_blk), lambda g, h, c: (row(g, h, c), OFF_VR // v_blk + h)),
        pl.BlockSpec((brows, v_blk), lambda g, h, c: (row(g, h, c), OFF_GR // v_blk + h)),
        pl.BlockSpec((brows, DK_R), lambda g, h, c: (c, 0)),
        pl.BlockSpec((brows, DK_R), lambda g, h, c: (c, 0)),
        pl.BlockSpec((nh, rows, rows), lambda g, h, c: (h, 0, 0)),
        pl.BlockSpec((nh, rows, LANES), lambda g, h, c: (h, 0, 0)),
        pl.BlockSpec((1, v_blk), lambda g, h, c: (0, h)),
    ]
    args = [proj, proj, proj, proj, cos_t, sin_t, dmat, dec, g_head]
    if shared_rows:
        assert nh == HR
        in_specs, args = in_specs[4:], args[4:]
    if has_init:
        in_specs.append(pl.BlockSpec((nseg, nh, DK_R, DV_R), lambda g, h, c: (g, h, 0, 0)))
        args.append(init)
    nb = groups * nseg
    return dict(
        jobs=functools.partial(_ret_jobs, rows=rows, nseg=nseg, has_init=has_init, nh=nh,
                               nsub=nsub, shared_rows=shared_rows),
        grid=(groups, HR // nh, steps),
        in_specs=in_specs,
        args=args,
        out_specs=[
            pl.BlockSpec((brows, v_blk), lambda g, h, c: (row(g, h, c), h)),
            pl.BlockSpec((nseg, nh, DK_R, DV_R), lambda g, h, c: (g, h, 0, 0)),
        ],
        out_shape=[
            jax.ShapeDtypeStruct((m, HR * DV_R), BF16),
            jax.ShapeDtypeStruct((nb, HR, DK_R, DV_R), F32),
        ],
    )


def _retention_tables(seg_len, nseg):
    rows = seg_len * nseg
    idx = (np.arange(rows) % seg_len).astype(np.float64)
    seg = np.arange(rows) // seg_len
    lg = np.log(1.0 - 2.0 ** (-5.0 - np.arange(HR, dtype=np.float64)))
    diff = idx[:, None] - idx[None, :]
    ok = (diff >= 0) & (seg[:, None] == seg[None, :])
    dmat = np.where(ok[None], np.exp(np.maximum(diff, 0.0)[None] * lg[:, None, None]), 0.0)
    dec = np.zeros((HR, rows, LANES))
    dec[:, :, 0] = np.exp((idx[None, :] + 1.0) * lg[:, None])
    dec[:, :, 1] = np.exp((seg_len - 1.0 - idx[None, :]) * lg[:, None])
    dec[:, :, 2] = np.exp(seg_len * lg)[:, None]
    return jnp.asarray(dmat, F32), jnp.asarray(dec, F32)


def _rotary_tables(pos):
    freqs = ROPE_BASE ** (-jnp.arange(0, DK_R, 2, dtype=F32) / DK_R)
    ang = pos[:, None] * freqs[None, :]
    cos, sin = jnp.cos(ang), jnp.sin(ang)
    return jnp.concatenate([cos, cos], axis=-1), jnp.concatenate([-sin, sin], axis=-1)


def _rmsnorm_kernel(x_ref, g_ref, o_ref):
    o_ref[...] = _rms(x_ref[...], g_ref[...]).astype(o_ref.dtype)


def _rmsnorm(x2d, g, out_dtype, bm=512):
    m = x2d.shape[0]
    return pl.pallas_call(
        _rmsnorm_kernel,
        grid=(m // bm,),
        in_specs=[
            pl.BlockSpec((bm, D_MODEL), lambda i: (i, 0)),
            pl.BlockSpec((1, D_MODEL), lambda i: (0, 0)),
        ],
        out_specs=pl.BlockSpec((bm, D_MODEL), lambda i: (i, 0)),
        out_shape=jax.ShapeDtypeStruct((m, D_MODEL), out_dtype),
        compiler_params=_params(("parallel",)),
        name="rmsnorm",
    )(x2d, g)


def _layer(x2d, w, wb, tables, rows, nseg, nchunks, init, g_final, final_norm):
    cast = wb is None
    assert not cast or x2d.shape[0] == GEMM_BM
    cfg = dict(bm=GEMM_BM, bn=GEMM_BN_CAST if cast else GEMM_BN, cast_w=cast)
    new_wb = {}

    def gemm(key, a_list, **kw):
        res = _gemm(a_list, w[key] if cast else wb[key], name=key, **{**cfg, **kw})
        if cast:
            new_wb[key] = res.pop()
        return res[0] if len(res) == 1 else res

    h, gc, gr = _norm_gates(x2d, w["g_mix"], w["wg"], w["gate_bias"])
    proj = gemm("in_proj", [h], nt=True, shifted=cast)
    if init is None:
        nsub = math.gcd(nchunks, PROMPT_CHUNKS_PER_STEP)
        steps = nchunks // nsub
        proj_rows = (proj, pl.BlockSpec((rows * nsub, N_MAIN), lambda g, h, c: (g * steps + c, 0)))
        (hm, c_new, n_new, m_slab), (hr, s_new) = _run_mixers(
            [_mlstm_parts(proj, gc, gr, w["g_mh"], rows, nseg, nchunks, HM, nsub, shared_rows=True),
             _ret_parts(proj, *tables, w["g_rh"], rows, nseg, nchunks, HR, nsub, shared_rows=True)],
            "mixers", shared=proj_rows)
    else:
        m_init = (init["m_tok"], init["C"], init["n"])
        (hm, c_new, n_new, m_slab), = _run_mixers(
            [_mlstm_parts(proj, gc, gr, w["g_mh"], rows, nseg, nchunks, 1, 1, m_init)], "mlstm")
        (hr, s_new), = _run_mixers(
            [_ret_parts(proj, *tables, w["g_rh"], rows, nseg, nchunks, RET_HEADS_WITH_STATE, 1,
                        init["S"])], "retention")
    x1, x1g, x1_ss = gemm("out_proj", [hm, hr], epilogue="residual_prenorm", x=x2d,
                          norm_g=w["g_ffn"])
    act = gemm("ffn_up", [x1g], epilogue="rownorm_relu2", row_ss=x1_ss, out_dtype=BF16)
    down_cfg = dict(bn=GEMM_BN, bk=FFN_DOWN_BK_CAST) if cast else dict(bk=FFN_DOWN_BK)
    y = gemm("ffn_down", [act], epilogue="residual", x=x1, **down_cfg)
    if final_norm:
        y = _rmsnorm(y, g_final, F32)
    return y, c_new, n_new, m_slab, s_new, (new_wb if cast else wb)


def kernel(x_prompt, x_sample, state_mlstm_C, state_mlstm_n, state_mlstm_m, state_ret_S, w_in, b_igate, b_fgate, g_mlstm_head, g_ret_head, w_out, g_norm_mix, g_norm_ffn, w_up, w_down, g_final):
    depth = w_in.shape[0]
    batch, seq, _ = x_prompt.shape
    dec_batch, dec_seq, _ = x_sample.shape
    assert seq % PROMPT_CHUNK == 0 and SAMPLE_ROWS % dec_seq == 0
    assert (dec_batch * dec_seq) % SAMPLE_ROWS == 0
    seg_s = SAMPLE_ROWS // dec_seq
    nchunks_p = seq // PROMPT_CHUNK

    cos_p, sin_p = _rotary_tables(jnp.arange(seq, dtype=F32) + 0.0)
    pos_s = jnp.arange(dec_seq, dtype=F32) + float(PAST_LEN)
    cos_s, sin_s = _rotary_tables(jnp.tile(pos_s, seg_s))
    tables_p = (cos_p, sin_p) + _retention_tables(PROMPT_CHUNK, 1)
    tables_s = (cos_s, sin_s) + _retention_tables(dec_seq, seg_s)

    yp = x_prompt.reshape(batch * seq, D_MODEL)
    ys = x_sample.reshape(dec_batch * dec_seq, D_MODEL)
    g_fin = g_final.reshape(1, D_MODEL)
    outs = {k: [] for k in ("pC", "pn", "pm", "pS", "sC", "sn", "sm", "sS")}
    for l in range(depth):
        w_in_t = w_in[l].T
        w_gate_t = w_in_t[GATE_OFF:GATE_OFF + 2 * HM]
        w = {
            "in_proj": w_in_t,
            "wg": jnp.pad(w_gate_t, ((0, LANES - 2 * HM), (0, 0))).astype(BF16),
            "gate_bias": jnp.pad(jnp.concatenate([b_igate[l], b_fgate[l]]), (0, LANES - 2 * HM)).reshape(1, LANES),
            "g_mix": g_norm_mix[l].reshape(1, D_MODEL),
            "g_ffn": g_norm_ffn[l].reshape(1, D_MODEL),
            "g_mh": g_mlstm_head[l].reshape(1, HM * DV_M),
            "g_rh": g_ret_head[l].reshape(1, HR * DV_R),
            "out_proj": w_out[l],
            "ffn_up": w_up[l],
            "ffn_down": w_down[l],
        }
        last = l == depth - 1
        m_tok = jnp.pad(jnp.repeat(state_mlstm_m[l], dec_seq, axis=0), ((0, 0), (0, LANES - HM)))
        init = {"m_tok": m_tok, "C": state_mlstm_C[l],
                "n": state_mlstm_n[l].reshape(dec_batch, HM, 1, DK_M), "S": state_ret_S[l]}
        ys, c, n, m_slab, s, wb = _layer(ys, w, None, tables_s, SAMPLE_ROWS, seg_s, 1, init, g_fin, last)
        outs["sC"].append(c)
        outs["sn"].append(n.reshape(dec_batch, HM, DK_M))
        m_seg = m_slab[:, :, ::dec_seq, 0]
        outs["sm"].append(m_seg.transpose(0, 2, 1).reshape(dec_batch, HM))
        outs["sS"].append(s)

        yp, c, n, m_slab, s, _ = _layer(yp, w, wb, tables_p, PROMPT_CHUNK, 1, nchunks_p, None, g_fin, last)
        outs["pC"].append(c)
        outs["pn"].append(n.reshape(batch, HM, DK_M))
        outs["pm"].append(m_slab[:, :, 0, 0])
        outs["pS"].append(s)

    def stack(xs):
        return xs[0][None] if len(xs) == 1 else jnp.stack(xs)

    return (yp.reshape(batch, seq, D_MODEL), ys.reshape(dec_batch, dec_seq, D_MODEL),
            stack(outs["pC"]), stack(outs["pn"]), stack(outs["pm"]), stack(outs["pS"]),
            stack(outs["sC"]), stack(outs["sn"]), stack(outs["sm"]), stack(outs["sS"]))
```

```python
import functools
import math

import jax
import jax.numpy as jnp
import numpy as np
from jax import lax
from jax.experimental import pallas as pl
from jax.experimental.pallas import tpu as pltpu

F32 = jnp.float32
BF16 = jnp.bfloat16

D_MODEL = 4096
HM, DK_M, DV_M = 4, 256, 512
HR, DK_R, DV_R = 8, 128, 256
PAST_LEN = 16384
GATE_SOFTCAP = 15.0
ROPE_BASE = 10000.0
EPS = 1e-6

LANES = 128
SUBLANES = 8
BF16_ROWS = 16
VMEM_LIMIT = 60 * 1024 * 1024

N_MAIN = 2 * HM * DK_M + 2 * HM * DV_M + 2 * HR * DK_R + 2 * HR * DV_R
GATE_OFF = 2 * HM * DK_M + 2 * HM * DV_M
OFF_QM, OFF_KM, OFF_VM, OFF_OM = 0, HM * DK_M, 2 * HM * DK_M, 2 * HM * DK_M + HM * DV_M
OFF_QR = GATE_OFF
OFF_KR = OFF_QR + HR * DK_R
OFF_VR = OFF_KR + HR * DK_R
OFF_GR = OFF_VR + HR * DV_R

PROMPT_CHUNK = 256
PROMPT_CHUNKS_PER_STEP = 1
HEADS_PER_REGION = 2
SAMPLE_ROWS = 128
RET_HEADS_WITH_STATE = 4


_NN = (((1,), (0,)), ((), ()))
_NT = (((1,), (1,)), ((), ()))

GEMM_BM = 1024
GEMM_BN = 1024
GEMM_BN_CAST = 512
ACC_CHUNK = 256
FFN_DOWN_BK = 4096


def _params(sem):
    return pltpu.CompilerParams(dimension_semantics=sem, vmem_limit_bytes=VMEM_LIMIT)


def _sigmoid(x):
    return 1.0 / (1.0 + jnp.exp(-x))


def _rms(x, g):
    return x * lax.rsqrt(jnp.mean(x * x, axis=-1, keepdims=True) + EPS) * g


def _norm_gates_kernel(x_ref, g_ref, wg_ref, bias_ref, h_ref, gc_ref, gr_ref):
    hb = _rms(x_ref[...], g_ref[...]).astype(BF16)
    h_ref[...] = hb
    z = lax.dot_general(hb, wg_ref[...], _NT, preferred_element_type=F32) + bias_ref[...]
    zc = GATE_SOFTCAP * jnp.tanh(z / GATE_SOFTCAP)
    logsig = jnp.minimum(zc, 0.0) - jnp.log1p(jnp.exp(-jnp.abs(zc)))
    lane = lax.broadcasted_iota(jnp.int32, zc.shape, 1)
    out = jnp.where(lane >= HM, logsig, zc)
    gc_ref[...] = out
    gr_ref[...] = out.T[:SUBLANES, :]


def _norm_gates(x2d, g, wg, bias, bm=512):
    assert 2 * HM == SUBLANES
    m = x2d.shape[0]
    return pl.pallas_call(
        _norm_gates_kernel,
        grid=(m // bm,),
        in_specs=[
            pl.BlockSpec((bm, D_MODEL), lambda i: (i, 0)),
            pl.BlockSpec((1, D_MODEL), lambda i: (0, 0)),
            pl.BlockSpec((LANES, D_MODEL), lambda i: (0, 0)),
            pl.BlockSpec((1, LANES), lambda i: (0, 0)),
        ],
        out_specs=[
            pl.BlockSpec((bm, D_MODEL), lambda i: (i, 0)),
            pl.BlockSpec((bm, LANES), lambda i: (i, 0)),
            pl.BlockSpec((SUBLANES, bm), lambda i: (0, i)),
        ],
        out_shape=[
            jax.ShapeDtypeStruct((m, D_MODEL), BF16),
            jax.ShapeDtypeStruct((m, LANES), F32),
            jax.ShapeDtypeStruct((SUBLANES, m), F32),
        ],
        compiler_params=_params(("parallel",)),
        name="norm_gates",
    )(x2d, g, wg, bias)


def _gemm_kernel(*refs, n_a, nt, cast_w, shifted_from, epilogue, side_cast):
    a_refs, w_ref = refs[:n_a], refs[n_a]
    rest = list(refs[n_a + 1:])
    w_hi_ref = rest.pop(0) if shifted_from is not None else None
    x_ref = rest.pop(0) if epilogue in ("residual", "residual_prenorm") else None
    g_ref = rest.pop(0) if epilogue == "residual_prenorm" else None
    ss_in_ref = rest.pop(0) if epilogue == "rownorm_relu2" else None
    side_in_ref = rest.pop(0) if side_cast else None
    o_ref = rest.pop(0)
    if epilogue == "residual_prenorm":
        xg_ref, ss_ref = rest.pop(0), rest.pop(0)

    wb_ref = rest.pop(0) if cast_w else None
    if side_cast:
        rest.pop(0)[...] = side_in_ref[...].astype(BF16)

    def product(cols, shift=0):
        w_src = w_ref
        if cast_w:
            w_src = wb_ref
            if nt:
                lo, hi = cols.start + shift, cols.stop + shift
                if hi <= w_ref.shape[0]:
                    blk = w_ref[lo:hi, :]
                else:
                    blk = jnp.concatenate([w_ref[lo:, :], w_hi_ref[...]], axis=0)
                wb_ref[cols, :] = blk.astype(BF16)
            else:
                wb_ref[:, cols] = w_ref[:, cols].astype(BF16)
        acc, off = None, 0
        for a_ref in a_refs:
            ka = a_ref.shape[1]
            wk = w_src[cols, off:off + ka] if nt else w_src[off:off + ka, cols]
            part = lax.dot_general(a_ref[...], wk, _NT if nt else _NN, preferred_element_type=F32)
            acc = part if acc is None else acc + part
            off += ka
        return acc

    if epilogue == "residual":
        @pl.when(pl.program_id(2) == 0)
        def _first():
            o_ref[...] = x_ref[...]

        bn = o_ref.shape[1]
        for c in range(0, bn, ACC_CHUNK):
            cols = slice(c, min(c + ACC_CHUNK, bn))
            o_ref[:, cols] += product(cols)
    elif epilogue == "residual_prenorm":
        bn = o_ref.shape[1]
        ss = jnp.zeros(ss_ref.shape, F32)
        for c in range(0, bn, ACC_CHUNK):
            cols = slice(c, min(c + ACC_CHUNK, bn))
            val = x_ref[:, cols] + product(cols)
            o_ref[:, cols] = val
            xg_ref[:, cols] = (val * g_ref[:, cols]).astype(BF16)
            sq = val * val
            for t in range(0, sq.shape[1], LANES):
                ss = ss + sq[:, t:t + LANES]
        ss_ref[...] = ss
    elif epilogue == "rownorm_relu2":
        mean_sq = jnp.sum(ss_in_ref[...], axis=1, keepdims=True) * (1.0 / D_MODEL)
        scale = lax.rsqrt(mean_sq + EPS)
        bn = o_ref.shape[1]
        for c in range(0, bn, ACC_CHUNK):
            cols = slice(c, min(c + ACC_CHUNK, bn))
            u = product(cols) * scale
            o_ref[:, cols] = jnp.square(jnp.maximum(u, 0.0)).astype(o_ref.dtype)
    else:
        bn = o_ref.shape[1]
        step = ACC_CHUNK if cast_w else bn

        def run(shift):
            for c in range(0, bn, step):
                cols = slice(c, min(c + step, bn))
                o_ref[:, cols] = product(cols, shift)

        if shifted_from is None:
            run(0)
        else:
            pl.when(pl.program_id(0) < shifted_from)(functools.partial(run, 0))
            pl.when(pl.program_id(0) >= shifted_from)(functools.partial(run, SUBLANES))


def _gemm(a_list, w, *, name, bm, bn, bk=None, nt=False, cast_w=False, shifted=False,
          epilogue=None, x=None, norm_g=None, row_ss=None, side_cast=None, out_dtype=F32):
    m = a_list[0].shape[0]
    k = sum(a.shape[1] for a in a_list)
    n = N_MAIN if shifted else (w.shape[0] if nt else w.shape[1])
    bk = k if bk is None else bk
    k_steps = k // bk
    assert len(a_list) == 1 or k_steps == 1
    assert epilogue == "residual" or k_steps == 1
    assert not cast_w or m == bm
    assert epilogue != "rownorm_relu2" or k == D_MODEL
    in_specs = [pl.BlockSpec((bm, a.shape[1] if k_steps == 1 else bk), lambda j, i, kk: (i, kk))
                for a in a_list]
    w_blk = (bn, bk) if nt else (bk, bn)
    w_map = (lambda j, i, kk: (j, kk)) if nt else (lambda j, i, kk: (kk, j))
    in_specs.append(pl.BlockSpec(w_blk, w_map))
    args = [*a_list, w]
    shifted_from = None
    if shifted:
        assert nt and cast_w and k_steps == 1 and GATE_OFF % bn == 0
        shifted_from = GATE_OFF // bn
        in_specs.append(pl.BlockSpec((SUBLANES, bk), lambda j, i, kk: ((j + 1) * (bn // SUBLANES), 0)))
        args.append(w)
    if epilogue in ("residual", "residual_prenorm"):
        in_specs.append(pl.BlockSpec((bm, bn), lambda j, i, kk: (i, j)))
        args.append(x)
    if epilogue == "residual_prenorm":
        in_specs.append(pl.BlockSpec((1, bn), lambda j, i, kk: (0, j)))
        args.append(norm_g)
    if epilogue == "rownorm_relu2":
        in_specs.append(pl.BlockSpec((bm, row_ss.shape[1]), lambda j, i, kk: (i, 0)))
        args.append(row_ss)
    out_specs = [pl.BlockSpec((bm, bn), lambda j, i, kk: (i, j))]
    out_shape = [jax.ShapeDtypeStruct((m, n), out_dtype)]
    if epilogue == "residual_prenorm":
        out_specs += [pl.BlockSpec((bm, bn), lambda j, i, kk: (i, j)),
                      pl.BlockSpec((bm, LANES), lambda j, i, kk: (i, j))]
        out_shape += [jax.ShapeDtypeStruct((m, n), BF16),
                      jax.ShapeDtypeStruct((m, (n // bn) * LANES), F32)]
    if cast_w:
        out_specs.append(pl.BlockSpec(w_blk, w_map))
        out_shape.append(jax.ShapeDtypeStruct((n, k) if nt else (k, n), BF16))
    if side_cast is not None:
        assert k_steps == 1
        m_tiles = m // bm
        side_blk = (side_cast.shape[0] // ((n // bn) * m_tiles), side_cast.shape[1])
        assert side_blk[0] * (n // bn) * m_tiles == side_cast.shape[0]
        side_spec = pl.BlockSpec(side_blk, lambda j, i, kk: (j * m_tiles + i, 0))
        in_specs.append(side_spec)
        args.append(side_cast)
        out_specs.append(side_spec)
        out_shape.append(jax.ShapeDtypeStruct(side_cast.shape, BF16))
    res = pl.pallas_call(
        functools.partial(_gemm_kernel, n_a=len(a_list), nt=nt, cast_w=cast_w,
                          shifted_from=shifted_from, epilogue=epilogue,
                          side_cast=side_cast is not None),
        grid=(n // bn, m // bm, k_steps),
        in_specs=in_specs,
        out_specs=out_specs,
        out_shape=out_shape,
        compiler_params=_params(("parallel", "parallel", "arbitrary")),
        name=name,
    )(*args)
    return list(res)


def _segment_masks(rows, nseg):
    ri = lax.broadcasted_iota(jnp.int32, (rows, rows), 0)
    ci = lax.broadcasted_iota(jnp.int32, (rows, rows), 1)
    if nseg == 1:
        same = None
        causal = ci <= ri
        upper = ri <= ci
    else:
        shift = int(math.log2(rows // nseg))
        same = lax.shift_right_logical(ri, shift) == lax.shift_right_logical(ci, shift)
        causal = (ci <= ri) & same
        upper = (ri <= ci) & same
    return same, causal, upper


def _state_dot(qb, state_bf16, nseg):
    if nseg == 1:
        return jnp.dot(qb, state_bf16[0], preferred_element_type=F32)
    rps = qb.shape[0] // nseg
    assert 2 * rps == BF16_ROWS
    outs = []
    for p in range(nseg // 2):
        qp = qb[p * BF16_ROWS:(p + 1) * BF16_ROWS]
        outs.append(jnp.dot(qp, state_bf16[2 * p], preferred_element_type=F32)[:rps])
        outs.append(jnp.dot(qp, state_bf16[2 * p + 1], preferred_element_type=F32)[rps:])
    return jnp.concatenate(outs, axis=0)


def _segment_lhs(xt, j, nseg):
    if nseg == 1:
        return xt.astype(BF16)
    rps = xt.shape[1] // nseg
    lane = lax.broadcasted_iota(jnp.int32, xt.shape, 1)
    keep = (lane >= j * rps) & (lane < (j + 1) * rps)
    return jnp.where(keep, xt, 0.0).astype(BF16)


def _col_view(ref, rs, off, width, idx):
    return ref.at[rs, off + idx * width:off + (idx + 1) * width]


def _mlstm_jobs(refs, rows, nseg, has_init, nh, nsub, shared_rows):
    if shared_rows:
        p_ref, refs = refs[0], refs[1:]
        cols = [(p_ref, OFF_QM), (p_ref, OFF_KM), (p_ref, OFF_VM), (p_ref, OFF_OM)]
    else:
        cols, refs = [(r, 0) for r in refs[:4]], refs[4:]
    if has_init:
        gc_ref, gr_ref, gh_ref, m0_ref, c0_ref, n0_ref, h_ref, c_ref, n_ref, m_ref = refs
    else:
        gc_ref, gr_ref, gh_ref, h_ref, c_ref, n_ref, m_ref = refs
    (q_ref, q0), (k_ref, k0), (v_ref, v0), (om_ref, om0) = cols
    jobs = []
    for sc in range(nsub):
        rs = slice(sc * rows, (sc + 1) * rows)
        chunk_jobs = []
        for hh in range(nh):
            vv = slice(hh * DV_M, (hh + 1) * DV_M)
            head_refs = [_col_view(q_ref, rs, q0, DK_M, hh), _col_view(k_ref, rs, k0, DK_M, hh),
                         _col_view(v_ref, rs, v0, DV_M, hh), _col_view(om_ref, rs, om0, DV_M, hh),
                         gc_ref.at[rs, :], gr_ref.at[:, rs], gh_ref.at[:, vv]]
            if has_init:
                head_refs += [m0_ref, c0_ref.at[:, hh], n0_ref.at[:, hh]]
            state = [c_ref.at[:, hh], n_ref.at[:, hh], m_ref.at[hh]]
            head_refs += [h_ref.at[rs, vv]] + state
            zero = state if (not has_init and sc == 0) else []
            chunk_jobs.append((zero, functools.partial(
                _mlstm_head, pl.program_id(1) * nh + hh, head_refs, rows, nseg, has_init)))
        jobs.append(chunk_jobs)
    return jobs


def _mlstm_head(head, refs, rows, nseg, has_init):
    if has_init:
        (q_ref, k_ref, v_ref, om_ref, gc_ref, gr_ref, gh_ref, m0_ref, c0_ref, n0_ref,
         h_ref, c_ref, n_ref, m_ref) = refs
    else:
        (q_ref, k_ref, v_ref, om_ref, gc_ref, gr_ref, gh_ref,
         h_ref, c_ref, n_ref, m_ref) = refs
    rps = rows // nseg
    lane = lax.broadcasted_iota(jnp.int32, (rows, LANES), 1)
    sub = lax.broadcasted_iota(jnp.int32, (SUBLANES, rows), 0)

    def pick_col(x_ref, idx):
        return jnp.sum(jnp.where(lane == idx, x_ref[...], 0.0), axis=1, keepdims=True)

    def pick_row(x_ref, idx):
        return jnp.sum(jnp.where(sub == idx, x_ref[...], 0.0), axis=0, keepdims=True)

    if has_init:
        c_prev, n_prev = c0_ref, n0_ref
        m_col = pick_col(m0_ref, head)
    else:
        c_prev, n_prev = c_ref, n_ref
        m_col = m_ref[:, 0:1]

    i_col, f_col = pick_col(gc_ref, head), pick_col(gc_ref, head + HM)
    i_row, f_row = pick_row(gr_ref, head), pick_row(gr_ref, head + HM)

    same, causal, upper = _segment_masks(rows, nseg)
    b_col = jnp.sum(jnp.where(causal, f_row, 0.0), axis=1, keepdims=True)
    b_row = jnp.sum(jnp.where(upper, f_col, 0.0), axis=0, keepdims=True)
    if same is None:
        bl_col = jnp.sum(f_row, axis=1, keepdims=True) + jnp.zeros_like(b_col)
        bl_row = jnp.sum(f_col, axis=0, keepdims=True) + jnp.zeros_like(b_row)
    else:
        bl_col = jnp.sum(jnp.where(same, f_row, 0.0), axis=1, keepdims=True)
        bl_row = jnp.sum(jnp.where(same, f_col, 0.0), axis=0, keepdims=True)

    dlog = jnp.where(causal, b_col - b_row + i_row, -jnp.inf)
    dmax = jnp.max(dlog, axis=1, keepdims=True)
    inter = b_col + m_col
    mt = jnp.maximum(inter, dmax)
    dw = jnp.exp(dlog - mt)
    iw = jnp.exp(inter - mt)

    q = q_ref[...] * (DK_M ** -0.5)
    k = k_ref[...]
    qb, kb, vb = q.astype(BF16), k.astype(BF16), v_ref[...].astype(BF16)
    s = lax.dot_general(qb, kb, (((1,), (1,)), ((), ())), preferred_element_type=F32) * dw
    intra = jnp.dot(s.astype(BF16), vb, preferred_element_type=F32)
    q_c = _state_dot(qb, [c_prev[j].astype(BF16) for j in range(nseg)], nseg)
    n_tok = jnp.broadcast_to(n_prev[...], (nseg, rps, DK_M)).reshape(rows, DK_M)
    q_n = jnp.sum(q * n_tok, axis=1, keepdims=True)
    num = iw * q_c + intra
    den = iw * q_n + jnp.sum(s, axis=1, keepdims=True)
    hh = num * (1.0 / jnp.maximum(jnp.abs(den), jnp.exp(-mt)))
    h_ref[...] = (_rms(hh, gh_ref[...]) * _sigmoid(om_ref[...])).astype(BF16)

    wlog_col = bl_col - b_col + i_col
    wlog_row = bl_row - b_row + i_row
    if same is None:
        wmax = jnp.max(wlog_row, axis=1, keepdims=True) + jnp.zeros_like(b_col)
    else:
        wmax = jnp.max(jnp.where(same, wlog_row, -jnp.inf), axis=1, keepdims=True)
    m_new = jnp.maximum(bl_col + m_col, wmax)
    decay = jnp.exp(bl_col + m_col - m_new)
    kw = k * jnp.exp(wlog_col - m_new)
    kw_t = kw.T
    for j in range(nseg):
        d_j = decay[j * rps:j * rps + 1, :]
        upd = jnp.dot(_segment_lhs(kw_t, j, nseg), vb, preferred_element_type=F32)
        c_ref[j] = d_j * c_prev[j] + upd
        n_ref[j] = d_j * n_prev[j] + jnp.sum(kw[j * rps:(j + 1) * rps], axis=0, keepdims=True)
    m_ref[...] = jnp.broadcast_to(m_new, (rows, LANES))


def _mlstm_parts(proj, gc, gr, g_head, rows, nseg, nchunks, nh, nsub, init=None,
                 shared_rows=False):
    m = proj.shape[0]
    groups = m // (rows * nchunks)
    has_init = init is not None
    assert not has_init or (nchunks == 1 and nsub == 1)
    steps = nchunks // nsub
    brows = rows * nsub
    row = lambda g, h, c: g * steps + c
    qk_blk, v_blk = nh * DK_M, nh * DV_M
    in_specs = [
        pl.BlockSpec((brows, qk_blk), lambda g, h, c: (row(g, h, c), OFF_QM // qk_blk + h)),
        pl.BlockSpec((brows, qk_blk), lambda g, h, c: (row(g, h, c), OFF_KM // qk_blk + h)),
        pl.BlockSpec((brows, v_blk), lambda g, h, c: (row(g, h, c), OFF_VM // v_blk + h)),
        pl.BlockSpec((brows, v_blk), lambda g, h, c: (row(g, h, c), OFF_OM // v_blk + h)),
        pl.BlockSpec((brows, LANES), lambda g, h, c: (row(g, h, c), 0)),
        pl.BlockSpec((SUBLANES, brows), lambda g, h, c: (0, row(g, h, c))),
        pl.BlockSpec((1, v_blk), lambda g, h, c: (0, h)),
    ]
    args = [proj, proj, proj, proj, gc, gr, g_head]
    if shared_rows:
        assert nh == HM
        in_specs, args = in_specs[4:], args[4:]
    if has_init:
        m0_tok, c0, n0 = init
        in_specs += [
            pl.BlockSpec((rows, LANES), lambda g, h, c: (g, 0)),
            pl.BlockSpec((nseg, nh, DK_M, DV_M), lambda g, h, c: (g, h, 0, 0)),
            pl.BlockSpec((nseg, nh, 1, DK_M), lambda g, h, c: (g, h, 0, 0)),
        ]
        args += [m0_tok, c0, n0]
    nb = groups * nseg
    return dict(
        jobs=functools.partial(_mlstm_jobs, rows=rows, nseg=nseg, has_init=has_init, nh=nh,
                               nsub=nsub, shared_rows=shared_rows),
        grid=(groups, HM // nh, steps),
        in_specs=in_specs,
        args=args,
        out_specs=[
            pl.BlockSpec((brows, v_blk), lambda g, h, c: (row(g, h, c), h)),
            pl.BlockSpec((nseg, nh, DK_M, DV_M), lambda g, h, c: (g, h, 0, 0)),
            pl.BlockSpec((nseg, nh, 1, DK_M), lambda g, h, c: (g, h, 0, 0)),
            pl.BlockSpec((None, nh, rows, LANES), lambda g, h, c: (g, h, 0, 0)),
        ],
        out_shape=[
            jax.ShapeDtypeStruct((m, HM * DV_M), BF16),
            jax.ShapeDtypeStruct((nb, HM, DK_M, DV_M), F32),
            jax.ShapeDtypeStruct((nb, HM, 1, DK_M), F32),
            jax.ShapeDtypeStruct((groups, HM, rows, LANES), F32),
        ],
    )


def _regions(chunk_jobs_per_part):
    return [jobs[i:i + HEADS_PER_REGION]
            for jobs in chunk_jobs_per_part for i in range(0, len(jobs), HEADS_PER_REGION)]


def _run_mixers(parts, name, shared=None):
    grid = parts[0]["grid"]
    assert all(p["grid"] == grid for p in parts)
    n_in = [len(p["in_specs"]) for p in parts]
    n_out = [len(p["out_specs"]) for p in parts]
    n_shared = 0 if shared is None else 1

    def body(*refs):
        common, refs = list(refs[:n_shared]), refs[n_shared:]
        ins, outs = refs[:sum(n_in)], refs[sum(n_in):]
        jobs, i, o = [], 0, 0
        for p, ni, no in zip(parts, n_in, n_out):
            jobs.append(p["jobs"](common + list(ins[i:i + ni]) + list(outs[o:o + no])))
            i, o = i + ni, o + no
        for chunk in range(len(jobs[0])):
            for region in _regions([j[chunk] for j in jobs]):
                zero_refs = [r for zero, _ in region for r in zero]
                if zero_refs:
                    @pl.when(pl.program_id(2) == 0)
                    def _zero_state():
                        for ref in zero_refs:
                            ref[...] = jnp.zeros(ref.shape, F32)
                for _, job in region:
                    job()

    res = pl.pallas_call(
        body,
        grid=grid,
        in_specs=[s for _, s in [shared][:n_shared]] + [s for p in parts for s in p["in_specs"]],
        out_specs=[s for p in parts for s in p["out_specs"]],
        out_shape=[s for p in parts for s in p["out_shape"]],
        compiler_params=_params(("parallel", "parallel", "arbitrary")),
        name=name,
    )(*[a for a, _ in [shared][:n_shared]], *[a for p in parts for a in p["args"]])
    outs, o = [], 0
    for no in n_out:
        outs.append(res[o:o + no])
        o += no
    return outs


def _ret_jobs(refs, rows, nseg, has_init, nh, nsub, shared_rows):
    if shared_rows:
        p_ref, refs = refs[0], refs[1:]
        cols = [(p_ref, OFF_QR), (p_ref, OFF_KR), (p_ref, OFF_VR), (p_ref, OFF_GR)]
    else:
        cols, refs = [(r, 0) for r in refs[:4]], refs[4:]
    if has_init:
        cos_ref, sin_ref, dmat_ref, dec_ref, gh_ref, s0_ref, o_ref, s_ref = refs
    else:
        cos_ref, sin_ref, dmat_ref, dec_ref, gh_ref, o_ref, s_ref = refs
    (q_ref, q0), (k_ref, k0), (v_ref, v0), (g_ref, g0) = cols
    jobs = []
    for sc in range(nsub):
        rs = slice(sc * rows, (sc + 1) * rows)
        chunk_jobs = []
        for hh in range(nh):
            vv = slice(hh * DV_R, (hh + 1) * DV_R)
            head_refs = [_col_view(q_ref, rs, q0, DK_R, hh), _col_view(k_ref, rs, k0, DK_R, hh),
                         _col_view(v_ref, rs, v0, DV_R, hh), _col_view(g_ref, rs, g0, DV_R, hh),
                         cos_ref.at[rs, :], sin_ref.at[rs, :], dmat_ref.at[hh], dec_ref.at[hh],
                         gh_ref.at[:, vv]]
            if has_init:
                head_refs.append(s0_ref.at[:, hh])
            head_refs += [o_ref.at[rs, vv], s_ref.at[:, hh]]
            zero = [s_ref.at[:, hh]] if (not has_init and sc == 0) else []
            chunk_jobs.append((zero, functools.partial(_ret_head, head_refs, rows, nseg, has_init)))
        jobs.append(chunk_jobs)
    return jobs


def _ret_head(refs, rows, nseg, has_init):
    if has_init:
        (q_ref, k_ref, v_ref, g_ref, cos_ref, sin_ref, dmat_ref, dec_ref, gh_ref, s0_ref,
         o_ref, s_ref) = refs
        s_prev = s0_ref
    else:
        (q_ref, k_ref, v_ref, g_ref, cos_ref, sin_ref, dmat_ref, dec_ref, gh_ref,
         o_ref, s_ref) = refs
        s_prev = s_ref

    cosf, sins = cos_ref[...], sin_ref[...]
    half = DK_R // 2

    def rot(x):
        return x * cosf + pltpu.roll(x, half, 1) * sins

    qr = rot(q_ref[...])
    kr = rot(k_ref[...]) * (DK_R ** -0.5)
    dec = dec_ref[...]
    inter, kdec, sdec = dec[:, 0:1], dec[:, 1:2], dec[0:1, 2:3]
    qb, kb, vb = qr.astype(BF16), kr.astype(BF16), v_ref[...].astype(BF16)
    s = lax.dot_general(qb, kb, (((1,), (1,)), ((), ())), preferred_element_type=F32) * dmat_ref[...]
    q_s = _state_dot(qb, [s_prev[j].astype(BF16) for j in range(nseg)], nseg)
    o = jnp.dot(s.astype(BF16), vb, preferred_element_type=F32) + inter * q_s
    g = g_ref[...]
    o_ref[...] = (_rms(o, gh_ref[...]) * (g * _sigmoid(g))).astype(BF16)

    kd_t = (kr * kdec).T
    for j in range(nseg):
        upd = jnp.dot(_segment_lhs(kd_t, j, nseg), vb, preferred_element_type=F32)
        s_ref[j] = sdec * s_prev[j] + upd


def _ret_parts(proj, cos_t, sin_t, dmat, dec, g_head, rows, nseg, nchunks, nh, nsub, init=None,
               shared_rows=False):
    m = proj.shape[0]
    groups = m // (rows * nchunks)
    has_init = init is not None
    assert not has_init or (nchunks == 1 and nsub == 1)
    steps = nchunks // nsub
    brows = rows * nsub
    row = lambda g, h, c: g * steps + c
    qk_blk, v_blk = nh * DK_R, nh * DV_R
    in_specs = [
        pl.BlockSpec((brows, qk_blk), lambda g, h, c: (row(g, h, c), OFF_QR // qk_blk + h)),
        pl.BlockSpec((brows, qk_blk), lambda g, h, c: (row(g, h, c), OFF_KR // qk_blk + h)),
        pl.BlockSpec((brows, v_blk), lambda g, h, c: (row(g, h, c), OFF_VR // v_blk + h)),
        pl.BlockSpec((brows, v_blk), lambda g, h, c: (row(g, h, c), OFF_GR // v_blk + h)),
        pl.BlockSpec((brows, DK_R), lambda g, h, c: (c, 0)),
        pl.BlockSpec((brows, DK_R), lambda g, h, c: (c, 0)),
        pl.BlockSpec((nh, rows, rows), lambda g, h, c: (h, 0, 0)),
        pl.BlockSpec((nh, rows, LANES), lambda g, h, c: (h, 0, 0)),
        pl.BlockSpec((1, v_blk), lambda g, h, c: (0, h)),
    ]
    args = [proj, proj, proj, proj, cos_t, sin_t, dmat, dec, g_head]
    if shared_rows:
        assert nh == HR
        in_specs, args = in_specs[4:], args[4:]
    if has_init:
        in_specs.append(pl.BlockSpec((nseg, nh, DK_R, DV_R), lambda g, h, c: (g, h, 0, 0)))
        args.append(init)
    nb = groups * nseg
    return dict(
        jobs=functools.partial(_ret_jobs, rows=rows, nseg=nseg, has_init=has_init, nh=nh,
                               nsub=nsub, shared_rows=shared_rows),
        grid=(groups, HR // nh, steps),
        in_specs=in_specs,
        args=args,
        out_specs=[
            pl.BlockSpec((brows, v_blk), lambda g, h, c: (row(g, h, c), h)),
            pl.BlockSpec((nseg, nh, DK_R, DV_R), lambda g, h, c: (g, h, 0, 0)),
        ],
        out_shape=[
            jax.ShapeDtypeStruct((m, HR * DV_R), BF16),
            jax.ShapeDtypeStruct((nb, HR, DK_R, DV_R), F32),
        ],
    )


def _retention_tables(seg_len, nseg):
    rows = seg_len * nseg
    idx = (np.arange(rows) % seg_len).astype(np.float64)
    seg = np.arange(rows) // seg_len
    lg = np.log(1.0 - 2.0 ** (-5.0 - np.arange(HR, dtype=np.float64)))
    diff = idx[:, None] - idx[None, :]
    ok = (diff >= 0) & (seg[:, None] == seg[None, :])
    dmat = np.where(ok[None], np.exp(np.maximum(diff, 0.0)[None] * lg[:, None, None]), 0.0)
    dec = np.zeros((HR, rows, LANES))
    dec[:, :, 0] = np.exp((idx[None, :] + 1.0) * lg[:, None])
    dec[:, :, 1] = np.exp((seg_len - 1.0 - idx[None, :]) * lg[:, None])
    dec[:, :, 2] = np.exp(seg_len * lg)[:, None]
    return jnp.asarray(dmat, F32), jnp.asarray(dec, F32)


def _rotary_tables(pos):
    freqs = ROPE_BASE ** (-jnp.arange(0, DK_R, 2, dtype=F32) / DK_R)
    ang = pos[:, None] * freqs[None, :]
    cos, sin = jnp.cos(ang), jnp.sin(ang)
    return jnp.concatenate([cos, cos], axis=-1), jnp.concatenate([-sin, sin], axis=-1)


def _rmsnorm_kernel(x_ref, g_ref, o_ref):
    o_ref[...] = _rms(x_ref[...], g_ref[...]).astype(o_ref.dtype)


def _rmsnorm(x2d, g, out_dtype, bm=512):
    m = x2d.shape[0]
    return pl.pallas_call(
        _rmsnorm_kernel,
        grid=(m // bm,),
        in_specs=[
            pl.BlockSpec((bm, D_MODEL), lambda i: (i, 0)),
            pl.BlockSpec((1, D_MODEL), lambda i: (0, 0)),
        ],
        out_specs=pl.BlockSpec((bm, D_MODEL), lambda i: (i, 0)),
        out_shape=jax.ShapeDtypeStruct((m, D_MODEL), out_dtype),
        compiler_params=_params(("parallel",)),
        name="rmsnorm",
    )(x2d, g)


def _layer_front(x2d, w, wb, tables, rows, nseg, nchunks, init):
    cast = wb is None
    assert not cast or x2d.shape[0] == GEMM_BM
    cfg = dict(bm=GEMM_BM, bn=GEMM_BN_CAST if cast else GEMM_BN, cast_w=cast)
    new_wb = {}

    def gemm(key, a_list, **kw):
        res = _gemm(a_list, w[key] if cast else wb[key], name=key, **{**cfg, **kw})
        if cast:
            new_wb[key] = res.pop()
        return res[0] if len(res) == 1 else res

    h, gc, gr = _norm_gates(x2d, w["g_mix"], w["wg"], w["gate_bias"])
    proj = gemm("in_proj", [h], nt=True, shifted=cast)
    if init is None:
        nsub = math.gcd(nchunks, PROMPT_CHUNKS_PER_STEP)
        steps = nchunks // nsub
        proj_rows = (proj, pl.BlockSpec((rows * nsub, N_MAIN), lambda g, h, c: (g * steps + c, 0)))
        (hm, c_new, n_new, m_slab), (hr, s_new) = _run_mixers(
            [_mlstm_parts(proj, gc, gr, w["g_mh"], rows, nseg, nchunks, HM, nsub, shared_rows=True),
             _ret_parts(proj, *tables, w["g_rh"], rows, nseg, nchunks, HR, nsub, shared_rows=True)],
            "mixers", shared=proj_rows)
    else:
        m_init = (init["m_tok"], init["C"], init["n"])
        (hm, c_new, n_new, m_slab), = _run_mixers(
            [_mlstm_parts(proj, gc, gr, w["g_mh"], rows, nseg, nchunks, 1, 1, m_init)], "mlstm")
        (hr, s_new), = _run_mixers(
            [_ret_parts(proj, *tables, w["g_rh"], rows, nseg, nchunks, RET_HEADS_WITH_STATE, 1,
                        init["S"])], "retention")
    x1, x1g, x1_ss = gemm("out_proj", [hm, hr], epilogue="residual_prenorm", x=x2d,
                          norm_g=w["g_ffn"])
    if cast:
        act = gemm("ffn_up", [x1g], epilogue="rownorm_relu2", row_ss=x1_ss, out_dtype=BF16)
        wb_out = new_wb
    else:
        act, w_down = gemm("ffn_up", [x1g], epilogue="rownorm_relu2", row_ss=x1_ss,
                           out_dtype=BF16, side_cast=w["ffn_down"])
        wb_out = {**wb, "ffn_down": w_down}
    return x1, act, (c_new, n_new, m_slab, s_new), wb_out


def _layer_back(x1, act, w_down, g_final, final_norm):
    y, = _gemm([act], w_down, name="ffn_down", bm=GEMM_BM, bn=GEMM_BN, bk=FFN_DOWN_BK,
               epilogue="residual", x=x1)
    return _rmsnorm(y, g_final, F32) if final_norm else y


def kernel(x_prompt, x_sample, state_mlstm_C, state_mlstm_n, state_mlstm_m, state_ret_S, w_in, b_igate, b_fgate, g_mlstm_head, g_ret_head, w_out, g_norm_mix, g_norm_ffn, w_up, w_down, g_final):
    depth = w_in.shape[0]
    batch, seq, _ = x_prompt.shape
    dec_batch, dec_seq, _ = x_sample.shape
    assert seq % PROMPT_CHUNK == 0 and SAMPLE_ROWS % dec_seq == 0
    assert (dec_batch * dec_seq) % SAMPLE_ROWS == 0
    seg_s = SAMPLE_ROWS // dec_seq
    nchunks_p = seq // PROMPT_CHUNK

    cos_p, sin_p = _rotary_tables(jnp.arange(seq, dtype=F32) + 0.0)
    pos_s = jnp.arange(dec_seq, dtype=F32) + float(PAST_LEN)
    cos_s, sin_s = _rotary_tables(jnp.tile(pos_s, seg_s))
    tables_p = (cos_p, sin_p) + _retention_tables(PROMPT_CHUNK, 1)
    tables_s = (cos_s, sin_s) + _retention_tables(dec_seq, seg_s)

    yp = x_prompt.reshape(batch * seq, D_MODEL)
    ys = x_sample.reshape(dec_batch * dec_seq, D_MODEL)
    g_fin = g_final.reshape(1, D_MODEL)
    outs = {k: [] for k in ("pC", "pn", "pm", "pS", "sC", "sn", "sm", "sS")}
    for l in range(depth):
        w_in_t = w_in[l].T
        w_gate_t = w_in_t[GATE_OFF:GATE_OFF + 2 * HM]
        w = {
            "in_proj": w_in_t,
            "wg": jnp.pad(w_gate_t, ((0, LANES - 2 * HM), (0, 0))).astype(BF16),
            "gate_bias": jnp.pad(jnp.concatenate([b_igate[l], b_fgate[l]]), (0, LANES - 2 * HM)).reshape(1, LANES),
            "g_mix": g_norm_mix[l].reshape(1, D_MODEL),
            "g_ffn": g_norm_ffn[l].reshape(1, D_MODEL),
            "g_mh": g_mlstm_head[l].reshape(1, HM * DV_M),
            "g_rh": g_ret_head[l].reshape(1, HR * DV_R),
            "out_proj": w_out[l],
            "ffn_up": w_up[l],
            "ffn_down": w_down[l],
        }
        last = l == depth - 1
        m_tok = jnp.pad(jnp.repeat(state_mlstm_m[l], dec_seq, axis=0), ((0, 0), (0, LANES - HM)))
        init = {"m_tok": m_tok, "C": state_mlstm_C[l],
                "n": state_mlstm_n[l].reshape(dec_batch, HM, 1, DK_M), "S": state_ret_S[l]}
        xs1, act_s, (c, n, m_slab, s), wb = _layer_front(
            ys, w, None, tables_s, SAMPLE_ROWS, seg_s, 1, init)
        outs["sC"].append(c)
        outs["sn"].append(n.reshape(dec_batch, HM, DK_M))
        m_seg = m_slab[:, :, ::dec_seq, 0]
        outs["sm"].append(m_seg.transpose(0, 2, 1).reshape(dec_batch, HM))
        outs["sS"].append(s)

        xp1, act_p, (c, n, m_slab, s), wb = _layer_front(
            yp, w, wb, tables_p, PROMPT_CHUNK, 1, nchunks_p, None)
        outs["pC"].append(c)
        outs["pn"].append(n.reshape(batch, HM, DK_M))
        outs["pm"].append(m_slab[:, :, 0, 0])
        outs["pS"].append(s)

        ys = _layer_back(xs1, act_s, wb["ffn_down"], g_fin, last)
        yp = _layer_back(xp1, act_p, wb["ffn_down"], g_fin, last)

    def stack(xs):
        return xs[0][None] if len(xs) == 1 else jnp.stack(xs)

    return (yp.reshape(batch, seq, D_MODEL), ys.reshape(dec_batch, dec_seq, D_MODEL),
            stack(outs["pC"]), stack(outs["pn"]), stack(outs["pm"]), stack(outs["pS"]),
            stack(outs["sC"]), stack(outs["sn"]), stack(outs["sm"]), stack(outs["sS"]))
```

```python
import functools
import math

import jax
import jax.numpy as jnp
import numpy as np
from jax import lax
from jax.experimental import pallas as pl
from jax.experimental.pallas import tpu as pltpu

F32 = jnp.float32
BF16 = jnp.bfloat16

D_MODEL = 4096
HM, DK_M, DV_M = 4, 256, 512
HR, DK_R, DV_R = 8, 128, 256
PAST_LEN = 16384
GATE_SOFTCAP = 15.0
ROPE_BASE = 10000.0
EPS = 1e-6

LANES = 128
SUBLANES = 8
BF16_ROWS = 16
VMEM_LIMIT = 60 * 1024 * 1024

N_MAIN = 2 * HM * DK_M + 2 * HM * DV_M + 2 * HR * DK_R + 2 * HR * DV_R
GATE_OFF = 2 * HM * DK_M + 2 * HM * DV_M
OFF_QM, OFF_KM, OFF_VM, OFF_OM = 0, HM * DK_M, 2 * HM * DK_M, 2 * HM * DK_M + HM * DV_M
OFF_QR = GATE_OFF
OFF_KR = OFF_QR + HR * DK_R
OFF_VR = OFF_KR + HR * DK_R
OFF_GR = OFF_VR + HR * DV_R

PROMPT_CHUNK = 256
PROMPT_CHUNKS_PER_STEP = 1
HEADS_PER_REGION = 2
SAMPLE_ROWS = 128
RET_HEADS_WITH_STATE = 4


_NN = (((1,), (0,)), ((), ()))
_NT = (((1,), (1,)), ((), ()))

GEMM_BM = 1024
GEMM_BN = 1024
GEMM_BN_CAST = 512
ACC_CHUNK = 256
FFN_DOWN_BK = 4096


def _params(sem):
    return pltpu.CompilerParams(dimension_semantics=sem, vmem_limit_bytes=VMEM_LIMIT)


def _sigmoid(x):
    return 1.0 / (1.0 + jnp.exp(-x))


def _rms(x, g):
    return x * lax.rsqrt(jnp.mean(x * x, axis=-1, keepdims=True) + EPS) * g


def _norm_gates_kernel(x_ref, g_ref, wg_ref, bias_ref, h_ref, gc_ref, gr_ref):
    hb = _rms(x_ref[...], g_ref[...]).astype(BF16)
    h_ref[...] = hb
    z = lax.dot_general(hb, wg_ref[...], _NT, preferred_element_type=F32) + bias_ref[...]
    zc = GATE_SOFTCAP * jnp.tanh(z / GATE_SOFTCAP)
    logsig = jnp.minimum(zc, 0.0) - jnp.log1p(jnp.exp(-jnp.abs(zc)))
    lane = lax.broadcasted_iota(jnp.int32, zc.shape, 1)
    out = jnp.where(lane >= HM, logsig, zc)
    gc_ref[...] = out
    gr_ref[...] = out.T[:SUBLANES, :]


def _norm_gates(x2d, g, wg, bias, bm=512):
    assert 2 * HM == SUBLANES
    m = x2d.shape[0]
    return pl.pallas_call(
        _norm_gates_kernel,
        grid=(m // bm,),
        in_specs=[
            pl.BlockSpec((bm, D_MODEL), lambda i: (i, 0)),
            pl.BlockSpec((1, D_MODEL), lambda i: (0, 0)),
            pl.BlockSpec((LANES, D_MODEL), lambda i: (0, 0)),
            pl.BlockSpec((1, LANES), lambda i: (0, 0)),
        ],
        out_specs=[
            pl.BlockSpec((bm, D_MODEL), lambda i: (i, 0)),
            pl.BlockSpec((bm, LANES), lambda i: (i, 0)),
            pl.BlockSpec((SUBLANES, bm), lambda i: (0, i)),
        ],
        out_shape=[
            jax.ShapeDtypeStruct((m, D_MODEL), BF16),
            jax.ShapeDtypeStruct((m, LANES), F32),
            jax.ShapeDtypeStruct((SUBLANES, m), F32),
        ],
        compiler_params=_params(("parallel",)),
        name="norm_gates",
    )(x2d, g, wg, bias)


def _gemm_kernel(*refs, n_a, nt, cast_w, shifted_from, epilogue, side_cast):
    a_refs, w_ref = refs[:n_a], refs[n_a]
    rest = list(refs[n_a + 1:])
    w_hi_ref = rest.pop(0) if shifted_from is not None else None
    x_ref = rest.pop(0) if epilogue in ("residual", "residual_prenorm") else None
    g_ref = rest.pop(0) if epilogue == "residual_prenorm" else None
    ss_in_ref = rest.pop(0) if epilogue == "rownorm_relu2" else None
    side_in_ref = rest.pop(0) if side_cast else None
    o_ref = rest.pop(0)
    if epilogue == "residual_prenorm":
        xg_ref, ss_ref = rest.pop(0), rest.pop(0)

    wb_ref = rest.pop(0) if cast_w else None
    if side_cast:
        rest.pop(0)[...] = side_in_ref[...].astype(BF16)

    def product(cols, shift=0):
        w_src = w_ref
        if cast_w:
            w_src = wb_ref
            if nt:
                lo, hi = cols.start + shift, cols.stop + shift
                if hi <= w_ref.shape[0]:
                    blk = w_ref[lo:hi, :]
                else:
                    blk = jnp.concatenate([w_ref[lo:, :], w_hi_ref[...]], axis=0)
                wb_ref[cols, :] = blk.astype(BF16)
            else:
                wb_ref[:, cols] = w_ref[:, cols].astype(BF16)
        acc, off = None, 0
        for a_ref in a_refs:
            ka = a_ref.shape[1]
            wk = w_src[cols, off:off + ka] if nt else w_src[off:off + ka, cols]
            part = lax.dot_general(a_ref[...], wk, _NT if nt else _NN, preferred_element_type=F32)
            acc = part if acc is None else acc + part
            off += ka
        return acc

    if epilogue == "residual":
        def accumulate(base_ref):
            bn = o_ref.shape[1]
            for c in range(0, bn, ACC_CHUNK):
                cols = slice(c, min(c + ACC_CHUNK, bn))
                o_ref[:, cols] = base_ref[:, cols] + product(cols)

        pl.when(pl.program_id(2) == 0)(functools.partial(accumulate, x_ref))
        pl.when(pl.program_id(2) != 0)(functools.partial(accumulate, o_ref))
    elif epilogue == "residual_prenorm":
        bn = o_ref.shape[1]
        ss = jnp.zeros(ss_ref.shape, F32)
        for c in range(0, bn, ACC_CHUNK):
            cols = slice(c, min(c + ACC_CHUNK, bn))
            val = x_ref[:, cols] + product(cols)
            o_ref[:, cols] = val
            xg_ref[:, cols] = (val * g_ref[:, cols]).astype(BF16)
            sq = val * val
            for t in range(0, sq.shape[1], LANES):
                ss = ss + sq[:, t:t + LANES]
        ss_ref[...] = ss
    elif epilogue == "rownorm_relu2":
        mean_sq = jnp.sum(ss_in_ref[...], axis=1, keepdims=True) * (1.0 / D_MODEL)
        scale = lax.rsqrt(mean_sq + EPS)
        bn = o_ref.shape[1]
        for c in range(0, bn, ACC_CHUNK):
            cols = slice(c, min(c + ACC_CHUNK, bn))
            u = product(cols) * scale
            o_ref[:, cols] = jnp.square(jnp.maximum(u, 0.0)).astype(o_ref.dtype)
    else:
        bn = o_ref.shape[1]
        step = ACC_CHUNK if cast_w else bn

        def run(shift):
            for c in range(0, bn, step):
                cols = slice(c, min(c + step, bn))
                o_ref[:, cols] = product(cols, shift)

        if shifted_from is None:
            run(0)
        else:
            pl.when(pl.program_id(0) < shifted_from)(functools.partial(run, 0))
            pl.when(pl.program_id(0) >= shifted_from)(functools.partial(run, SUBLANES))


def _gemm(a_list, w, *, name, bm, bn, bk=None, nt=False, cast_w=False, shifted=False,
          epilogue=None, x=None, norm_g=None, row_ss=None, side_cast=None, out_dtype=F32):
    m = a_list[0].shape[0]
    k = sum(a.shape[1] for a in a_list)
    n = N_MAIN if shifted else (w.shape[0] if nt else w.shape[1])
    bk = k if bk is None else bk
    k_steps = k // bk
    assert len(a_list) == 1 or k_steps == 1
    assert epilogue == "residual" or k_steps == 1
    assert not cast_w or m == bm
    assert epilogue != "rownorm_relu2" or k == D_MODEL
    in_specs = [pl.BlockSpec((bm, a.shape[1] if k_steps == 1 else bk), lambda j, i, kk: (i, kk))
                for a in a_list]
    w_blk = (bn, bk) if nt else (bk, bn)
    w_map = (lambda j, i, kk: (j, kk)) if nt else (lambda j, i, kk: (kk, j))
    in_specs.append(pl.BlockSpec(w_blk, w_map))
    args = [*a_list, w]
    shifted_from = None
    if shifted:
        assert nt and cast_w and k_steps == 1 and GATE_OFF % bn == 0
        shifted_from = GATE_OFF // bn
        in_specs.append(pl.BlockSpec((SUBLANES, bk), lambda j, i, kk: ((j + 1) * (bn // SUBLANES), 0)))
        args.append(w)
    if epilogue in ("residual", "residual_prenorm"):
        in_specs.append(pl.BlockSpec((bm, bn), lambda j, i, kk: (i, j)))
        args.append(x)
    if epilogue == "residual_prenorm":
        in_specs.append(pl.BlockSpec((1, bn), lambda j, i, kk: (0, j)))
        args.append(norm_g)
    if epilogue == "rownorm_relu2":
        in_specs.append(pl.BlockSpec((bm, row_ss.shape[1]), lambda j, i, kk: (i, 0)))
        args.append(row_ss)
    out_specs = [pl.BlockSpec((bm, bn), lambda j, i, kk: (i, j))]
    out_shape = [jax.ShapeDtypeStruct((m, n), out_dtype)]
    if epilogue == "residual_prenorm":
        out_specs += [pl.BlockSpec((bm, bn), lambda j, i, kk: (i, j)),
                      pl.BlockSpec((bm, LANES), lambda j, i, kk: (i, j))]
        out_shape += [jax.ShapeDtypeStruct((m, n), BF16),
                      jax.ShapeDtypeStruct((m, (n // bn) * LANES), F32)]
    if cast_w:
        out_specs.append(pl.BlockSpec(w_blk, w_map))
        out_shape.append(jax.ShapeDtypeStruct((n, k) if nt else (k, n), BF16))
    if side_cast is not None:
        assert k_steps == 1
        m_tiles = m // bm
        side_blk = (side_cast.shape[0] // ((n // bn) * m_tiles), side_cast.shape[1])
        assert side_blk[0] * (n // bn) * m_tiles == side_cast.shape[0]
        side_spec = pl.BlockSpec(side_blk, lambda j, i, kk: (j * m_tiles + i, 0))
        in_specs.append(side_spec)
        args.append(side_cast)
        out_specs.append(side_spec)
        out_shape.append(jax.ShapeDtypeStruct(side_cast.shape, BF16))
    res = pl.pallas_call(
        functools.partial(_gemm_kernel, n_a=len(a_list), nt=nt, cast_w=cast_w,
                          shifted_from=shifted_from, epilogue=epilogue,
                          side_cast=side_cast is not None),
        grid=(n // bn, m // bm, k_steps),
        in_specs=in_specs,
        out_specs=out_specs,
        out_shape=out_shape,
        compiler_params=_params(("parallel", "parallel", "arbitrary")),
        name=name,
    )(*args)
    return list(res)


def _segment_masks(rows, nseg):
    ri = lax.broadcasted_iota(jnp.int32, (rows, rows), 0)
    ci = lax.broadcasted_iota(jnp.int32, (rows, rows), 1)
    if nseg == 1:
        same = None
        causal = ci <= ri
        upper = ri <= ci
    else:
        shift = int(math.log2(rows // nseg))
        same = lax.shift_right_logical(ri, shift) == lax.shift_right_logical(ci, shift)
        causal = (ci <= ri) & same
        upper = (ri <= ci) & same
    return same, causal, upper


def _state_dot(qb, state_bf16, nseg):
    if nseg == 1:
        return jnp.dot(qb, state_bf16[0], preferred_element_type=F32)
    rps = qb.shape[0] // nseg
    assert 2 * rps == BF16_ROWS
    outs = []
    for p in range(nseg // 2):
        qp = qb[p * BF16_ROWS:(p + 1) * BF16_ROWS]
        outs.append(jnp.dot(qp, state_bf16[2 * p], preferred_element_type=F32)[:rps])
        outs.append(jnp.dot(qp, state_bf16[2 * p + 1], preferred_element_type=F32)[rps:])
    return jnp.concatenate(outs, axis=0)


def _segment_lhs(xt, j, nseg):
    if nseg == 1:
        return xt.astype(BF16)
    rps = xt.shape[1] // nseg
    lane = lax.broadcasted_iota(jnp.int32, xt.shape, 1)
    keep = (lane >= j * rps) & (lane < (j + 1) * rps)
    return jnp.where(keep, xt, 0.0).astype(BF16)


def _col_view(ref, rs, off, width, idx):
    return ref.at[rs, off + idx * width:off + (idx + 1) * width]


def _mlstm_jobs(refs, rows, nseg, has_init, nh, nsub, shared_rows):
    if shared_rows:
        p_ref, refs = refs[0], refs[1:]
        cols = [(p_ref, OFF_QM), (p_ref, OFF_KM), (p_ref, OFF_VM), (p_ref, OFF_OM)]
    else:
        cols, refs = [(r, 0) for r in refs[:4]], refs[4:]
    if has_init:
        gc_ref, gr_ref, gh_ref, m0_ref, c0_ref, n0_ref, h_ref, c_ref, n_ref, m_ref = refs
    else:
        gc_ref, gr_ref, gh_ref, h_ref, c_ref, n_ref, m_ref = refs
    (q_ref, q0), (k_ref, k0), (v_ref, v0), (om_ref, om0) = cols
    jobs = []
    for sc in range(nsub):
        rs = slice(sc * rows, (sc + 1) * rows)
        chunk_jobs = []
        for hh in range(nh):
            vv = slice(hh * DV_M, (hh + 1) * DV_M)
            head_refs = [_col_view(q_ref, rs, q0, DK_M, hh), _col_view(k_ref, rs, k0, DK_M, hh),
                         _col_view(v_ref, rs, v0, DV_M, hh), _col_view(om_ref, rs, om0, DV_M, hh),
                         gc_ref.at[rs, :], gr_ref.at[:, rs], gh_ref.at[:, vv]]
            if has_init:
                head_refs += [m0_ref, c0_ref.at[:, hh], n0_ref.at[:, hh]]
            state = [c_ref.at[:, hh], n_ref.at[:, hh], m_ref.at[hh]]
            head_refs += [h_ref.at[rs, vv]] + state
            zero = state if (not has_init and sc == 0) else []
            chunk_jobs.append((zero, functools.partial(
                _mlstm_head, pl.program_id(1) * nh + hh, head_refs, rows, nseg, has_init)))
        jobs.append(chunk_jobs)
    return jobs


def _mlstm_head(head, refs, rows, nseg, has_init):
    if has_init:
        (q_ref, k_ref, v_ref, om_ref, gc_ref, gr_ref, gh_ref, m0_ref, c0_ref, n0_ref,
         h_ref, c_ref, n_ref, m_ref) = refs
    else:
        (q_ref, k_ref, v_ref, om_ref, gc_ref, gr_ref, gh_ref,
         h_ref, c_ref, n_ref, m_ref) = refs
    rps = rows // nseg
    lane = lax.broadcasted_iota(jnp.int32, (rows, LANES), 1)
    sub = lax.broadcasted_iota(jnp.int32, (SUBLANES, rows), 0)

    def pick_col(x_ref, idx):
        return jnp.sum(jnp.where(lane == idx, x_ref[...], 0.0), axis=1, keepdims=True)

    def pick_row(x_ref, idx):
        return jnp.sum(jnp.where(sub == idx, x_ref[...], 0.0), axis=0, keepdims=True)

    if has_init:
        c_prev, n_prev = c0_ref, n0_ref
        m_col = pick_col(m0_ref, head)
    else:
        c_prev, n_prev = c_ref, n_ref
        m_col = m_ref[:, 0:1]

    i_col, f_col = pick_col(gc_ref, head), pick_col(gc_ref, head + HM)
    i_row, f_row = pick_row(gr_ref, head), pick_row(gr_ref, head + HM)

    same, causal, upper = _segment_masks(rows, nseg)
    b_col = jnp.sum(jnp.where(causal, f_row, 0.0), axis=1, keepdims=True)
    b_row = jnp.sum(jnp.where(upper, f_col, 0.0), axis=0, keepdims=True)
    if same is None:
        bl_col = jnp.sum(f_row, axis=1, keepdims=True) + jnp.zeros_like(b_col)
        bl_row = jnp.sum(f_col, axis=0, keepdims=True) + jnp.zeros_like(b_row)
    else:
        bl_col = jnp.sum(jnp.where(same, f_row, 0.0), axis=1, keepdims=True)
        bl_row = jnp.sum(jnp.where(same, f_col, 0.0), axis=0, keepdims=True)

    dlog = jnp.where(causal, b_col - b_row + i_row, -jnp.inf)
    dmax = jnp.max(dlog, axis=1, keepdims=True)
    inter = b_col + m_col
    mt = jnp.maximum(inter, dmax)
    dw = jnp.exp(dlog - mt)
    iw = jnp.exp(inter - mt)

    q = q_ref[...] * (DK_M ** -0.5)
    k = k_ref[...]
    qb, kb, vb = q.astype(BF16), k.astype(BF16), v_ref[...].astype(BF16)
    s = lax.dot_general(qb, kb, (((1,), (1,)), ((), ())), preferred_element_type=F32) * dw
    intra = jnp.dot(s.astype(BF16), vb, preferred_element_type=F32)
    q_c = _state_dot(qb, [c_prev[j].astype(BF16) for j in range(nseg)], nseg)
    n_tok = jnp.broadcast_to(n_prev[...], (nseg, rps, DK_M)).reshape(rows, DK_M)
    q_n = jnp.sum(q * n_tok, axis=1, keepdims=True)
    num = iw * q_c + intra
    den = iw * q_n + jnp.sum(s, axis=1, keepdims=True)
    hh = num * (1.0 / jnp.maximum(jnp.abs(den), jnp.exp(-mt)))
    h_ref[...] = (_rms(hh, gh_ref[...]) * _sigmoid(om_ref[...])).astype(BF16)

    wlog_col = bl_col - b_col + i_col
    wlog_row = bl_row - b_row + i_row
    if same is None:
        wmax = jnp.max(wlog_row, axis=1, keepdims=True) + jnp.zeros_like(b_col)
    else:
        wmax = jnp.max(jnp.where(same, wlog_row, -jnp.inf), axis=1, keepdims=True)
    m_new = jnp.maximum(bl_col + m_col, wmax)
    decay = jnp.exp(bl_col + m_col - m_new)
    kw = k * jnp.exp(wlog_col - m_new)
    kw_t = kw.T
    for j in range(nseg):
        d_j = decay[j * rps:j * rps + 1, :]
        upd = jnp.dot(_segment_lhs(kw_t, j, nseg), vb, preferred_element_type=F32)
        c_ref[j] = d_j * c_prev[j] + upd
        n_ref[j] = d_j * n_prev[j] + jnp.sum(kw[j * rps:(j + 1) * rps], axis=0, keepdims=True)
    m_ref[...] = jnp.broadcast_to(m_new, (rows, LANES))


def _mlstm_parts(proj, gc, gr, g_head, rows, nseg, nchunks, nh, nsub, init=None,
                 shared_rows=False):
    m = proj.shape[0]
    groups = m // (rows * nchunks)
    has_init = init is not None
    assert not has_init or (nchunks == 1 and nsub == 1)
    steps = nchunks // nsub
    brows = rows * nsub
    row = lambda g, h, c: g * steps + c
    qk_blk, v_blk = nh * DK_M, nh * DV_M
    in_specs = [
        pl.BlockSpec((brows, qk_blk), lambda g, h, c: (row(g, h, c), OFF_QM // qk_blk + h)),
        pl.BlockSpec((brows, qk_blk), lambda g, h, c: (row(g, h, c), OFF_KM // qk_blk + h)),
        pl.BlockSpec((brows, v_blk), lambda g, h, c: (row(g, h, c), OFF_VM // v_blk + h)),
        pl.BlockSpec((brows, v_blk), lambda g, h, c: (row(g, h, c), OFF_OM // v_blk + h)),
        pl.BlockSpec((brows, LANES), lambda g, h, c: (row(g, h, c), 0)),
        pl.BlockSpec((SUBLANES, brows), lambda g, h, c: (0, row(g, h, c))),
        pl.BlockSpec((1, v_blk), lambda g, h, c: (0, h)),
    ]
    args = [proj, proj, proj, proj, gc, gr, g_head]
    if shared_rows:
        assert nh == HM
        in_specs, args = in_specs[4:], args[4:]
    if has_init:
        m0_tok, c0, n0 = init
        in_specs += [
            pl.BlockSpec((rows, LANES), lambda g, h, c: (g, 0)),
            pl.BlockSpec((nseg, nh, DK_M, DV_M), lambda g, h, c: (g, h, 0, 0)),
            pl.BlockSpec((nseg, nh, 1, DK_M), lambda g, h, c: (g, h, 0, 0)),
        ]
        args += [m0_tok, c0, n0]
    nb = groups * nseg
    return dict(
        jobs=functools.partial(_mlstm_jobs, rows=rows, nseg=nseg, has_init=has_init, nh=nh,
                               nsub=nsub, shared_rows=shared_rows),
        grid=(groups, HM // nh, steps),
        in_specs=in_specs,
        args=args,
        out_specs=[
            pl.BlockSpec((brows, v_blk), lambda g, h, c: (row(g, h, c), h)),
            pl.BlockSpec((nseg, nh, DK_M, DV_M), lambda g, h, c: (g, h, 0, 0)),
            pl.BlockSpec((nseg, nh, 1, DK_M), lambda g, h, c: (g, h, 0, 0)),
            pl.BlockSpec((None, nh, rows, LANES), lambda g, h, c: (g, h, 0, 0)),
        ],
        out_shape=[
            jax.ShapeDtypeStruct((m, HM * DV_M), BF16),
            jax.ShapeDtypeStruct((nb, HM, DK_M, DV_M), F32),
            jax.ShapeDtypeStruct((nb, HM, 1, DK_M), F32),
            jax.ShapeDtypeStruct((groups, HM, rows, LANES), F32),
        ],
    )


def _regions(chunk_jobs_per_part):
    return [jobs[i:i + HEADS_PER_REGION]
            for jobs in chunk_jobs_per_part for i in range(0, len(jobs), HEADS_PER_REGION)]


def _run_mixers(parts, name, shared=None):
    grid = parts[0]["grid"]
    assert all(p["grid"] == grid for p in parts)
    n_in = [len(p["in_specs"]) for p in parts]
    n_out = [len(p["out_specs"]) for p in parts]
    n_shared = 0 if shared is None else 1

    def body(*refs):
        common, refs = list(refs[:n_shared]), refs[n_shared:]
        ins, outs = refs[:sum(n_in)], refs[sum(n_in):]
        jobs, i, o = [], 0, 0
        for p, ni, no in zip(parts, n_in, n_out):
            jobs.append(p["jobs"](common + list(ins[i:i + ni]) + list(outs[o:o + no])))
            i, o = i + ni, o + no
        for chunk in range(len(jobs[0])):
            for region in _regions([j[chunk] for j in jobs]):
                zero_refs = [r for zero, _ in region for r in zero]
                if zero_refs:
                    @pl.when(pl.program_id(2) == 0)
                    def _zero_state():
                        for ref in zero_refs:
                            ref[...] = jnp.zeros(ref.shape, F32)
                for _, job in region:
                    job()

    res = pl.pallas_call(
        body,
        grid=grid,
        in_specs=[s for _, s in [shared][:n_shared]] + [s for p in parts for s in p["in_specs"]],
        out_specs=[s for p in parts for s in p["out_specs"]],
        out_shape=[s for p in parts for s in p["out_shape"]],
        compiler_params=_params(("parallel", "parallel", "arbitrary")),
        name=name,
    )(*[a for a, _ in [shared][:n_shared]], *[a for p in parts for a in p["args"]])
    outs, o = [], 0
    for no in n_out:
        outs.append(res[o:o + no])
        o += no
    return outs


def _ret_jobs(refs, rows, nseg, has_init, nh, nsub, shared_rows):
    if shared_rows:
        p_ref, refs = refs[0], refs[1:]
        cols = [(p_ref, OFF_QR), (p_ref, OFF_KR), (p_ref, OFF_VR), (p_ref, OFF_GR)]
    else:
        cols, refs = [(r, 0) for r in refs[:4]], refs[4:]
    if has_init:
        cos_ref, sin_ref, dmat_ref, dec_ref, gh_ref, s0_ref, o_ref, s_ref = refs
    else:
        cos_ref, sin_ref, dmat_ref, dec_ref, gh_ref, o_ref, s_ref = refs
    (q_ref, q0), (k_ref, k0), (v_ref, v0), (g_ref, g0) = cols
    jobs = []
    for sc in range(nsub):
        rs = slice(sc * rows, (sc + 1) * rows)
        chunk_jobs = []
        for hh in range(nh):
            vv = slice(hh * DV_R, (hh + 1) * DV_R)
            head_refs = [_col_view(q_ref, rs, q0, DK_R, hh), _col_view(k_ref, rs, k0, DK_R, hh),
                         _col_view(v_ref, rs, v0, DV_R, hh), _col_view(g_ref, rs, g0, DV_R, hh),
                         cos_ref.at[rs, :], sin_ref.at[rs, :], dmat_ref.at[hh], dec_ref.at[hh],
                         gh_ref.at[:, vv]]
            if has_init:
                head_refs.append(s0_ref.at[:, hh])
            head_refs += [o_ref.at[rs, vv], s_ref.at[:, hh]]
            zero = [s_ref.at[:, hh]] if (not has_init and sc == 0) else []
            chunk_jobs.append((zero, functools.partial(_ret_head, head_refs, rows, nseg, has_init)))
        jobs.append(chunk_jobs)
    return jobs


def _ret_head(refs, rows, nseg, has_init):
    if has_init:
        (q_ref, k_ref, v_ref, g_ref, cos_ref, sin_ref, dmat_ref, dec_ref, gh_ref, s0_ref,
         o_ref, s_ref) = refs
        s_prev = s0_ref
    else:
        (q_ref, k_ref, v_ref, g_ref, cos_ref, sin_ref, dmat_ref, dec_ref, gh_ref,
         o_ref, s_ref) = refs
        s_prev = s_ref

    cosf, sins = cos_ref[...], sin_ref[...]
    half = DK_R // 2

    def rot(x):
        return x * cosf + pltpu.roll(x, half, 1) * sins

    qr = rot(q_ref[...])
    kr = rot(k_ref[...]) * (DK_R ** -0.5)
    dec = dec_ref[...]
    inter, kdec, sdec = dec[:, 0:1], dec[:, 1:2], dec[0:1, 2:3]
    qb, kb, vb = qr.astype(BF16), kr.astype(BF16), v_ref[...].astype(BF16)
    s = lax.dot_general(qb, kb, (((1,), (1,)), ((), ())), preferred_element_type=F32) * dmat_ref[...]
    q_s = _state_dot(qb, [s_prev[j].astype(BF16) for j in range(nseg)], nseg)
    o = jnp.dot(s.astype(BF16), vb, preferred_element_type=F32) + inter * q_s
    g = g_ref[...]
    o_ref[...] = (_rms(o, gh_ref[...]) * (g * _sigmoid(g))).astype(BF16)

    kd_t = (kr * kdec).T
    for j in range(nseg):
        upd = jnp.dot(_segment_lhs(kd_t, j, nseg), vb, preferred_element_type=F32)
        s_ref[j] = sdec * s_prev[j] + upd


def _ret_parts(proj, cos_t, sin_t, dmat, dec, g_head, rows, nseg, nchunks, nh, nsub, init=None,
               shared_rows=False):
    m = proj.shape[0]
    groups = m // (rows * nchunks)
    has_init = init is not None
    assert not has_init or (nchunks == 1 and nsub == 1)
    steps = nchunks // nsub
    brows = rows * nsub
    row = lambda g, h, c: g * steps + c
    qk_blk, v_blk = nh * DK_R, nh * DV_R
    in_specs = [
        pl.BlockSpec((brows, qk_blk), lambda g, h, c: (row(g, h, c), OFF_QR // qk_blk + h)),
        pl.BlockSpec((brows, qk_blk), lambda g, h, c: (row(g, h, c), OFF_KR // qk_blk + h)),
        pl.BlockSpec((brows, v_blk), lambda g, h, c: (row(g, h, c), OFF_VR // v_blk + h)),
        pl.BlockSpec((brows, v_blk), lambda g, h, c: (row(g, h, c), OFF_GR // v_blk + h)),
        pl.BlockSpec((brows, DK_R), lambda g, h, c: (c, 0)),
        pl.BlockSpec((brows, DK_R), lambda g, h, c: (c, 0)),
        pl.BlockSpec((nh, rows, rows), lambda g, h, c: (h, 0, 0)),
        pl.BlockSpec((nh, rows, LANES), lambda g, h, c: (h, 0, 0)),
        pl.BlockSpec((1, v_blk), lambda g, h, c: (0, h)),
    ]
    args = [proj, proj, proj, proj, cos_t, sin_t, dmat, dec, g_head]
    if shared_rows:
        assert nh == HR
        in_specs, args = in_specs[4:], args[4:]
    if has_init:
        in_specs.append(pl.BlockSpec((nseg, nh, DK_R, DV_R), lambda g, h, c: (g, h, 0, 0)))
        args.append(init)
    nb = groups * nseg
    return dict(
        jobs=functools.partial(_ret_jobs, rows=rows, nseg=nseg, has_init=has_init, nh=nh,
                               nsub=nsub, shared_rows=shared_rows),
        grid=(groups, HR // nh, steps),
        in_specs=in_specs,
        args=args,
        out_specs=[
            pl.BlockSpec((brows, v_blk), lambda g, h, c: (row(g, h, c), h)),
            pl.BlockSpec((nseg, nh, DK_R, DV_R), lambda g, h, c: (g, h, 0, 0)),
        ],
        out_shape=[
            jax.ShapeDtypeStruct((m, HR * DV_R), BF16),
            jax.ShapeDtypeStruct((nb, HR, DK_R, DV_R), F32),
        ],
    )


def _retention_tables(seg_len, nseg):
    rows = seg_len * nseg
    idx = (np.arange(rows) % seg_len).astype(np.float64)
    seg = np.arange(rows) // seg_len
    lg = np.log(1.0 - 2.0 ** (-5.0 - np.arange(HR, dtype=np.float64)))
    diff = idx[:, None] - idx[None, :]
    ok = (diff >= 0) & (seg[:, None] == seg[None, :])
    dmat = np.where(ok[None], np.exp(np.maximum(diff, 0.0)[None] * lg[:, None, None]), 0.0)
    dec = np.zeros((HR, rows, LANES))
    dec[:, :, 0] = np.exp((idx[None, :] + 1.0) * lg[:, None])
    dec[:, :, 1] = np.exp((seg_len - 1.0 - idx[None, :]) * lg[:, None])
    dec[:, :, 2] = np.exp(seg_len * lg)[:, None]
    return jnp.asarray(dmat, F32), jnp.asarray(dec, F32)


def _rotary_tables(pos):
    freqs = ROPE_BASE ** (-jnp.arange(0, DK_R, 2, dtype=F32) / DK_R)
    ang = pos[:, None] * freqs[None, :]
    cos, sin = jnp.cos(ang), jnp.sin(ang)
    return jnp.concatenate([cos, cos], axis=-1), jnp.concatenate([-sin, sin], axis=-1)


def _rmsnorm_kernel(x_ref, g_ref, o_ref):
    o_ref[...] = _rms(x_ref[...], g_ref[...]).astype(o_ref.dtype)


def _rmsnorm(x2d, g, out_dtype, bm=512):
    m = x2d.shape[0]
    return pl.pallas_call(
        _rmsnorm_kernel,
        grid=(m // bm,),
        in_specs=[
            pl.BlockSpec((bm, D_MODEL), lambda i: (i, 0)),
            pl.BlockSpec((1, D_MODEL), lambda i: (0, 0)),
        ],
        out_specs=pl.BlockSpec((bm, D_MODEL), lambda i: (i, 0)),
        out_shape=jax.ShapeDtypeStruct((m, D_MODEL), out_dtype),
        compiler_params=_params(("parallel",)),
        name="rmsnorm",
    )(x2d, g)


def _layer_front(x2d, w, wb, tables, rows, nseg, nchunks, init):
    cast = wb is None
    assert not cast or x2d.shape[0] == GEMM_BM
    cfg = dict(bm=GEMM_BM, bn=GEMM_BN_CAST if cast else GEMM_BN, cast_w=cast)
    new_wb = {}

    def gemm(key, a_list, **kw):
        res = _gemm(a_list, w[key] if cast else wb[key], name=key, **{**cfg, **kw})
        if cast:
            new_wb[key] = res.pop()
        return res[0] if len(res) == 1 else res

    h, gc, gr = _norm_gates(x2d, w["g_mix"], w["wg"], w["gate_bias"])
    proj = gemm("in_proj", [h], nt=True, shifted=cast)
    if init is None:
        nsub = math.gcd(nchunks, PROMPT_CHUNKS_PER_STEP)
        steps = nchunks // nsub
        proj_rows = (proj, pl.BlockSpec((rows * nsub, N_MAIN), lambda g, h, c: (g * steps + c, 0)))
        (hm, c_new, n_new, m_slab), (hr, s_new) = _run_mixers(
            [_mlstm_parts(proj, gc, gr, w["g_mh"], rows, nseg, nchunks, HM, nsub, shared_rows=True),
             _ret_parts(proj, *tables, w["g_rh"], rows, nseg, nchunks, HR, nsub, shared_rows=True)],
            "mixers", shared=proj_rows)
    else:
        m_init = (init["m_tok"], init["C"], init["n"])
        (hm, c_new, n_new, m_slab), = _run_mixers(
            [_mlstm_parts(proj, gc, gr, w["g_mh"], rows, nseg, nchunks, 1, 1, m_init)], "mlstm")
        (hr, s_new), = _run_mixers(
            [_ret_parts(proj, *tables, w["g_rh"], rows, nseg, nchunks, RET_HEADS_WITH_STATE, 1,
                        init["S"])], "retention")
    x1, x1g, x1_ss = gemm("out_proj", [hm, hr], epilogue="residual_prenorm", x=x2d,
                          norm_g=w["g_ffn"])
    if cast:
        act = gemm("ffn_up", [x1g], epilogue="rownorm_relu2", row_ss=x1_ss, out_dtype=BF16)
        wb_out = new_wb
    else:
        act, w_down = gemm("ffn_up", [x1g], epilogue="rownorm_relu2", row_ss=x1_ss,
                           out_dtype=BF16, side_cast=w["ffn_down"])
        wb_out = {**wb, "ffn_down": w_down}
    return x1, act, (c_new, n_new, m_slab, s_new), wb_out


def _layer_back(x1, act, w_down, g_final, final_norm):
    y, = _gemm([act], w_down, name="ffn_down", bm=GEMM_BM, bn=GEMM_BN, bk=FFN_DOWN_BK,
               epilogue="residual", x=x1)
    return _rmsnorm(y, g_final, F32) if final_norm else y


def kernel(x_prompt, x_sample, state_mlstm_C, state_mlstm_n, state_mlstm_m, state_ret_S, w_in, b_igate, b_fgate, g_mlstm_head, g_ret_head, w_out, g_norm_mix, g_norm_ffn, w_up, w_down, g_final):
    depth = w_in.shape[0]
    batch, seq, _ = x_prompt.shape
    dec_batch, dec_seq, _ = x_sample.shape
    assert seq % PROMPT_CHUNK == 0 and SAMPLE_ROWS % dec_seq == 0
    assert (dec_batch * dec_seq) % SAMPLE_ROWS == 0
    seg_s = SAMPLE_ROWS // dec_seq
    nchunks_p = seq // PROMPT_CHUNK

    cos_p, sin_p = _rotary_tables(jnp.arange(seq, dtype=F32) + 0.0)
    pos_s = jnp.arange(dec_seq, dtype=F32) + float(PAST_LEN)
    cos_s, sin_s = _rotary_tables(jnp.tile(pos_s, seg_s))
    tables_p = (cos_p, sin_p) + _retention_tables(PROMPT_CHUNK, 1)
    tables_s = (cos_s, sin_s) + _retention_tables(dec_seq, seg_s)

    yp = x_prompt.reshape(batch * seq, D_MODEL)
    ys = x_sample.reshape(dec_batch * dec_seq, D_MODEL)
    g_fin = g_final.reshape(1, D_MODEL)
    outs = {k: [] for k in ("pC", "pn", "pm", "pS", "sC", "sn", "sm", "sS")}
    for l in range(depth):
        w_in_t = w_in[l].T
        w_gate_t = w_in_t[GATE_OFF:GATE_OFF + 2 * HM]
        w = {
            "in_proj": w_in_t,
            "wg": jnp.pad(w_gate_t, ((0, LANES - 2 * HM), (0, 0))).astype(BF16),
            "gate_bias": jnp.pad(jnp.concatenate([b_igate[l], b_fgate[l]]), (0, LANES - 2 * HM)).reshape(1, LANES),
            "g_mix": g_norm_mix[l].reshape(1, D_MODEL),
            "g_ffn": g_norm_ffn[l].reshape(1, D_MODEL),
            "g_mh": g_mlstm_head[l].reshape(1, HM * DV_M),
            "g_rh": g_ret_head[l].reshape(1, HR * DV_R),
            "out_proj": w_out[l],
            "ffn_up": w_up[l],
            "ffn_down": w_down[l],
        }
        last = l == depth - 1
        m_tok = jnp.pad(jnp.repeat(state_mlstm_m[l], dec_seq, axis=0), ((0, 0), (0, LANES - HM)))
        init = {"m_tok": m_tok, "C": state_mlstm_C[l],
                "n": state_mlstm_n[l].reshape(dec_batch, HM, 1, DK_M), "S": state_ret_S[l]}
        xs1, act_s, (c, n, m_slab, s), wb = _layer_front(
            ys, w, None, tables_s, SAMPLE_ROWS, seg_s, 1, init)
        outs["sC"].append(c)
        outs["sn"].append(n.reshape(dec_batch, HM, DK_M))
        m_seg = m_slab[:, :, ::dec_seq, 0]
        outs["sm"].append(m_seg.transpose(0, 2, 1).reshape(dec_batch, HM))
        outs["sS"].append(s)

        xp1, act_p, (c, n, m_slab, s), wb = _layer_front(
            yp, w, wb, tables_p, PROMPT_CHUNK, 1, nchunks_p, None)
        outs["pC"].append(c)
        outs["pn"].append(n.reshape(batch, HM, DK_M))
        outs["pm"].append(m_slab[:, :, 0, 0])
        outs["pS"].append(s)

        ys = _layer_back(xs1, act_s, wb["ffn_down"], g_fin, last)
        yp = _layer_back(xp1, act_p, wb["ffn_down"], g_fin, last)

    def stack(xs):
        return xs[0][None] if len(xs) == 1 else jnp.stack(xs)

    return (yp.reshape(batch, seq, D_MODEL), ys.reshape(dec_batch, dec_seq, D_MODEL),
            stack(outs["pC"]), stack(outs["pn"]), stack(outs["pm"]), stack(outs["pS"]),
            stack(outs["sC"]), stack(outs["sn"]), stack(outs["sm"]), stack(outs["sS"]))
```

```python
import functools
import math

import jax
import jax.numpy as jnp
import numpy as np
from jax import lax
from jax.experimental import pallas as pl
from jax.experimental.pallas import tpu as pltpu

F32 = jnp.float32
BF16 = jnp.bfloat16

D_MODEL = 4096
HM, DK_M, DV_M = 4, 256, 512
HR, DK_R, DV_R = 8, 128, 256
PAST_LEN = 16384
GATE_SOFTCAP = 15.0
ROPE_BASE = 10000.0
EPS = 1e-6

LANES = 128
SUBLANES = 8
BF16_ROWS = 16
VMEM_LIMIT = 60 * 1024 * 1024

N_MAIN = 2 * HM * DK_M + 2 * HM * DV_M + 2 * HR * DK_R + 2 * HR * DV_R
GATE_OFF = 2 * HM * DK_M + 2 * HM * DV_M
OFF_QM, OFF_KM, OFF_VM, OFF_OM = 0, HM * DK_M, 2 * HM * DK_M, 2 * HM * DK_M + HM * DV_M
OFF_QR = GATE_OFF
OFF_KR = OFF_QR + HR * DK_R
OFF_VR = OFF_KR + HR * DK_R
OFF_GR = OFF_VR + HR * DV_R

PROMPT_CHUNK = 256
HEADS_PER_REGION = 2
SAMPLE_ROWS = 128
RET_HEADS_WITH_STATE = 4


_NN = (((1,), (0,)), ((), ()))
_NT = (((1,), (1,)), ((), ()))

GEMM_BM = 1024
GEMM_BN = 1024
GEMM_BN_CAST = 512
ACC_CHUNK = 256
FFN_DOWN_BK = 4096


def _params(sem):
    return pltpu.CompilerParams(dimension_semantics=sem, vmem_limit_bytes=VMEM_LIMIT)


def _sigmoid(x):
    return 1.0 / (1.0 + jnp.exp(-x))


def _rms(x, g):
    return x * lax.rsqrt(jnp.mean(x * x, axis=-1, keepdims=True) + EPS) * g


def _norm_gates_kernel(x_ref, g_ref, wg_ref, bias_ref, h_ref, gc_ref, gr_ref):
    hb = _rms(x_ref[...], g_ref[...]).astype(BF16)
    h_ref[...] = hb
    z = lax.dot_general(hb, wg_ref[...], _NT, preferred_element_type=F32) + bias_ref[...]
    zc = GATE_SOFTCAP * jnp.tanh(z / GATE_SOFTCAP)
    logsig = jnp.minimum(zc, 0.0) - jnp.log1p(jnp.exp(-jnp.abs(zc)))
    lane = lax.broadcasted_iota(jnp.int32, zc.shape, 1)
    out = jnp.where(lane >= HM, logsig, zc)
    gc_ref[...] = out
    gr_ref[...] = out.T[:SUBLANES, :]


def _norm_gates(x2d, g, wg, bias, bm=512):
    assert 2 * HM == SUBLANES
    m = x2d.shape[0]
    return pl.pallas_call(
        _norm_gates_kernel,
        grid=(m // bm,),
        in_specs=[
            pl.BlockSpec((bm, D_MODEL), lambda i: (i, 0)),
            pl.BlockSpec((1, D_MODEL), lambda i: (0, 0)),
            pl.BlockSpec((LANES, D_MODEL), lambda i: (0, 0)),
            pl.BlockSpec((1, LANES), lambda i: (0, 0)),
        ],
        out_specs=[
            pl.BlockSpec((bm, D_MODEL), lambda i: (i, 0)),
            pl.BlockSpec((bm, LANES), lambda i: (i, 0)),
            pl.BlockSpec((SUBLANES, bm), lambda i: (0, i)),
        ],
        out_shape=[
            jax.ShapeDtypeStruct((m, D_MODEL), BF16),
            jax.ShapeDtypeStruct((m, LANES), F32),
            jax.ShapeDtypeStruct((SUBLANES, m), F32),
        ],
        compiler_params=_params(("parallel",)),
        name="norm_gates",
    )(x2d, g, wg, bias)


def _gemm_kernel(*refs, n_a, nt, cast_w, shifted_from, epilogue, side_cast):
    a_refs, w_ref = refs[:n_a], refs[n_a]
    rest = list(refs[n_a + 1:])
    w_hi_ref = rest.pop(0) if shifted_from is not None else None
    x_ref = rest.pop(0) if epilogue in ("residual", "residual_prenorm") else None
    g_ref = rest.pop(0) if epilogue == "residual_prenorm" else None
    ss_in_ref = rest.pop(0) if epilogue == "rownorm_relu2" else None
    side_in_ref = rest.pop(0) if side_cast else None
    o_ref = rest.pop(0)
    if epilogue == "residual_prenorm":
        xg_ref, ss_ref = rest.pop(0), rest.pop(0)

    wb_ref = rest.pop(0) if cast_w else None
    if side_cast:
        rest.pop(0)[...] = side_in_ref[...].astype(BF16)

    def product(cols, shift=0):
        w_src = w_ref
        if cast_w:
            w_src = wb_ref
            if nt:
                lo, hi = cols.start + shift, cols.stop + shift
                if hi <= w_ref.shape[0]:
                    blk = w_ref[lo:hi, :]
                else:
                    blk = jnp.concatenate([w_ref[lo:, :], w_hi_ref[...]], axis=0)
                wb_ref[cols, :] = blk.astype(BF16)
            else:
                wb_ref[:, cols] = w_ref[:, cols].astype(BF16)
        acc, off = None, 0
        for a_ref in a_refs:
            ka = a_ref.shape[1]
            wk = w_src[cols, off:off + ka] if nt else w_src[off:off + ka, cols]
            part = lax.dot_general(a_ref[...], wk, _NT if nt else _NN, preferred_element_type=F32)
            acc = part if acc is None else acc + part
            off += ka
        return acc

    if epilogue == "residual":
        def accumulate(base_ref):
            bn = o_ref.shape[1]
            for c in range(0, bn, ACC_CHUNK):
                cols = slice(c, min(c + ACC_CHUNK, bn))
                o_ref[:, cols] = base_ref[:, cols] + product(cols)

        pl.when(pl.program_id(2) == 0)(functools.partial(accumulate, x_ref))
        pl.when(pl.program_id(2) != 0)(functools.partial(accumulate, o_ref))
    elif epilogue == "residual_prenorm":
        bn = o_ref.shape[1]
        ss = jnp.zeros(ss_ref.shape, F32)
        for c in range(0, bn, ACC_CHUNK):
            cols = slice(c, min(c + ACC_CHUNK, bn))
            val = x_ref[:, cols] + product(cols)
            o_ref[:, cols] = val
            xg_ref[:, cols] = (val * g_ref[:, cols]).astype(BF16)
            sq = val * val
            for t in range(0, sq.shape[1], LANES):
                ss = ss + sq[:, t:t + LANES]
        ss_ref[...] = ss
    elif epilogue == "rownorm_relu2":
        mean_sq = jnp.sum(ss_in_ref[...], axis=1, keepdims=True) * (1.0 / D_MODEL)
        scale = lax.rsqrt(mean_sq + EPS)
        bn = o_ref.shape[1]
        for c in range(0, bn, ACC_CHUNK):
            cols = slice(c, min(c + ACC_CHUNK, bn))
            u = product(cols) * scale
            o_ref[:, cols] = jnp.square(jnp.maximum(u, 0.0)).astype(o_ref.dtype)
    else:
        bn = o_ref.shape[1]
        step = ACC_CHUNK if cast_w else bn

        def run(shift):
            for c in range(0, bn, step):
                cols = slice(c, min(c + step, bn))
                o_ref[:, cols] = product(cols, shift)

        if shifted_from is None:
            run(0)
        else:
            pl.when(pl.program_id(0) < shifted_from)(functools.partial(run, 0))
            pl.when(pl.program_id(0) >= shifted_from)(functools.partial(run, SUBLANES))


def _gemm(a_list, w, *, name, bm, bn, bk=None, nt=False, cast_w=False, shifted=False,
          epilogue=None, x=None, norm_g=None, row_ss=None, side_cast=None, out_dtype=F32):
    m = a_list[0].shape[0]
    k = sum(a.shape[1] for a in a_list)
    n = N_MAIN if shifted else (w.shape[0] if nt else w.shape[1])
    bk = k if bk is None else bk
    k_steps = k // bk
    assert len(a_list) == 1 or k_steps == 1
    assert epilogue == "residual" or k_steps == 1
    assert not cast_w or m == bm
    assert epilogue != "rownorm_relu2" or k == D_MODEL
    in_specs = [pl.BlockSpec((bm, a.shape[1] if k_steps == 1 else bk), lambda j, i, kk: (i, kk))
                for a in a_list]
    w_blk = (bn, bk) if nt else (bk, bn)
    w_map = (lambda j, i, kk: (j, kk)) if nt else (lambda j, i, kk: (kk, j))
    in_specs.append(pl.BlockSpec(w_blk, w_map))
    args = [*a_list, w]
    shifted_from = None
    if shifted:
        assert nt and cast_w and k_steps == 1 and GATE_OFF % bn == 0
        shifted_from = GATE_OFF // bn
        in_specs.append(pl.BlockSpec((SUBLANES, bk), lambda j, i, kk: ((j + 1) * (bn // SUBLANES), 0)))
        args.append(w)
    if epilogue in ("residual", "residual_prenorm"):
        in_specs.append(pl.BlockSpec((bm, bn), lambda j, i, kk: (i, j)))
        args.append(x)
    if epilogue == "residual_prenorm":
        in_specs.append(pl.BlockSpec((1, bn), lambda j, i, kk: (0, j)))
        args.append(norm_g)
    if epilogue == "rownorm_relu2":
        in_specs.append(pl.BlockSpec((bm, row_ss.shape[1]), lambda j, i, kk: (i, 0)))
        args.append(row_ss)
    out_specs = [pl.BlockSpec((bm, bn), lambda j, i, kk: (i, j))]
    out_shape = [jax.ShapeDtypeStruct((m, n), out_dtype)]
    if epilogue == "residual_prenorm":
        out_specs += [pl.BlockSpec((bm, bn), lambda j, i, kk: (i, j)),
                      pl.BlockSpec((bm, LANES), lambda j, i, kk: (i, j))]
        out_shape += [jax.ShapeDtypeStruct((m, n), BF16),
                      jax.ShapeDtypeStruct((m, (n // bn) * LANES), F32)]
    if cast_w:
        out_specs.append(pl.BlockSpec(w_blk, w_map))
        out_shape.append(jax.ShapeDtypeStruct((n, k) if nt else (k, n), BF16))
    if side_cast is not None:
        assert k_steps == 1
        m_tiles = m // bm
        side_blk = (side_cast.shape[0] // ((n // bn) * m_tiles), side_cast.shape[1])
        assert side_blk[0] * (n // bn) * m_tiles == side_cast.shape[0]
        side_spec = pl.BlockSpec(side_blk, lambda j, i, kk: (j * m_tiles + i, 0))
        in_specs.append(side_spec)
        args.append(side_cast)
        out_specs.append(side_spec)
        out_shape.append(jax.ShapeDtypeStruct(side_cast.shape, BF16))
    res = pl.pallas_call(
        functools.partial(_gemm_kernel, n_a=len(a_list), nt=nt, cast_w=cast_w,
                          shifted_from=shifted_from, epilogue=epilogue,
                          side_cast=side_cast is not None),
        grid=(n // bn, m // bm, k_steps),
        in_specs=in_specs,
        out_specs=out_specs,
        out_shape=out_shape,
        compiler_params=_params(("parallel", "parallel", "arbitrary")),
        name=name,
    )(*args)
    return list(res)


def _segment_masks(rows, nseg):
    ri = lax.broadcasted_iota(jnp.int32, (rows, rows), 0)
    ci = lax.broadcasted_iota(jnp.int32, (rows, rows), 1)
    if nseg == 1:
        same = None
        causal = ci <= ri
        upper = ri <= ci
    else:
        shift = int(math.log2(rows // nseg))
        same = lax.shift_right_logical(ri, shift) == lax.shift_right_logical(ci, shift)
        causal = (ci <= ri) & same
        upper = (ri <= ci) & same
    return same, causal, upper


def _state_dot(qb, state_bf16, nseg):
    if nseg == 1:
        return jnp.dot(qb, state_bf16[0], preferred_element_type=F32)
    rps = qb.shape[0] // nseg
    assert 2 * rps == BF16_ROWS
    outs = []
    for p in range(nseg // 2):
        qp = qb[p * BF16_ROWS:(p + 1) * BF16_ROWS]
        outs.append(jnp.dot(qp, state_bf16[2 * p], preferred_element_type=F32)[:rps])
        outs.append(jnp.dot(qp, state_bf16[2 * p + 1], preferred_element_type=F32)[rps:])
    return jnp.concatenate(outs, axis=0)


def _segment_lhs(xt, j, nseg):
    if nseg == 1:
        return xt.astype(BF16)
    rps = xt.shape[1] // nseg
    lane = lax.broadcasted_iota(jnp.int32, xt.shape, 1)
    keep = (lane >= j * rps) & (lane < (j + 1) * rps)
    return jnp.where(keep, xt, 0.0).astype(BF16)


def _col_view(ref, off, width, idx):
    return ref.at[:, off + idx * width:off + (idx + 1) * width]


def _mlstm_jobs(refs, rows, nseg, has_init, nh, shared_rows):
    if shared_rows:
        p_ref, refs = refs[0], refs[1:]
        cols = [(p_ref, OFF_QM), (p_ref, OFF_KM), (p_ref, OFF_VM), (p_ref, OFF_OM)]
    else:
        cols, refs = [(r, 0) for r in refs[:4]], refs[4:]
    if has_init:
        gc_ref, gr_ref, gh_ref, m0_ref, c0_ref, n0_ref, h_ref, c_ref, n_ref, m_ref = refs
    else:
        gc_ref, gr_ref, gh_ref, h_ref, c_ref, n_ref, m_ref = refs
    (q_ref, q0), (k_ref, k0), (v_ref, v0), (om_ref, om0) = cols
    jobs = []
    for hh in range(nh):
        vv = slice(hh * DV_M, (hh + 1) * DV_M)
        head_refs = [_col_view(q_ref, q0, DK_M, hh), _col_view(k_ref, k0, DK_M, hh),
                     _col_view(v_ref, v0, DV_M, hh), _col_view(om_ref, om0, DV_M, hh),
                     gc_ref, gr_ref, gh_ref.at[:, vv]]
        if has_init:
            head_refs += [m0_ref, c0_ref.at[:, hh], n0_ref.at[:, hh]]
        state = [c_ref.at[:, hh], n_ref.at[:, hh], m_ref.at[hh]]
        head_refs += [h_ref.at[:, vv]] + state
        jobs.append(([] if has_init else state, functools.partial(
            _mlstm_head, pl.program_id(1) * nh + hh, head_refs, rows, nseg, has_init)))
    return jobs


def _mlstm_head(head, refs, rows, nseg, has_init):
    if has_init:
        (q_ref, k_ref, v_ref, om_ref, gc_ref, gr_ref, gh_ref, m0_ref, c0_ref, n0_ref,
         h_ref, c_ref, n_ref, m_ref) = refs
    else:
        (q_ref, k_ref, v_ref, om_ref, gc_ref, gr_ref, gh_ref,
         h_ref, c_ref, n_ref, m_ref) = refs
    rps = rows // nseg
    lane = lax.broadcasted_iota(jnp.int32, (rows, LANES), 1)
    sub = lax.broadcasted_iota(jnp.int32, (SUBLANES, rows), 0)

    def pick_col(x_ref, idx):
        return jnp.sum(jnp.where(lane == idx, x_ref[...], 0.0), axis=1, keepdims=True)

    def pick_row(x_ref, idx):
        return jnp.sum(jnp.where(sub == idx, x_ref[...], 0.0), axis=0, keepdims=True)

    if has_init:
        c_prev, n_prev = c0_ref, n0_ref
        m_col = pick_col(m0_ref, head)
    else:
        c_prev, n_prev = c_ref, n_ref
        m_col = m_ref[:, 0:1]

    i_col, f_col = pick_col(gc_ref, head), pick_col(gc_ref, head + HM)
    i_row, f_row = pick_row(gr_ref, head), pick_row(gr_ref, head + HM)

    same, causal, upper = _segment_masks(rows, nseg)
    b_col = jnp.sum(jnp.where(causal, f_row, 0.0), axis=1, keepdims=True)
    b_row = jnp.sum(jnp.where(upper, f_col, 0.0), axis=0, keepdims=True)
    if same is None:
        bl_col = jnp.sum(f_row, axis=1, keepdims=True) + jnp.zeros_like(b_col)
        bl_row = jnp.sum(f_col, axis=0, keepdims=True) + jnp.zeros_like(b_row)
    else:
        bl_col = jnp.sum(jnp.where(same, f_row, 0.0), axis=1, keepdims=True)
        bl_row = jnp.sum(jnp.where(same, f_col, 0.0), axis=0, keepdims=True)

    dlog = jnp.where(causal, b_col - b_row + i_row, -jnp.inf)
    dmax = jnp.max(dlog, axis=1, keepdims=True)
    inter = b_col + m_col
    mt = jnp.maximum(inter, dmax)
    dw = jnp.exp(dlog - mt)
    iw = jnp.exp(inter - mt)

    q = q_ref[...] * (DK_M ** -0.5)
    k = k_ref[...]
    qb, kb, vb = q.astype(BF16), k.astype(BF16), v_ref[...].astype(BF16)
    s = lax.dot_general(qb, kb, (((1,), (1,)), ((), ())), preferred_element_type=F32) * dw
    intra = jnp.dot(s.astype(BF16), vb, preferred_element_type=F32)
    q_c = _state_dot(qb, [c_prev[j].astype(BF16) for j in range(nseg)], nseg)
    n_tok = jnp.broadcast_to(n_prev[...], (nseg, rps, DK_M)).reshape(rows, DK_M)
    q_n = jnp.sum(q * n_tok, axis=1, keepdims=True)
    num = iw * q_c + intra
    den = iw * q_n + jnp.sum(s, axis=1, keepdims=True)
    hh = num * (1.0 / jnp.maximum(jnp.abs(den), jnp.exp(-mt)))
    h_ref[...] = (_rms(hh, gh_ref[...]) * _sigmoid(om_ref[...])).astype(BF16)

    wlog_col = bl_col - b_col + i_col
    wlog_row = bl_row - b_row + i_row
    if same is None:
        wmax = jnp.max(wlog_row, axis=1, keepdims=True) + jnp.zeros_like(b_col)
    else:
        wmax = jnp.max(jnp.where(same, wlog_row, -jnp.inf), axis=1, keepdims=True)
    m_new = jnp.maximum(bl_col + m_col, wmax)
    decay = jnp.exp(bl_col + m_col - m_new)
    kw = k * jnp.exp(wlog_col - m_new)
    kw_t = kw.T
    for j in range(nseg):
        d_j = decay[j * rps:j * rps + 1, :]
        upd = jnp.dot(_segment_lhs(kw_t, j, nseg), vb, preferred_element_type=F32)
        c_ref[j] = d_j * c_prev[j] + upd
        n_ref[j] = d_j * n_prev[j] + jnp.sum(kw[j * rps:(j + 1) * rps], axis=0, keepdims=True)
    m_ref[...] = jnp.broadcast_to(m_new, (rows, LANES))


def _mlstm_parts(proj, gc, gr, g_head, rows, nseg, nchunks, nh, init=None, shared_rows=False):
    m = proj.shape[0]
    groups = m // (rows * nchunks)
    has_init = init is not None
    assert not has_init or nchunks == 1
    row = lambda g, h, c: g * nchunks + c
    qk_blk, v_blk = nh * DK_M, nh * DV_M
    in_specs = [
        pl.BlockSpec((rows, qk_blk), lambda g, h, c: (row(g, h, c), OFF_QM // qk_blk + h)),
        pl.BlockSpec((rows, qk_blk), lambda g, h, c: (row(g, h, c), OFF_KM // qk_blk + h)),
        pl.BlockSpec((rows, v_blk), lambda g, h, c: (row(g, h, c), OFF_VM // v_blk + h)),
        pl.BlockSpec((rows, v_blk), lambda g, h, c: (row(g, h, c), OFF_OM // v_blk + h)),
        pl.BlockSpec((rows, LANES), lambda g, h, c: (row(g, h, c), 0)),
        pl.BlockSpec((SUBLANES, rows), lambda g, h, c: (0, row(g, h, c))),
        pl.BlockSpec((1, v_blk), lambda g, h, c: (0, h)),
    ]
    args = [proj, proj, proj, proj, gc, gr, g_head]
    if shared_rows:
        assert nh == HM
        in_specs, args = in_specs[4:], args[4:]
    if has_init:
        m0_tok, c0, n0 = init
        in_specs += [
            pl.BlockSpec((rows, LANES), lambda g, h, c: (g, 0)),
            pl.BlockSpec((nseg, nh, DK_M, DV_M), lambda g, h, c: (g, h, 0, 0)),
            pl.BlockSpec((nseg, nh, 1, DK_M), lambda g, h, c: (g, h, 0, 0)),
        ]
        args += [m0_tok, c0, n0]
    nb = groups * nseg
    return dict(
        jobs=functools.partial(_mlstm_jobs, rows=rows, nseg=nseg, has_init=has_init, nh=nh,
                               shared_rows=shared_rows),
        grid=(groups, HM // nh, nchunks),
        in_specs=in_specs,
        args=args,
        out_specs=[
            pl.BlockSpec((rows, v_blk), lambda g, h, c: (row(g, h, c), h)),
            pl.BlockSpec((nseg, nh, DK_M, DV_M), lambda g, h, c: (g, h, 0, 0)),
            pl.BlockSpec((nseg, nh, 1, DK_M), lambda g, h, c: (g, h, 0, 0)),
            pl.BlockSpec((None, nh, rows, LANES), lambda g, h, c: (g, h, 0, 0)),
        ],
        out_shape=[
            jax.ShapeDtypeStruct((m, HM * DV_M), BF16),
            jax.ShapeDtypeStruct((nb, HM, DK_M, DV_M), F32),
            jax.ShapeDtypeStruct((nb, HM, 1, DK_M), F32),
            jax.ShapeDtypeStruct((groups, HM, rows, LANES), F32),
        ],
    )


def _regions(jobs_per_part):
    return [jobs[i:i + HEADS_PER_REGION]
            for jobs in jobs_per_part for i in range(0, len(jobs), HEADS_PER_REGION)]


def _run_mixers(parts, name, shared=None):
    grid = parts[0]["grid"]
    assert all(p["grid"] == grid for p in parts)
    n_in = [len(p["in_specs"]) for p in parts]
    n_out = [len(p["out_specs"]) for p in parts]
    n_shared = 0 if shared is None else 1

    def body(*refs):
        common, refs = list(refs[:n_shared]), refs[n_shared:]
        ins, outs = refs[:sum(n_in)], refs[sum(n_in):]
        jobs, i, o = [], 0, 0
        for p, ni, no in zip(parts, n_in, n_out):
            jobs.append(p["jobs"](common + list(ins[i:i + ni]) + list(outs[o:o + no])))
            i, o = i + ni, o + no
        for region in _regions(jobs):
            zero_refs = [r for zero, _ in region for r in zero]
            if zero_refs:
                @pl.when(pl.program_id(2) == 0)
                def _zero_state():
                    for ref in zero_refs:
                        ref[...] = jnp.zeros(ref.shape, F32)
            for _, job in region:
                job()

    res = pl.pallas_call(
        body,
        grid=grid,
        in_specs=[s for _, s in [shared][:n_shared]] + [s for p in parts for s in p["in_specs"]],
        out_specs=[s for p in parts for s in p["out_specs"]],
        out_shape=[s for p in parts for s in p["out_shape"]],
        compiler_params=_params(("parallel", "parallel", "arbitrary")),
        name=name,
    )(*[a for a, _ in [shared][:n_shared]], *[a for p in parts for a in p["args"]])
    outs, o = [], 0
    for no in n_out:
        outs.append(res[o:o + no])
        o += no
    return outs


def _ret_jobs(refs, rows, nseg, has_init, nh, shared_rows):
    if shared_rows:
        p_ref, refs = refs[0], refs[1:]
        cols = [(p_ref, OFF_QR), (p_ref, OFF_KR), (p_ref, OFF_VR), (p_ref, OFF_GR)]
    else:
        cols, refs = [(r, 0) for r in refs[:4]], refs[4:]
    if has_init:
        cos_ref, sin_ref, dmat_ref, dec_ref, gh_ref, s0_ref, o_ref, s_ref = refs
    else:
        cos_ref, sin_ref, dmat_ref, dec_ref, gh_ref, o_ref, s_ref = refs
    (q_ref, q0), (k_ref, k0), (v_ref, v0), (g_ref, g0) = cols
    jobs = []
    for hh in range(nh):
        vv = slice(hh * DV_R, (hh + 1) * DV_R)
        head_refs = [_col_view(q_ref, q0, DK_R, hh), _col_view(k_ref, k0, DK_R, hh),
                     _col_view(v_ref, v0, DV_R, hh), _col_view(g_ref, g0, DV_R, hh),
                     cos_ref, sin_ref, dmat_ref.at[hh], dec_ref.at[hh], gh_ref.at[:, vv]]
        if has_init:
            head_refs.append(s0_ref.at[:, hh])
        head_refs += [o_ref.at[:, vv], s_ref.at[:, hh]]
        zero = [] if has_init else [s_ref.at[:, hh]]
        jobs.append((zero, functools.partial(_ret_head, head_refs, rows, nseg, has_init)))
    return jobs


def _ret_head(refs, rows, nseg, has_init):
    if has_init:
        (q_ref, k_ref, v_ref, g_ref, cos_ref, sin_ref, dmat_ref, dec_ref, gh_ref, s0_ref,
         o_ref, s_ref) = refs
        s_prev = s0_ref
    else:
        (q_ref, k_ref, v_ref, g_ref, cos_ref, sin_ref, dmat_ref, dec_ref, gh_ref,
         o_ref, s_ref) = refs
        s_prev = s_ref

    cosf, sins = cos_ref[...], sin_ref[...]
    half = DK_R // 2

    def rot(x):
        return x * cosf + pltpu.roll(x, half, 1) * sins

    qr = rot(q_ref[...])
    kr = rot(k_ref[...]) * (DK_R ** -0.5)
    dec = dec_ref[...]
    inter, kdec, sdec = dec[:, 0:1], dec[:, 1:2], dec[0:1, 2:3]
    qb, kb, vb = qr.astype(BF16), kr.astype(BF16), v_ref[...].astype(BF16)
    s = lax.dot_general(qb, kb, (((1,), (1,)), ((), ())), preferred_element_type=F32) * dmat_ref[...]
    q_s = _state_dot(qb, [s_prev[j].astype(BF16) for j in range(nseg)], nseg)
    o = jnp.dot(s.astype(BF16), vb, preferred_element_type=F32) + inter * q_s
    g = g_ref[...]
    o_ref[...] = (_rms(o, gh_ref[...]) * (g * _sigmoid(g))).astype(BF16)

    kd_t = (kr * kdec).T
    for j in range(nseg):
        upd = jnp.dot(_segment_lhs(kd_t, j, nseg), vb, preferred_element_type=F32)
        s_ref[j] = sdec * s_prev[j] + upd


def _ret_parts(proj, cos_t, sin_t, dmat, dec, g_head, rows, nseg, nchunks, nh, init=None,
               shared_rows=False):
    m = proj.shape[0]
    groups = m // (rows * nchunks)
    has_init = init is not None
    assert not has_init or nchunks == 1
    row = lambda g, h, c: g * nchunks + c
    qk_blk, v_blk = nh * DK_R, nh * DV_R
    in_specs = [
        pl.BlockSpec((rows, qk_blk), lambda g, h, c: (row(g, h, c), OFF_QR // qk_blk + h)),
        pl.BlockSpec((rows, qk_blk), lambda g, h, c: (row(g, h, c), OFF_KR // qk_blk + h)),
        pl.BlockSpec((rows, v_blk), lambda g, h, c: (row(g, h, c), OFF_VR // v_blk + h)),
        pl.BlockSpec((rows, v_blk), lambda g, h, c: (row(g, h, c), OFF_GR // v_blk + h)),
        pl.BlockSpec((rows, DK_R), lambda g, h, c: (c, 0)),
        pl.BlockSpec((rows, DK_R), lambda g, h, c: (c, 0)),
        pl.BlockSpec((nh, rows, rows), lambda g, h, c: (h, 0, 0)),
        pl.BlockSpec((nh, rows, LANES), lambda g, h, c: (h, 0, 0)),
        pl.BlockSpec((1, v_blk), lambda g, h, c: (0, h)),
    ]
    args = [proj, proj, proj, proj, cos_t, sin_t, dmat, dec, g_head]
    if shared_rows:
        assert nh == HR
        in_specs, args = in_specs[4:], args[4:]
    if has_init:
        in_specs.append(pl.BlockSpec((nseg, nh, DK_R, DV_R), lambda g, h, c: (g, h, 0, 0)))
        args.append(init)
    nb = groups * nseg
    return dict(
        jobs=functools.partial(_ret_jobs, rows=rows, nseg=nseg, has_init=has_init, nh=nh,
                               shared_rows=shared_rows),
        grid=(groups, HR // nh, nchunks),
        in_specs=in_specs,
        args=args,
        out_specs=[
            pl.BlockSpec((rows, v_blk), lambda g, h, c: (row(g, h, c), h)),
            pl.BlockSpec((nseg, nh, DK_R, DV_R), lambda g, h, c: (g, h, 0, 0)),
        ],
        out_shape=[
            jax.ShapeDtypeStruct((m, HR * DV_R), BF16),
            jax.ShapeDtypeStruct((nb, HR, DK_R, DV_R), F32),
        ],
    )


def _retention_tables(seg_len, nseg):
    rows = seg_len * nseg
    idx = (np.arange(rows) % seg_len).astype(np.float64)
    seg = np.arange(rows) // seg_len
    lg = np.log(1.0 - 2.0 ** (-5.0 - np.arange(HR, dtype=np.float64)))
    diff = idx[:, None] - idx[None, :]
    ok = (diff >= 0) & (seg[:, None] == seg[None, :])
    dmat = np.where(ok[None], np.exp(np.maximum(diff, 0.0)[None] * lg[:, None, None]), 0.0)
    dec = np.zeros((HR, rows, LANES))
    dec[:, :, 0] = np.exp((idx[None, :] + 1.0) * lg[:, None])
    dec[:, :, 1] = np.exp((seg_len - 1.0 - idx[None, :]) * lg[:, None])
    dec[:, :, 2] = np.exp(seg_len * lg)[:, None]
    return jnp.asarray(dmat, F32), jnp.asarray(dec, F32)


def _rotary_tables(pos):
    freqs = ROPE_BASE ** (-jnp.arange(0, DK_R, 2, dtype=F32) / DK_R)
    ang = pos[:, None] * freqs[None, :]
    cos, sin = jnp.cos(ang), jnp.sin(ang)
    return jnp.concatenate([cos, cos], axis=-1), jnp.concatenate([-sin, sin], axis=-1)


def _rmsnorm_kernel(x_ref, g_ref, o_ref):
    o_ref[...] = _rms(x_ref[...], g_ref[...]).astype(o_ref.dtype)


def _rmsnorm(x2d, g, out_dtype, bm=512):
    m = x2d.shape[0]
    return pl.pallas_call(
        _rmsnorm_kernel,
        grid=(m // bm,),
        in_specs=[
            pl.BlockSpec((bm, D_MODEL), lambda i: (i, 0)),
            pl.BlockSpec((1, D_MODEL), lambda i: (0, 0)),
        ],
        out_specs=pl.BlockSpec((bm, D_MODEL), lambda i: (i, 0)),
        out_shape=jax.ShapeDtypeStruct((m, D_MODEL), out_dtype),
        compiler_params=_params(("parallel",)),
        name="rmsnorm",
    )(x2d, g)


def _layer_front(x2d, w, wb, tables, rows, nseg, nchunks, init):
    cast = wb is None
    assert not cast or x2d.shape[0] == GEMM_BM
    cfg = dict(bm=GEMM_BM, bn=GEMM_BN_CAST if cast else GEMM_BN, cast_w=cast)
    new_wb = {}

    def gemm(key, a_list, **kw):
        res = _gemm(a_list, w[key] if cast else wb[key], name=key, **{**cfg, **kw})
        if cast:
            new_wb[key] = res.pop()
        return res[0] if len(res) == 1 else res

    h, gc, gr = _norm_gates(x2d, w["g_mix"], w["wg"], w["gate_bias"])
    proj = gemm("in_proj", [h], nt=True, shifted=cast)
    if init is None:
        proj_rows = (proj, pl.BlockSpec((rows, N_MAIN), lambda g, h, c: (g * nchunks + c, 0)))
        (hm, c_new, n_new, m_slab), (hr, s_new) = _run_mixers(
            [_mlstm_parts(proj, gc, gr, w["g_mh"], rows, nseg, nchunks, HM, shared_rows=True),
             _ret_parts(proj, *tables, w["g_rh"], rows, nseg, nchunks, HR, shared_rows=True)],
            "mixers", shared=proj_rows)
    else:
        m_init = (init["m_tok"], init["C"], init["n"])
        (hm, c_new, n_new, m_slab), = _run_mixers(
            [_mlstm_parts(proj, gc, gr, w["g_mh"], rows, nseg, nchunks, 1, m_init)], "mlstm")
        (hr, s_new), = _run_mixers(
            [_ret_parts(proj, *tables, w["g_rh"], rows, nseg, nchunks, RET_HEADS_WITH_STATE,
                        init["S"])], "retention")
    x1, x1g, x1_ss = gemm("out_proj", [hm, hr], epilogue="residual_prenorm", x=x2d,
                          norm_g=w["g_ffn"])
    if cast:
        act = gemm("ffn_up", [x1g], epilogue="rownorm_relu2", row_ss=x1_ss, out_dtype=BF16)
        wb_out = new_wb
    else:
        act, w_down = gemm("ffn_up", [x1g], epilogue="rownorm_relu2", row_ss=x1_ss,
                           out_dtype=BF16, side_cast=w["ffn_down"])
        wb_out = {**wb, "ffn_down": w_down}
    return x1, act, (c_new, n_new, m_slab, s_new), wb_out


def _layer_back(x1, act, w_down, g_final, final_norm):
    y, = _gemm([act], w_down, name="ffn_down", bm=GEMM_BM, bn=GEMM_BN, bk=FFN_DOWN_BK,
               epilogue="residual", x=x1)
    return _rmsnorm(y, g_final, F32) if final_norm else y


def kernel(x_prompt, x_sample, state_mlstm_C, state_mlstm_n, state_mlstm_m, state_ret_S, w_in, b_igate, b_fgate, g_mlstm_head, g_ret_head, w_out, g_norm_mix, g_norm_ffn, w_up, w_down, g_final):
    depth = w_in.shape[0]
    batch, seq, _ = x_prompt.shape
    dec_batch, dec_seq, _ = x_sample.shape
    assert seq % PROMPT_CHUNK == 0 and SAMPLE_ROWS % dec_seq == 0
    assert (dec_batch * dec_seq) % SAMPLE_ROWS == 0
    seg_s = SAMPLE_ROWS // dec_seq
    nchunks_p = seq // PROMPT_CHUNK

    cos_p, sin_p = _rotary_tables(jnp.arange(seq, dtype=F32) + 0.0)
    pos_s = jnp.arange(dec_seq, dtype=F32) + float(PAST_LEN)
    cos_s, sin_s = _rotary_tables(jnp.tile(pos_s, seg_s))
    tables_p = (cos_p, sin_p) + _retention_tables(PROMPT_CHUNK, 1)
    tables_s = (cos_s, sin_s) + _retention_tables(dec_seq, seg_s)

    yp = x_prompt.reshape(batch * seq, D_MODEL)
    ys = x_sample.reshape(dec_batch * dec_seq, D_MODEL)
    g_fin = g_final.reshape(1, D_MODEL)
    outs = {k: [] for k in ("pC", "pn", "pm", "pS", "sC", "sn", "sm", "sS")}
    for l in range(depth):
        w_in_t = w_in[l].T
        w_gate_t = w_in_t[GATE_OFF:GATE_OFF + 2 * HM]
        w = {
            "in_proj": w_in_t,
            "wg": jnp.pad(w_gate_t, ((0, LANES - 2 * HM), (0, 0))).astype(BF16),
            "gate_bias": jnp.pad(jnp.concatenate([b_igate[l], b_fgate[l]]), (0, LANES - 2 * HM)).reshape(1, LANES),
            "g_mix": g_norm_mix[l].reshape(1, D_MODEL),
            "g_ffn": g_norm_ffn[l].reshape(1, D_MODEL),
            "g_mh": g_mlstm_head[l].reshape(1, HM * DV_M),
            "g_rh": g_ret_head[l].reshape(1, HR * DV_R),
            "out_proj": w_out[l],
            "ffn_up": w_up[l],
            "ffn_down": w_down[l],
        }
        last = l == depth - 1
        m_tok = jnp.pad(jnp.repeat(state_mlstm_m[l], dec_seq, axis=0), ((0, 0), (0, LANES - HM)))
        init = {"m_tok": m_tok, "C": state_mlstm_C[l],
                "n": state_mlstm_n[l].reshape(dec_batch, HM, 1, DK_M), "S": state_ret_S[l]}
        xs1, act_s, (c, n, m_slab, s), wb = _layer_front(
            ys, w, None, tables_s, SAMPLE_ROWS, seg_s, 1, init)
        outs["sC"].append(c)
        outs["sn"].append(n.reshape(dec_batch, HM, DK_M))
        m_seg = m_slab[:, :, ::dec_seq, 0]
        outs["sm"].append(m_seg.transpose(0, 2, 1).reshape(dec_batch, HM))
        outs["sS"].append(s)

        xp1, act_p, (c, n, m_slab, s), wb = _layer_front(
            yp, w, wb, tables_p, PROMPT_CHUNK, 1, nchunks_p, None)
        outs["pC"].append(c)
        outs["pn"].append(n.reshape(batch, HM, DK_M))
        outs["pm"].append(m_slab[:, :, 0, 0])
        outs["pS"].append(s)

        ys = _layer_back(xs1, act_s, wb["ffn_down"], g_fin, last)
        yp = _layer_back(xp1, act_p, wb["ffn_down"], g_fin, last)

    def stack(xs):
        return xs[0][None] if len(xs) == 1 else jnp.stack(xs)

    return (yp.reshape(batch, seq, D_MODEL), ys.reshape(dec_batch, dec_seq, D_MODEL),
            stack(outs["pC"]), stack(outs["pn"]), stack(outs["pm"]), stack(outs["pS"]),
            stack(outs["sC"]), stack(outs["sn"]), stack(outs["sm"]), stack(outs["sS"]))
```

```python
import functools
import math

import jax
import jax.numpy as jnp
import numpy as np
from jax import lax
from jax.experimental import pallas as pl
from jax.experimental.pallas import tpu as pltpu

F32 = jnp.float32
BF16 = jnp.bfloat16

D_MODEL = 4096
HM, DK_M, DV_M = 4, 256, 512
HR, DK_R, DV_R = 8, 128, 256
PAST_LEN = 16384
GATE_SOFTCAP = 15.0
ROPE_BASE = 10000.0
EPS = 1e-6

LANES = 128
SUBLANES = 8
BF16_ROWS = 16
VMEM_LIMIT = 60 * 1024 * 1024

N_MAIN = 2 * HM * DK_M + 2 * HM * DV_M + 2 * HR * DK_R + 2 * HR * DV_R
GATE_OFF = 2 * HM * DK_M + 2 * HM * DV_M
OFF_QM, OFF_KM, OFF_VM, OFF_OM = 0, HM * DK_M, 2 * HM * DK_M, 2 * HM * DK_M + HM * DV_M
OFF_QR = GATE_OFF
OFF_KR = OFF_QR + HR * DK_R
OFF_VR = OFF_KR + HR * DK_R
OFF_GR = OFF_VR + HR * DV_R

PROMPT_CHUNK = 256
HEADS_PER_REGION = 2
SAMPLE_ROWS = 128
RET_HEADS_WITH_STATE = 4


_NN = (((1,), (0,)), ((), ()))
_NT = (((1,), (1,)), ((), ()))

GEMM_BM = 1024
GEMM_BN = 1024
GEMM_BN_CAST = 512
RING_SLOTS = 3
OUT_PROJ_BN_RING = 256
ACC_CHUNK = 256
FFN_DOWN_BK = 4096


def _params(sem):
    return pltpu.CompilerParams(dimension_semantics=sem, vmem_limit_bytes=VMEM_LIMIT)


def _sigmoid(x):
    return 1.0 / (1.0 + jnp.exp(-x))


def _rms(x, g):
    return x * lax.rsqrt(jnp.mean(x * x, axis=-1, keepdims=True) + EPS) * g


def _norm_gates_kernel(x_ref, g_ref, wg_ref, bias_ref, h_ref, gc_ref, gr_ref):
    hb = _rms(x_ref[...], g_ref[...]).astype(BF16)
    h_ref[...] = hb
    z = lax.dot_general(hb, wg_ref[...], _NT, preferred_element_type=F32) + bias_ref[...]
    zc = GATE_SOFTCAP * jnp.tanh(z / GATE_SOFTCAP)
    logsig = jnp.minimum(zc, 0.0) - jnp.log1p(jnp.exp(-jnp.abs(zc)))
    lane = lax.broadcasted_iota(jnp.int32, zc.shape, 1)
    out = jnp.where(lane >= HM, logsig, zc)
    gc_ref[...] = out
    gr_ref[...] = out.T[:SUBLANES, :]


def _norm_gates(x2d, g, wg, bias, bm=512):
    assert 2 * HM == SUBLANES
    m = x2d.shape[0]
    return pl.pallas_call(
        _norm_gates_kernel,
        grid=(m // bm,),
        in_specs=[
            pl.BlockSpec((bm, D_MODEL), lambda i: (i, 0)),
            pl.BlockSpec((1, D_MODEL), lambda i: (0, 0)),
            pl.BlockSpec((LANES, D_MODEL), lambda i: (0, 0)),
            pl.BlockSpec((1, LANES), lambda i: (0, 0)),
        ],
        out_specs=[
            pl.BlockSpec((bm, D_MODEL), lambda i: (i, 0)),
            pl.BlockSpec((bm, LANES), lambda i: (i, 0)),
            pl.BlockSpec((SUBLANES, bm), lambda i: (0, i)),
        ],
        out_shape=[
            jax.ShapeDtypeStruct((m, D_MODEL), BF16),
            jax.ShapeDtypeStruct((m, LANES), F32),
            jax.ShapeDtypeStruct((SUBLANES, m), F32),
        ],
        compiler_params=_params(("parallel",)),
        name="norm_gates",
    )(x2d, g, wg, bias)


def _gemm_kernel(*refs, n_a, nt, cast_w, shifted_from, epilogue, side_cast, ring):
    a_refs, w_ref = refs[:n_a], refs[n_a]
    rest = list(refs[n_a + 1:])
    w_hi_ref = rest.pop(0) if (shifted_from is not None and not ring) else None
    x_ref = rest.pop(0) if epilogue in ("residual", "residual_prenorm") else None
    g_ref = rest.pop(0) if epilogue == "residual_prenorm" else None
    ss_in_ref = rest.pop(0) if epilogue == "rownorm_relu2" else None
    side_in_ref = rest.pop(0) if side_cast else None
    o_ref = rest.pop(0)
    if epilogue == "residual_prenorm":
        xg_ref, ss_ref = rest.pop(0), rest.pop(0)

    wb_ref = rest.pop(0) if cast_w else None
    if side_cast:
        rest.pop(0)[...] = side_in_ref[...].astype(BF16)

    if ring:
        w_hbm, wbuf_ref, sem_ref = w_ref, rest.pop(0), rest.pop(0)
        tile, n_tiles = pl.program_id(0), pl.num_programs(0)
        bn = o_ref.shape[1]
        shift_tiles_from, shifted_from = shifted_from, None

        def tile_copy(t, slot):
            if nt:
                start = t * bn
                if shift_tiles_from is not None:
                    start = start + jnp.where(t >= shift_tiles_from, SUBLANES, 0)
                src = w_hbm.at[pl.ds(pl.multiple_of(start, SUBLANES), bn), :]
            else:
                src = w_hbm.at[:, pl.ds(pl.multiple_of(t * bn, LANES), bn)]
            return pltpu.make_async_copy(src, wbuf_ref.at[slot], sem_ref.at[slot])

        @pl.when(tile == 0)
        def _prefill():
            for t in range(RING_SLOTS - 1):
                tile_copy(t, t).start()

        ahead = tile + (RING_SLOTS - 1)

        @pl.when(ahead < n_tiles)
        def _prefetch():
            tile_copy(ahead, lax.rem(ahead, RING_SLOTS)).start()

        slot = lax.rem(tile, RING_SLOTS)
        tile_copy(tile, slot).wait()
        w_ref = wbuf_ref.at[slot]

    def product(cols, shift=0):
        w_src = w_ref
        if cast_w:
            w_src = wb_ref
            if nt:
                lo, hi = cols.start + shift, cols.stop + shift
                if hi <= w_ref.shape[0]:
                    blk = w_ref[lo:hi, :]
                else:
                    blk = jnp.concatenate([w_ref[lo:, :], w_hi_ref[...]], axis=0)
                wb_ref[cols, :] = blk.astype(BF16)
            else:
                wb_ref[:, cols] = w_ref[:, cols].astype(BF16)
        acc, off = None, 0
        for a_ref in a_refs:
            ka = a_ref.shape[1]
            wk = w_src[cols, off:off + ka] if nt else w_src[off:off + ka, cols]
            part = lax.dot_general(a_ref[...], wk, _NT if nt else _NN, preferred_element_type=F32)
            acc = part if acc is None else acc + part
            off += ka
        return acc

    if epilogue == "residual":
        def accumulate(base_ref):
            bn = o_ref.shape[1]
            for c in range(0, bn, ACC_CHUNK):
                cols = slice(c, min(c + ACC_CHUNK, bn))
                o_ref[:, cols] = base_ref[:, cols] + product(cols)

        pl.when(pl.program_id(2) == 0)(functools.partial(accumulate, x_ref))
        pl.when(pl.program_id(2) != 0)(functools.partial(accumulate, o_ref))
    elif epilogue == "residual_prenorm":
        bn = o_ref.shape[1]
        ss = jnp.zeros(ss_ref.shape, F32)
        for c in range(0, bn, ACC_CHUNK):
            cols = slice(c, min(c + ACC_CHUNK, bn))
            val = x_ref[:, cols] + product(cols)
            o_ref[:, cols] = val
            xg_ref[:, cols] = (val * g_ref[:, cols]).astype(BF16)
            sq = val * val
            for t in range(0, sq.shape[1], LANES):
                ss = ss + sq[:, t:t + LANES]
        ss_ref[...] = ss
    elif epilogue == "rownorm_relu2":
        mean_sq = jnp.sum(ss_in_ref[...], axis=1, keepdims=True) * (1.0 / D_MODEL)
        scale = lax.rsqrt(mean_sq + EPS)
        bn = o_ref.shape[1]
        for c in range(0, bn, ACC_CHUNK):
            cols = slice(c, min(c + ACC_CHUNK, bn))
            u = product(cols) * scale
            o_ref[:, cols] = jnp.square(jnp.maximum(u, 0.0)).astype(o_ref.dtype)
    else:
        bn = o_ref.shape[1]
        step = ACC_CHUNK if cast_w else bn

        def run(shift):
            for c in range(0, bn, step):
                cols = slice(c, min(c + step, bn))
                o_ref[:, cols] = product(cols, shift)

        if shifted_from is None:
            run(0)
        else:
            pl.when(pl.program_id(0) < shifted_from)(functools.partial(run, 0))
            pl.when(pl.program_id(0) >= shifted_from)(functools.partial(run, SUBLANES))


def _gemm(a_list, w, *, name, bm, bn, bk=None, nt=False, cast_w=False, shifted=False,
          epilogue=None, x=None, norm_g=None, row_ss=None, side_cast=None, ring=False,
          out_dtype=F32):
    m = a_list[0].shape[0]
    k = sum(a.shape[1] for a in a_list)
    n = N_MAIN if shifted else (w.shape[0] if nt else w.shape[1])
    bk = k if bk is None else bk
    k_steps = k // bk
    assert len(a_list) == 1 or k_steps == 1
    assert epilogue == "residual" or k_steps == 1
    assert not cast_w or m == bm
    assert epilogue != "rownorm_relu2" or k == D_MODEL
    in_specs = [pl.BlockSpec((bm, a.shape[1] if k_steps == 1 else bk), lambda j, i, kk: (i, kk))
                for a in a_list]
    w_blk = (bn, bk) if nt else (bk, bn)
    w_map = (lambda j, i, kk: (j, kk)) if nt else (lambda j, i, kk: (kk, j))
    scratch = []
    if ring:
        assert cast_w and k_steps == 1 and n // bn >= RING_SLOTS
        in_specs.append(pl.BlockSpec(memory_space=pl.ANY))
        scratch = [pltpu.VMEM((RING_SLOTS,) + w_blk, F32), pltpu.SemaphoreType.DMA((RING_SLOTS,))]
    else:
        in_specs.append(pl.BlockSpec(w_blk, w_map))
    args = [*a_list, w]
    shifted_from = None
    if shifted:
        assert nt and cast_w and k_steps == 1 and GATE_OFF % bn == 0
        shifted_from = GATE_OFF // bn
        if not ring:
            in_specs.append(pl.BlockSpec((SUBLANES, bk), lambda j, i, kk: ((j + 1) * (bn // SUBLANES), 0)))
            args.append(w)
    if epilogue in ("residual", "residual_prenorm"):
        in_specs.append(pl.BlockSpec((bm, bn), lambda j, i, kk: (i, j)))
        args.append(x)
    if epilogue == "residual_prenorm":
        in_specs.append(pl.BlockSpec((1, bn), lambda j, i, kk: (0, j)))
        args.append(norm_g)
    if epilogue == "rownorm_relu2":
        in_specs.append(pl.BlockSpec((bm, row_ss.shape[1]), lambda j, i, kk: (i, 0)))
        args.append(row_ss)
    out_specs = [pl.BlockSpec((bm, bn), lambda j, i, kk: (i, j))]
    out_shape = [jax.ShapeDtypeStruct((m, n), out_dtype)]
    if epilogue == "residual_prenorm":
        out_specs += [pl.BlockSpec((bm, bn), lambda j, i, kk: (i, j)),
                      pl.BlockSpec((bm, LANES), lambda j, i, kk: (i, j))]
        out_shape += [jax.ShapeDtypeStruct((m, n), BF16),
                      jax.ShapeDtypeStruct((m, (n // bn) * LANES), F32)]
    if cast_w:
        out_specs.append(pl.BlockSpec(w_blk, w_map))
        out_shape.append(jax.ShapeDtypeStruct((n, k) if nt else (k, n), BF16))
    if side_cast is not None:
        assert k_steps == 1
        m_tiles = m // bm
        side_blk = (side_cast.shape[0] // ((n // bn) * m_tiles), side_cast.shape[1])
        assert side_blk[0] * (n // bn) * m_tiles == side_cast.shape[0]
        side_spec = pl.BlockSpec(side_blk, lambda j, i, kk: (j * m_tiles + i, 0))
        in_specs.append(side_spec)
        args.append(side_cast)
        out_specs.append(side_spec)
        out_shape.append(jax.ShapeDtypeStruct(side_cast.shape, BF16))
    res = pl.pallas_call(
        functools.partial(_gemm_kernel, n_a=len(a_list), nt=nt, cast_w=cast_w,
                          shifted_from=shifted_from, epilogue=epilogue,
                          side_cast=side_cast is not None, ring=ring),
        grid=(n // bn, m // bm, k_steps),
        in_specs=in_specs,
        out_specs=out_specs,
        out_shape=out_shape,
        scratch_shapes=scratch,
        compiler_params=_params(("arbitrary" if ring else "parallel", "parallel", "arbitrary")),
        name=name,
    )(*args)
    return list(res)


def _segment_masks(rows, nseg):
    ri = lax.broadcasted_iota(jnp.int32, (rows, rows), 0)
    ci = lax.broadcasted_iota(jnp.int32, (rows, rows), 1)
    if nseg == 1:
        same = None
        causal = ci <= ri
        upper = ri <= ci
    else:
        shift = int(math.log2(rows // nseg))
        same = lax.shift_right_logical(ri, shift) == lax.shift_right_logical(ci, shift)
        causal = (ci <= ri) & same
        upper = (ri <= ci) & same
    return same, causal, upper


def _state_dot(qb, state_bf16, nseg):
    if nseg == 1:
        return jnp.dot(qb, state_bf16[0], preferred_element_type=F32)
    rps = qb.shape[0] // nseg
    assert 2 * rps == BF16_ROWS
    outs = []
    for p in range(nseg // 2):
        qp = qb[p * BF16_ROWS:(p + 1) * BF16_ROWS]
        outs.append(jnp.dot(qp, state_bf16[2 * p], preferred_element_type=F32)[:rps])
        outs.append(jnp.dot(qp, state_bf16[2 * p + 1], preferred_element_type=F32)[rps:])
    return jnp.concatenate(outs, axis=0)


def _segment_lhs(xt, j, nseg):
    if nseg == 1:
        return xt.astype(BF16)
    rps = xt.shape[1] // nseg
    lane = lax.broadcasted_iota(jnp.int32, xt.shape, 1)
    keep = (lane >= j * rps) & (lane < (j + 1) * rps)
    return jnp.where(keep, xt, 0.0).astype(BF16)


def _col_view(ref, off, width, idx):
    return ref.at[:, off + idx * width:off + (idx + 1) * width]


def _mlstm_jobs(refs, rows, nseg, has_init, nh, shared_rows):
    if shared_rows:
        p_ref, refs = refs[0], refs[1:]
        cols = [(p_ref, OFF_QM), (p_ref, OFF_KM), (p_ref, OFF_VM), (p_ref, OFF_OM)]
    else:
        cols, refs = [(r, 0) for r in refs[:4]], refs[4:]
    if has_init:
        gc_ref, gr_ref, gh_ref, m0_ref, c0_ref, n0_ref, h_ref, c_ref, n_ref, m_ref = refs
    else:
        gc_ref, gr_ref, gh_ref, h_ref, c_ref, n_ref, m_ref = refs
    (q_ref, q0), (k_ref, k0), (v_ref, v0), (om_ref, om0) = cols
    jobs = []
    for hh in range(nh):
        vv = slice(hh * DV_M, (hh + 1) * DV_M)
        head_refs = [_col_view(q_ref, q0, DK_M, hh), _col_view(k_ref, k0, DK_M, hh),
                     _col_view(v_ref, v0, DV_M, hh), _col_view(om_ref, om0, DV_M, hh),
                     gc_ref, gr_ref, gh_ref.at[:, vv]]
        if has_init:
            head_refs += [m0_ref, c0_ref.at[:, hh], n0_ref.at[:, hh]]
        state = [c_ref.at[:, hh], n_ref.at[:, hh], m_ref.at[hh]]
        head_refs += [h_ref.at[:, vv]] + state
        jobs.append(([] if has_init else state, functools.partial(
            _mlstm_head, pl.program_id(1) * nh + hh, head_refs, rows, nseg, has_init)))
    return jobs


def _mlstm_head(head, refs, rows, nseg, has_init):
    if has_init:
        (q_ref, k_ref, v_ref, om_ref, gc_ref, gr_ref, gh_ref, m0_ref, c0_ref, n0_ref,
         h_ref, c_ref, n_ref, m_ref) = refs
    else:
        (q_ref, k_ref, v_ref, om_ref, gc_ref, gr_ref, gh_ref,
         h_ref, c_ref, n_ref, m_ref) = refs
    rps = rows // nseg
    lane = lax.broadcasted_iota(jnp.int32, (rows, LANES), 1)
    sub = lax.broadcasted_iota(jnp.int32, (SUBLANES, rows), 0)

    def pick_col(x_ref, idx):
        return jnp.sum(jnp.where(lane == idx, x_ref[...], 0.0), axis=1, keepdims=True)

    def pick_row(x_ref, idx):
        return jnp.sum(jnp.where(sub == idx, x_ref[...], 0.0), axis=0, keepdims=True)

    if has_init:
        c_prev, n_prev = c0_ref, n0_ref
        m_col = pick_col(m0_ref, head)
    else:
        c_prev, n_prev = c_ref, n_ref
        m_col = m_ref[:, 0:1]

    i_col, f_col = pick_col(gc_ref, head), pick_col(gc_ref, head + HM)
    i_row, f_row = pick_row(gr_ref, head), pick_row(gr_ref, head + HM)

    same, causal, upper = _segment_masks(rows, nseg)
    b_col = jnp.sum(jnp.where(causal, f_row, 0.0), axis=1, keepdims=True)
    b_row = jnp.sum(jnp.where(upper, f_col, 0.0), axis=0, keepdims=True)
    if same is None:
        bl_col = jnp.sum(f_row, axis=1, keepdims=True) + jnp.zeros_like(b_col)
        bl_row = jnp.sum(f_col, axis=0, keepdims=True) + jnp.zeros_like(b_row)
    else:
        bl_col = jnp.sum(jnp.where(same, f_row, 0.0), axis=1, keepdims=True)
        bl_row = jnp.sum(jnp.where(same, f_col, 0.0), axis=0, keepdims=True)

    dlog = jnp.where(causal, b_col - b_row + i_row, -jnp.inf)
    dmax = jnp.max(dlog, axis=1, keepdims=True)
    inter = b_col + m_col
    mt = jnp.maximum(inter, dmax)
    dw = jnp.exp(dlog - mt)
    iw = jnp.exp(inter - mt)

    q = q_ref[...] * (DK_M ** -0.5)
    k = k_ref[...]
    qb, kb, vb = q.astype(BF16), k.astype(BF16), v_ref[...].astype(BF16)
    s = lax.dot_general(qb, kb, (((1,), (1,)), ((), ())), preferred_element_type=F32) * dw
    intra = jnp.dot(s.astype(BF16), vb, preferred_element_type=F32)
    q_c = _state_dot(qb, [c_prev[j].astype(BF16) for j in range(nseg)], nseg)
    n_tok = jnp.broadcast_to(n_prev[...], (nseg, rps, DK_M)).reshape(rows, DK_M)
    q_n = jnp.sum(q * n_tok, axis=1, keepdims=True)
    num = iw * q_c + intra
    den = iw * q_n + jnp.sum(s, axis=1, keepdims=True)
    hh = num * (1.0 / jnp.maximum(jnp.abs(den), jnp.exp(-mt)))
    h_ref[...] = (_rms(hh, gh_ref[...]) * _sigmoid(om_ref[...])).astype(BF16)

    wlog_col = bl_col - b_col + i_col
    wlog_row = bl_row - b_row + i_row
    if same is None:
        wmax = jnp.max(wlog_row, axis=1, keepdims=True) + jnp.zeros_like(b_col)
    else:
        wmax = jnp.max(jnp.where(same, wlog_row, -jnp.inf), axis=1, keepdims=True)
    m_new = jnp.maximum(bl_col + m_col, wmax)
    decay = jnp.exp(bl_col + m_col - m_new)
    kw = k * jnp.exp(wlog_col - m_new)
    kw_t = kw.T
    for j in range(nseg):
        d_j = decay[j * rps:j * rps + 1, :]
        upd = jnp.dot(_segment_lhs(kw_t, j, nseg), vb, preferred_element_type=F32)
        c_ref[j] = d_j * c_prev[j] + upd
        n_ref[j] = d_j * n_prev[j] + jnp.sum(kw[j * rps:(j + 1) * rps], axis=0, keepdims=True)
    m_ref[...] = jnp.broadcast_to(m_new, (rows, LANES))


def _mlstm_parts(proj, gc, gr, g_head, rows, nseg, nchunks, nh, init=None, shared_rows=False):
    m = proj.shape[0]
    groups = m // (rows * nchunks)
    has_init = init is not None
    assert not has_init or nchunks == 1
    row = lambda g, h, c: g * nchunks + c
    qk_blk, v_blk = nh * DK_M, nh * DV_M
    in_specs = [
        pl.BlockSpec((rows, qk_blk), lambda g, h, c: (row(g, h, c), OFF_QM // qk_blk + h)),
        pl.BlockSpec((rows, qk_blk), lambda g, h, c: (row(g, h, c), OFF_KM // qk_blk + h)),
        pl.BlockSpec((rows, v_blk), lambda g, h, c: (row(g, h, c), OFF_VM // v_blk + h)),
        pl.BlockSpec((rows, v_blk), lambda g, h, c: (row(g, h, c), OFF_OM // v_blk + h)),
        pl.BlockSpec((rows, LANES), lambda g, h, c: (row(g, h, c), 0)),
        pl.BlockSpec((SUBLANES, rows), lambda g, h, c: (0, row(g, h, c))),
        pl.BlockSpec((1, v_blk), lambda g, h, c: (0, h)),
    ]
    args = [proj, proj, proj, proj, gc, gr, g_head]
    if shared_rows:
        assert nh == HM
        in_specs, args = in_specs[4:], args[4:]
    if has_init:
        m0_tok, c0, n0 = init
        in_specs += [
            pl.BlockSpec((rows, LANES), lambda g, h, c: (g, 0)),
            pl.BlockSpec((nseg, nh, DK_M, DV_M), lambda g, h, c: (g, h, 0, 0)),
            pl.BlockSpec((nseg, nh, 1, DK_M), lambda g, h, c: (g, h, 0, 0)),
        ]
        args += [m0_tok, c0, n0]
    nb = groups * nseg
    return dict(
        jobs=functools.partial(_mlstm_jobs, rows=rows, nseg=nseg, has_init=has_init, nh=nh,
                               shared_rows=shared_rows),
        grid=(groups, HM // nh, nchunks),
        in_specs=in_specs,
        args=args,
        out_specs=[
            pl.BlockSpec((rows, v_blk), lambda g, h, c: (row(g, h, c), h)),
            pl.BlockSpec((nseg, nh, DK_M, DV_M), lambda g, h, c: (g, h, 0, 0)),
            pl.BlockSpec((nseg, nh, 1, DK_M), lambda g, h, c: (g, h, 0, 0)),
            pl.BlockSpec((None, nh, rows, LANES), lambda g, h, c: (g, h, 0, 0)),
        ],
        out_shape=[
            jax.ShapeDtypeStruct((m, HM * DV_M), BF16),
            jax.ShapeDtypeStruct((nb, HM, DK_M, DV_M), F32),
            jax.ShapeDtypeStruct((nb, HM, 1, DK_M), F32),
            jax.ShapeDtypeStruct((groups, HM, rows, LANES), F32),
        ],
    )


def _regions(jobs_per_part):
    return [jobs[i:i + HEADS_PER_REGION]
            for jobs in jobs_per_part for i in range(0, len(jobs), HEADS_PER_REGION)]


def _run_mixers(parts, name, shared=None):
    grid = parts[0]["grid"]
    assert all(p["grid"] == grid for p in parts)
    n_in = [len(p["in_specs"]) for p in parts]
    n_out = [len(p["out_specs"]) for p in parts]
    n_shared = 0 if shared is None else 1

    def body(*refs):
        common, refs = list(refs[:n_shared]), refs[n_shared:]
        ins, outs = refs[:sum(n_in)], refs[sum(n_in):]
        jobs, i, o = [], 0, 0
        for p, ni, no in zip(parts, n_in, n_out):
            jobs.append(p["jobs"](common + list(ins[i:i + ni]) + list(outs[o:o + no])))
            i, o = i + ni, o + no
        for region in _regions(jobs):
            zero_refs = [r for zero, _ in region for r in zero]
            if zero_refs:
                @pl.when(pl.program_id(2) == 0)
                def _zero_state():
                    for ref in zero_refs:
                        ref[...] = jnp.zeros(ref.shape, F32)
            for _, job in region:
                job()

    res = pl.pallas_call(
        body,
        grid=grid,
        in_specs=[s for _, s in [shared][:n_shared]] + [s for p in parts for s in p["in_specs"]],
        out_specs=[s for p in parts for s in p["out_specs"]],
        out_shape=[s for p in parts for s in p["out_shape"]],
        compiler_params=_params(("parallel", "parallel", "arbitrary")),
        name=name,
    )(*[a for a, _ in [shared][:n_shared]], *[a for p in parts for a in p["args"]])
    outs, o = [], 0
    for no in n_out:
        outs.append(res[o:o + no])
        o += no
    return outs


def _ret_jobs(refs, rows, nseg, has_init, nh, shared_rows):
    if shared_rows:
        p_ref, refs = refs[0], refs[1:]
        cols = [(p_ref, OFF_QR), (p_ref, OFF_KR), (p_ref, OFF_VR), (p_ref, OFF_GR)]
    else:
        cols, refs = [(r, 0) for r in refs[:4]], refs[4:]
    if has_init:
        cos_ref, sin_ref, dmat_ref, dec_ref, gh_ref, s0_ref, o_ref, s_ref = refs
    else:
        cos_ref, sin_ref, dmat_ref, dec_ref, gh_ref, o_ref, s_ref = refs
    (q_ref, q0), (k_ref, k0), (v_ref, v0), (g_ref, g0) = cols
    jobs = []
    for hh in range(nh):
        vv = slice(hh * DV_R, (hh + 1) * DV_R)
        head_refs = [_col_view(q_ref, q0, DK_R, hh), _col_view(k_ref, k0, DK_R, hh),
                     _col_view(v_ref, v0, DV_R, hh), _col_view(g_ref, g0, DV_R, hh),
                     cos_ref, sin_ref, dmat_ref.at[hh], dec_ref.at[hh], gh_ref.at[:, vv]]
        if has_init:
            head_refs.append(s0_ref.at[:, hh])
        head_refs += [o_ref.at[:, vv], s_ref.at[:, hh]]
        zero = [] if has_init else [s_ref.at[:, hh]]
        jobs.append((zero, functools.partial(_ret_head, head_refs, rows, nseg, has_init)))
    return jobs


def _ret_head(refs, rows, nseg, has_init):
    if has_init:
        (q_ref, k_ref, v_ref, g_ref, cos_ref, sin_ref, dmat_ref, dec_ref, gh_ref, s0_ref,
         o_ref, s_ref) = refs
        s_prev = s0_ref
    else:
        (q_ref, k_ref, v_ref, g_ref, cos_ref, sin_ref, dmat_ref, dec_ref, gh_ref,
         o_ref, s_ref) = refs
        s_prev = s_ref

    cosf, sins = cos_ref[...], sin_ref[...]
    half = DK_R // 2

    def rot(x):
        return x * cosf + pltpu.roll(x, half, 1) * sins

    qr = rot(q_ref[...])
    kr = rot(k_ref[...]) * (DK_R ** -0.5)
    dec = dec_ref[...]
    inter, kdec, sdec = dec[:, 0:1], dec[:, 1:2], dec[0:1, 2:3]
    qb, kb, vb = qr.astype(BF16), kr.astype(BF16), v_ref[...].astype(BF16)
    s = lax.dot_general(qb, kb, (((1,), (1,)), ((), ())), preferred_element_type=F32) * dmat_ref[...]
    q_s = _state_dot(qb, [s_prev[j].astype(BF16) for j in range(nseg)], nseg)
    o = jnp.dot(s.astype(BF16), vb, preferred_element_type=F32) + inter * q_s
    g = g_ref[...]
    o_ref[...] = (_rms(o, gh_ref[...]) * (g * _sigmoid(g))).astype(BF16)

    kd_t = (kr * kdec).T
    for j in range(nseg):
        upd = jnp.dot(_segment_lhs(kd_t, j, nseg), vb, preferred_element_type=F32)
        s_ref[j] = sdec * s_prev[j] + upd


def _ret_parts(proj, cos_t, sin_t, dmat, dec, g_head, rows, nseg, nchunks, nh, init=None,
               shared_rows=False):
    m = proj.shape[0]
    groups = m // (rows * nchunks)
    has_init = init is not None
    assert not has_init or nchunks == 1
    row = lambda g, h, c: g * nchunks + c
    qk_blk, v_blk = nh * DK_R, nh * DV_R
    in_specs = [
        pl.BlockSpec((rows, qk_blk), lambda g, h, c: (row(g, h, c), OFF_QR // qk_blk + h)),
        pl.BlockSpec((rows, qk_blk), lambda g, h, c: (row(g, h, c), OFF_KR // qk_blk + h)),
        pl.BlockSpec((rows, v_blk), lambda g, h, c: (row(g, h, c), OFF_VR // v_blk + h)),
        pl.BlockSpec((rows, v_blk), lambda g, h, c: (row(g, h, c), OFF_GR // v_blk + h)),
        pl.BlockSpec((rows, DK_R), lambda g, h, c: (c, 0)),
        pl.BlockSpec((rows, DK_R), lambda g, h, c: (c, 0)),
        pl.BlockSpec((nh, rows, rows), lambda g, h, c: (h, 0, 0)),
        pl.BlockSpec((nh, rows, LANES), lambda g, h, c: (h, 0, 0)),
        pl.BlockSpec((1, v_blk), lambda g, h, c: (0, h)),
    ]
    args = [proj, proj, proj, proj, cos_t, sin_t, dmat, dec, g_head]
    if shared_rows:
        assert nh == HR
        in_specs, args = in_specs[4:], args[4:]
    if has_init:
        in_specs.append(pl.BlockSpec((nseg, nh, DK_R, DV_R), lambda g, h, c: (g, h, 0, 0)))
        args.append(init)
    nb = groups * nseg
    return dict(
        jobs=functools.partial(_ret_jobs, rows=rows, nseg=nseg, has_init=has_init, nh=nh,
                               shared_rows=shared_rows),
        grid=(groups, HR // nh, nchunks),
        in_specs=in_specs,
        args=args,
        out_specs=[
            pl.BlockSpec((rows, v_blk), lambda g, h, c: (row(g, h, c), h)),
            pl.BlockSpec((nseg, nh, DK_R, DV_R), lambda g, h, c: (g, h, 0, 0)),
        ],
        out_shape=[
            jax.ShapeDtypeStruct((m, HR * DV_R), BF16),
            jax.ShapeDtypeStruct((nb, HR, DK_R, DV_R), F32),
        ],
    )


def _retention_tables(seg_len, nseg):
    rows = seg_len * nseg
    idx = (np.arange(rows) % seg_len).astype(np.float64)
    seg = np.arange(rows) // seg_len
    lg = np.log(1.0 - 2.0 ** (-5.0 - np.arange(HR, dtype=np.float64)))
    diff = idx[:, None] - idx[None, :]
    ok = (diff >= 0) & (seg[:, None] == seg[None, :])
    dmat = np.where(ok[None], np.exp(np.maximum(diff, 0.0)[None] * lg[:, None, None]), 0.0)
    dec = np.zeros((HR, rows, LANES))
    dec[:, :, 0] = np.exp((idx[None, :] + 1.0) * lg[:, None])
    dec[:, :, 1] = np.exp((seg_len - 1.0 - idx[None, :]) * lg[:, None])
    dec[:, :, 2] = np.exp(seg_len * lg)[:, None]
    return jnp.asarray(dmat, F32), jnp.asarray(dec, F32)


def _rotary_tables(pos):
    freqs = ROPE_BASE ** (-jnp.arange(0, DK_R, 2, dtype=F32) / DK_R)
    ang = pos[:, None] * freqs[None, :]
    cos, sin = jnp.cos(ang), jnp.sin(ang)
    return jnp.concatenate([cos, cos], axis=-1), jnp.concatenate([-sin, sin], axis=-1)


def _rmsnorm_kernel(x_ref, g_ref, o_ref):
    o_ref[...] = _rms(x_ref[...], g_ref[...]).astype(o_ref.dtype)


def _rmsnorm(x2d, g, out_dtype, bm=512):
    m = x2d.shape[0]
    return pl.pallas_call(
        _rmsnorm_kernel,
        grid=(m // bm,),
        in_specs=[
            pl.BlockSpec((bm, D_MODEL), lambda i: (i, 0)),
            pl.BlockSpec((1, D_MODEL), lambda i: (0, 0)),
        ],
        out_specs=pl.BlockSpec((bm, D_MODEL), lambda i: (i, 0)),
        out_shape=jax.ShapeDtypeStruct((m, D_MODEL), out_dtype),
        compiler_params=_params(("parallel",)),
        name="rmsnorm",
    )(x2d, g)


def _layer_front(x2d, w, wb, tables, rows, nseg, nchunks, init):
    cast = wb is None
    assert not cast or x2d.shape[0] == GEMM_BM
    cfg = dict(bm=GEMM_BM, bn=GEMM_BN_CAST if cast else GEMM_BN, cast_w=cast)
    new_wb = {}

    def gemm(key, a_list, **kw):
        res = _gemm(a_list, w[key] if cast else wb[key], name=key, **{**cfg, **kw})
        if cast:
            new_wb[key] = res.pop()
        return res[0] if len(res) == 1 else res

    h, gc, gr = _norm_gates(x2d, w["g_mix"], w["wg"], w["gate_bias"])
    proj = gemm("in_proj", [h], nt=True, shifted=cast, ring=cast)
    if init is None:
        proj_rows = (proj, pl.BlockSpec((rows, N_MAIN), lambda g, h, c: (g * nchunks + c, 0)))
        (hm, c_new, n_new, m_slab), (hr, s_new) = _run_mixers(
            [_mlstm_parts(proj, gc, gr, w["g_mh"], rows, nseg, nchunks, HM, shared_rows=True),
             _ret_parts(proj, *tables, w["g_rh"], rows, nseg, nchunks, HR, shared_rows=True)],
            "mixers", shared=proj_rows)
    else:
        m_init = (init["m_tok"], init["C"], init["n"])
        (hm, c_new, n_new, m_slab), = _run_mixers(
            [_mlstm_parts(proj, gc, gr, w["g_mh"], rows, nseg, nchunks, 1, m_init)], "mlstm")
        (hr, s_new), = _run_mixers(
            [_ret_parts(proj, *tables, w["g_rh"], rows, nseg, nchunks, RET_HEADS_WITH_STATE,
                        init["S"])], "retention")
    ring_cfg = dict(ring=True, bn=OUT_PROJ_BN_RING) if cast else {}
    x1, x1g, x1_ss = gemm("out_proj", [hm, hr], epilogue="residual_prenorm", x=x2d,
                          norm_g=w["g_ffn"], **ring_cfg)
    if cast:
        act = gemm("ffn_up", [x1g], epilogue="rownorm_relu2", row_ss=x1_ss, out_dtype=BF16,
                   ring=True)
        wb_out = new_wb
    else:
        act, w_down = gemm("ffn_up", [x1g], epilogue="rownorm_relu2", row_ss=x1_ss,
                           out_dtype=BF16, side_cast=w["ffn_down"])
        wb_out = {**wb, "ffn_down": w_down}
    return x1, act, (c_new, n_new, m_slab, s_new), wb_out


def _layer_back(x1, act, w_down, g_final, final_norm):
    y, = _gemm([act], w_down, name="ffn_down", bm=GEMM_BM, bn=GEMM_BN, bk=FFN_DOWN_BK,
               epilogue="residual", x=x1)
    return _rmsnorm(y, g_final, F32) if final_norm else y


def kernel(x_prompt, x_sample, state_mlstm_C, state_mlstm_n, state_mlstm_m, state_ret_S, w_in, b_igate, b_fgate, g_mlstm_head, g_ret_head, w_out, g_norm_mix, g_norm_ffn, w_up, w_down, g_final):
    depth = w_in.shape[0]
    batch, seq, _ = x_prompt.shape
    dec_batch, dec_seq, _ = x_sample.shape
    assert seq % PROMPT_CHUNK == 0 and SAMPLE_ROWS % dec_seq == 0
    assert (dec_batch * dec_seq) % SAMPLE_ROWS == 0
    seg_s = SAMPLE_ROWS // dec_seq
    nchunks_p = seq // PROMPT_CHUNK

    cos_p, sin_p = _rotary_tables(jnp.arange(seq, dtype=F32) + 0.0)
    pos_s = jnp.arange(dec_seq, dtype=F32) + float(PAST_LEN)
    cos_s, sin_s = _rotary_tables(jnp.tile(pos_s, seg_s))
    tables_p = (cos_p, sin_p) + _retention_tables(PROMPT_CHUNK, 1)
    tables_s = (cos_s, sin_s) + _retention_tables(dec_seq, seg_s)

    yp = x_prompt.reshape(batch * seq, D_MODEL)
    ys = x_sample.reshape(dec_batch * dec_seq, D_MODEL)
    g_fin = g_final.reshape(1, D_MODEL)
    outs = {k: [] for k in ("pC", "pn", "pm", "pS", "sC", "sn", "sm", "sS")}
    for l in range(depth):
        w_in_t = w_in[l].T
        w_gate_t = w_in_t[GATE_OFF:GATE_OFF + 2 * HM]
        w = {
            "in_proj": w_in_t,
            "wg": jnp.pad(w_gate_t, ((0, LANES - 2 * HM), (0, 0))).astype(BF16),
            "gate_bias": jnp.pad(jnp.concatenate([b_igate[l], b_fgate[l]]), (0, LANES - 2 * HM)).reshape(1, LANES),
            "g_mix": g_norm_mix[l].reshape(1, D_MODEL),
            "g_ffn": g_norm_ffn[l].reshape(1, D_MODEL),
            "g_mh": g_mlstm_head[l].reshape(1, HM * DV_M),
            "g_rh": g_ret_head[l].reshape(1, HR * DV_R),
            "out_proj": w_out[l],
            "ffn_up": w_up[l],
            "ffn_down": w_down[l],
        }
        last = l == depth - 1
        m_tok = jnp.pad(jnp.repeat(state_mlstm_m[l], dec_seq, axis=0), ((0, 0), (0, LANES - HM)))
        init = {"m_tok": m_tok, "C": state_mlstm_C[l],
                "n": state_mlstm_n[l].reshape(dec_batch, HM, 1, DK_M), "S": state_ret_S[l]}
        xs1, act_s, (c, n, m_slab, s), wb = _layer_front(
            ys, w, None, tables_s, SAMPLE_ROWS, seg_s, 1, init)
        outs["sC"].append(c)
        outs["sn"].append(n.reshape(dec_batch, HM, DK_M))
        m_seg = m_slab[:, :, ::dec_seq, 0]
        outs["sm"].append(m_seg.transpose(0, 2, 1).reshape(dec_batch, HM))
        outs["sS"].append(s)

        xp1, act_p, (c, n, m_slab, s), wb = _layer_front(
            yp, w, wb, tables_p, PROMPT_CHUNK, 1, nchunks_p, None)
        outs["pC"].append(c)
        outs["pn"].append(n.reshape(batch, HM, DK_M))
        outs["pm"].append(m_slab[:, :, 0, 0])
        outs["pS"].append(s)

        ys = _layer_back(xs1, act_s, wb["ffn_down"], g_fin, last)
        yp = _layer_back(xp1, act_p, wb["ffn_down"], g_fin, last)

    def stack(xs):
        return xs[0][None] if len(xs) == 1 else jnp.stack(xs)

    return (yp.reshape(batch, seq, D_MODEL), ys.reshape(dec_batch, dec_seq, D_MODEL),
            stack(outs["pC"]), stack(outs["pn"]), stack(outs["pm"]), stack(outs["pS"]),
            stack(outs["sC"]), stack(outs["sn"]), stack(outs["sm"]), stack(outs["sS"]))
```
